```python
import jax, jax.numpy as jnp
from jax import lax
import numpy as np

D_MODEL = 4096
BATCH = 1
SEQ = 8192
DEPTH = 1

GRID_W = 64
CTX_LEN = 256
RET_HEADS = D_MODEL // 512
RET_DK = 256
RET_DV = 256
RET_QK_W = RET_HEADS * RET_DK
RET_W = RET_HEADS * RET_DV
RET_CHUNK = 128
ROPE_BASE = 10000.0
SGU_GROUPS = D_MODEL // 512
SGU_GROUP_DIM = 256
SGU_W = SGU_GROUPS * SGU_GROUP_DIM
SGU_CHUNK = 128
MIX_W = RET_W + SGU_W
Q_OFF = 0
K_OFF = Q_OFF + RET_QK_W
V_OFF = K_OFF + RET_QK_W
G_OFF = V_OFF + RET_W
U_OFF = G_OFF + RET_W
SV_OFF = U_OFF + SGU_W
IN_W = SV_OFF + SGU_W
MOE_GROUPS = 8
EXPERTS_PER_GROUP = 8
N_EXPERTS = MOE_GROUPS * EXPERTS_PER_GROUP
MOE_TOP_K = 2
D_EXPERT = D_MODEL // 8
MOE_BLOCK = 128
N_MOD = 6
EPS = 1e-6

kernel_name = 'hybrid_retention_sgu_hmoe_block'


def rms_norm(x, g):
    x32 = x.astype(jnp.float32)
    y = x32 * lax.rsqrt(jnp.mean(x32 * x32, axis=-1, keepdims=True) + EPS)
    return (y * g.astype(jnp.float32)).astype(x.dtype)


def modulate(h, shift, scale):
    return h * (1 + scale) + shift


def split_heads(t, n_heads):
    b, l, _ = t.shape
    return t.reshape(b, l, n_heads, -1).transpose(0, 2, 1, 3)


def rope_tables(n_tokens, dtype):
    n_rows = n_tokens // GRID_W
    rows = jnp.repeat(jnp.arange(n_rows), GRID_W)
    cols = jnp.tile(jnp.arange(GRID_W), n_rows)
    n_freq = RET_DK // 4
    freqs = ROPE_BASE ** (-jnp.arange(n_freq, dtype=jnp.float32) / n_freq)
    pos = jnp.stack([rows, cols], axis=-1).astype(jnp.float32)
    ang = pos[:, :, None, None] * freqs
    ang = jnp.broadcast_to(ang, (n_tokens, 2, 2, n_freq)).reshape(n_tokens, RET_DK)
    return jnp.cos(ang).astype(dtype), jnp.sin(ang).astype(dtype)


def rope_2d(t, cos, sin):
    tr = t.reshape(t.shape[:-1] + (2, 2, RET_DK // 4))
    rot = jnp.stack([-tr[..., 1, :], tr[..., 0, :]], axis=-2).reshape(t.shape)
    return t * cos + rot * sin


def retention_scan(q, k, v, log_gamma, s0, strict):
    b, h, l, _ = q.shape
    dt = q.dtype
    n_chunks = l // RET_CHUNK
    lg = log_gamma.astype(jnp.float32)
    idx = jnp.arange(RET_CHUNK, dtype=jnp.float32)
    diff = idx[:, None] - idx[None, :]
    mask = (diff > 0) if strict else (diff >= 0)
    decay = jnp.where(mask, jnp.exp(jnp.where(mask, diff, 0.0) * lg[:, None, None]), 0.0).astype(dt)
    q_decay = jnp.exp((idx + 1.0) * lg[:, None]).astype(dt)[..., None]
    k_decay = jnp.exp((RET_CHUNK - 1.0 - idx) * lg[:, None]).astype(dt)[..., None]
    chunk_decay = jnp.exp(RET_CHUNK * lg).astype(dt)[:, None, None]

    def to_chunks(t):
        return jnp.moveaxis(t.reshape(b, h, n_chunks, RET_CHUNK, t.shape[-1]), 2, 0)

    def step(s, qkv):
        qc, kc, vc = qkv
        scores = jnp.einsum('bhqd,bhkd->bhqk', qc, kc) * decay
        out = (jnp.einsum('bhqk,bhkv->bhqv', scores, vc)
               + jnp.einsum('bhqd,bhdv->bhqv', qc, s) * q_decay)
        s = s * chunk_decay + jnp.einsum('bhkd,bhkv->bhdv', kc * k_decay, vc)
        return s, out

    _, out = lax.scan(step, s0.astype(dt), (to_chunks(q), to_chunks(k), to_chunks(v)))
    return jnp.moveaxis(out, 0, 2).reshape(b, h, l, v.shape[-1])


def retention_state(k, v, log_gamma, reverse):
    l = k.shape[2]
    pos = jnp.arange(l, dtype=jnp.float32)
    expo = pos if reverse else (l - 1.0 - pos)
    w = jnp.exp(expo[None, :] * log_gamma.astype(jnp.float32)[:, None]).astype(k.dtype)
    return jnp.einsum('hl,bhld,bhle->bhde', w, k, v)


def bidir_retention(q, k, v, lg_f, lg_b, s0_f, s0_b):
    out_f = retention_scan(q, k, v, lg_f, s0_f, False)
    flip = lambda t: jnp.flip(t, axis=2)
    out_b = retention_scan(flip(q), flip(k), flip(v), lg_b, s0_b, True)
    return out_f + flip(out_b)


def spatial_gating(u, v, ln_g, ln_b, w_s, b_s):
    b, l, _ = u.shape
    u = jax.nn.gelu(u)
    v32 = jax.nn.gelu(v).astype(jnp.float32).reshape(b, l, SGU_GROUPS, SGU_GROUP_DIM)
    mu = jnp.mean(v32, axis=-1, keepdims=True)
    var = jnp.mean(jnp.square(v32 - mu), axis=-1, keepdims=True)
    vn = ((v32 - mu) * lax.rsqrt(var + EPS)).astype(u.dtype)
    vn = vn * ln_g.reshape(SGU_GROUPS, SGU_GROUP_DIM) + ln_b.reshape(SGU_GROUPS, SGU_GROUP_DIM)
    vn = vn.reshape(b, l // SGU_CHUNK, SGU_CHUNK, SGU_GROUPS, SGU_GROUP_DIM)
    mixed = jnp.einsum('gpq,bnqgc->bnpgc', w_s, vn) + b_s.T[:, :, None]
    return u * mixed.reshape(b, l, SGU_W)


def token_mixer(p, rope, lg_f, lg_b, s0_f, s0_b, gn_g, ln_g, ln_b, w_s, b_s):
    b, l, _ = p.shape
    q = split_heads(p[..., Q_OFF:K_OFF], RET_HEADS)
    k = split_heads(p[..., K_OFF:V_OFF], RET_HEADS) * RET_DK ** -0.5
    v = split_heads(p[..., V_OFF:G_OFF], RET_HEADS)
    if rope is not None:
        q = rope_2d(q, *rope)
        k = rope_2d(k, *rope)
    y32 = bidir_retention(q, k, v, lg_f, lg_b, s0_f, s0_b).astype(jnp.float32)
    mu = jnp.mean(y32, axis=-1, keepdims=True)
    var = jnp.mean(jnp.square(y32 - mu), axis=-1, keepdims=True)
    y = ((y32 - mu) * lax.rsqrt(var + EPS)).astype(p.dtype)
    y = y.transpose(0, 2, 1, 3).reshape(b, l, RET_W) * gn_g
    ret_out = jax.nn.silu(p[..., G_OFF:U_OFF]) * y
    sgu_out = spatial_gating(p[..., U_OFF:SV_OFF], p[..., SV_OFF:IN_W], ln_g, ln_b, w_s, b_s)
    return jnp.concatenate([ret_out, sgu_out], axis=-1)


def grouped_experts(hf, e_flat, tok_flat, w_flat, w_gate, w_up, w_down):
    m = e_flat.shape[0]
    n_tok = hf.shape[0]
    n_blocks = -(-m // MOE_BLOCK) + N_EXPERTS
    n_rows = n_blocks * MOE_BLOCK
    order = jnp.argsort(e_flat)
    e_sorted = e_flat[order]
    counts = jax.ops.segment_sum(jnp.ones_like(e_flat), e_flat, num_segments=N_EXPERTS)
    starts = jnp.cumsum(counts) - counts
    padded = (counts + MOE_BLOCK - 1) // MOE_BLOCK * MOE_BLOCK
    pends = jnp.cumsum(padded)
    pstarts = pends - padded
    dest = pstarts[e_sorted] + jnp.arange(m, dtype=e_flat.dtype) - starts[e_sorted]
    row_tok = jnp.zeros((n_rows,), jnp.int32).at[dest].set(tok_flat[order])
    row_w = jnp.zeros((n_rows,), w_flat.dtype).at[dest].set(w_flat[order])
    block_e = jnp.minimum(jnp.searchsorted(pends, jnp.arange(n_blocks) * MOE_BLOCK, side='right'),
                          N_EXPERTS - 1)

    def expert_block(args):
        tok, e = args
        xb = hf[tok]
        hid = jax.nn.silu(xb @ w_gate[e]) * (xb @ w_up[e])
        return hid @ w_down[e]

    out = lax.map(expert_block, (row_tok.reshape(n_blocks, MOE_BLOCK), block_e))
    out = out.reshape(n_rows, -1) * row_w[:, None]
    return jax.ops.segment_sum(out, row_tok, num_segments=n_tok)


def hier_moe(h, w_rg, b_rg, w_re, b_re, w_gate, w_up, w_down):
    b, l, d = h.shape
    n_tok = b * l
    hf = h.reshape(n_tok, d)
    g_logits = (hf @ w_rg + b_rg).astype(jnp.float32)
    g_prob = jax.nn.softmax(g_logits, axis=-1)
    g_sel = jnp.argmax(g_logits, axis=-1).astype(jnp.int32)
    p_g = jnp.take_along_axis(g_prob, g_sel[:, None], axis=-1)
    e_logits = (hf @ w_re + b_re).astype(jnp.float32).reshape(n_tok, MOE_GROUPS, EXPERTS_PER_GROUP)
    e_logits = jnp.take_along_axis(e_logits, g_sel[:, None, None], axis=1)[:, 0]
    top_v, top_i = lax.top_k(e_logits, MOE_TOP_K)
    p_e = jax.nn.softmax(top_v, axis=-1)
    weights = (p_g * p_e).astype(h.dtype)
    expert = g_sel[:, None] * EXPERTS_PER_GROUP + top_i.astype(jnp.int32)
    tok = jnp.repeat(jnp.arange(n_tok, dtype=jnp.int32), MOE_TOP_K)
    y = grouped_experts(hf, expert.reshape(-1), tok, weights.reshape(-1), w_gate, w_up, w_down)
    return y.reshape(b, l, d)


def setup_inputs(seed: int = 0) -> dict:
    key = jax.random.key(seed)
    ks = jax.random.split(key, 25)
    f32 = jnp.float32
    nrm = lambda k, shape, s: jax.random.normal(k, shape, f32) * s
    heads = jnp.arange(RET_HEADS, dtype=f32)
    decay_raw = jnp.log(-jnp.log1p(-jnp.exp2(-5.0 - heads)))
    return {
        'x': nrm(ks[0], (BATCH, SEQ, D_MODEL), 1.0),
        'c': nrm(ks[1], (BATCH, D_MODEL), 1.0),
        'ctx': nrm(ks[2], (BATCH, CTX_LEN, D_MODEL), 1.0),
        'c_ctx': nrm(ks[3], (D_MODEL,), 1.0),
        'w_ada': nrm(ks[4], (DEPTH, D_MODEL, N_MOD * D_MODEL), 0.5 * D_MODEL ** -0.5),
        'b_ada': nrm(ks[5], (DEPTH, N_MOD * D_MODEL), 0.02),
        'norm1_g': 1.0 + nrm(ks[6], (DEPTH, D_MODEL), 0.02),
        'norm2_g': 1.0 + nrm(ks[7], (DEPTH, D_MODEL), 0.02),
        'w_in': nrm(ks[8], (DEPTH, D_MODEL, IN_W), D_MODEL ** -0.5),
        'ret_decay_f': decay_raw + nrm(ks[9], (DEPTH, RET_HEADS), 0.05),
        'ret_decay_b': decay_raw + nrm(ks[10], (DEPTH, RET_HEADS), 0.05),
        'ret_gn_g': 1.0 + nrm(ks[11], (DEPTH, RET_W), 0.02),
        'sgu_ln_g': 1.0 + nrm(ks[12], (DEPTH, SGU_W), 0.02),
        'sgu_ln_b': nrm(ks[13], (DEPTH, SGU_W), 0.02),
        'sgu_w_s': nrm(ks[14], (DEPTH, SGU_GROUPS, SGU_CHUNK, SGU_CHUNK), SGU_CHUNK ** -0.5),
        'sgu_b_s': 1.0 + nrm(ks[15], (DEPTH, SGU_GROUPS, SGU_CHUNK), 0.1),
        'w_out': nrm(ks[16], (DEPTH, MIX_W, D_MODEL), MIX_W ** -0.5),
        'w_router_group': nrm(ks[17], (DEPTH, D_MODEL, MOE_GROUPS), D_MODEL ** -0.5),
        'b_router_group': nrm(ks[18], (DEPTH, MOE_GROUPS), 0.01),
        'w_router_expert': nrm(ks[19], (DEPTH, D_MODEL, N_EXPERTS), D_MODEL ** -0.5),
        'b_router_expert': nrm(ks[20], (DEPTH, N_EXPERTS), 0.01),
        'w_gate': nrm(ks[21], (DEPTH, N_EXPERTS, D_MODEL, D_EXPERT), D_MODEL ** -0.5),
        'w_up': nrm(ks[22], (DEPTH, N_EXPERTS, D_MODEL, D_EXPERT), D_MODEL ** -0.5),
        'w_down': nrm(ks[23], (DEPTH, N_EXPERTS, D_EXPERT, D_MODEL), D_EXPERT ** -0.5),
        'final_g': 1.0 + nrm(ks[24], (D_MODEL,), 0.02),
    }


def reference(x, c, ctx, c_ctx, w_ada, b_ada, norm1_g, norm2_g, w_in, ret_decay_f, ret_decay_b,
              ret_gn_g, sgu_ln_g, sgu_ln_b, sgu_w_s, sgu_b_s, w_out, w_router_group, b_router_group,
              w_router_expert, b_router_expert, w_gate, w_up, w_down, final_g):
    n_lat = x.shape[1]
    rope = rope_tables(n_lat, x.dtype)
    silu_c = jax.nn.silu(c)
    silu_cc = jax.nn.silu(c_ctx)
    for layer in range(DEPTH):
        mod = (silu_c @ w_ada[layer] + b_ada[layer])[:, None, :]
        sh1, sc1, g1, sh2, sc2, g2 = jnp.split(mod, N_MOD, axis=-1)
        mod_c = silu_cc @ w_ada[layer] + b_ada[layer]
        csh1, csc1, cg1, csh2, csc2, cg2 = jnp.split(mod_c, N_MOD, axis=-1)
        lg_f = -jnp.exp(ret_decay_f[layer])
        lg_b = -jnp.exp(ret_decay_b[layer])
        mixer_params = (ret_gn_g[layer], sgu_ln_g[layer], sgu_ln_b[layer], sgu_w_s[layer], sgu_b_s[layer])
        moe_params = (w_router_group[layer], b_router_group[layer], w_router_expert[layer],
                      b_router_expert[layer], w_gate[layer], w_up[layer], w_down[layer])

        hc = modulate(rms_norm(ctx, norm1_g[layer]), csh1, csc1)
        kc = split_heads(hc @ w_in[layer, :, K_OFF:V_OFF], RET_HEADS) * RET_DK ** -0.5
        vc = split_heads(hc @ w_in[layer, :, V_OFF:G_OFF], RET_HEADS)
        s_f = retention_state(kc, vc, lg_f, False)
        s_b = retention_state(kc, vc, lg_b, True)

        h = modulate(rms_norm(x, norm1_g[layer]), sh1, sc1)
        mix = token_mixer(h @ w_in[layer], rope, lg_f, lg_b, s_f, s_b, *mixer_params)
        x = x + g1 * (mix @ w_out[layer])
        h = modulate(rms_norm(x, norm2_g[layer]), sh2, sc2)
        x = x + g2 * hier_moe(h, *moe_params)

        if layer < DEPTH - 1:
            zero_state = jnp.zeros_like(s_f)
            mix_c = token_mixer(hc @ w_in[layer], None, lg_f, lg_b, zero_state, zero_state, *mixer_params)
            ctx = ctx + cg1 * (mix_c @ w_out[layer])
            hc2 = modulate(rms_norm(ctx, norm2_g[layer]), csh2, csc2)
            ctx = ctx + cg2 * hier_moe(hc2, *moe_params)
    return rms_norm(x, final_g)
```

```python
import functools

import jax
import jax.numpy as jnp
from jax import lax
from jax.experimental import pallas as pl
from jax.experimental.pallas import tpu as pltpu

F32 = jnp.float32
BF16 = jnp.bfloat16

GRID_W = 64
RET_DK = 256
RET_DV = 256
RET_CHUNK = 128
ROPE_BASE = 10000.0
SGU_GROUP_DIM = 256
SGU_CHUNK = 128
MOE_GROUPS = 8
EXPERTS_PER_GROUP = 8
N_EXPERTS = MOE_GROUPS * EXPERTS_PER_GROUP
N_MOD = 6
EPS = 1e-6

V7X_LANES = 128
V7X_VMEM_BYTES = 64 * 1024 * 1024
MOD_ROWS = 8

ADA_TN = 512
MM_TM = 1024
MM_TN = 512
ROW_TILE = 256
SEQ_TILE = 512
EXPERT_ROWS = 256
EXPERT_SPLIT = 2


def _vmem_limit(block_bytes, scratch_bytes):
    want = 2 * block_bytes + scratch_bytes
    return int(min(V7X_VMEM_BYTES - 4 * 1024 * 1024, max(2 * want, 32 * 1024 * 1024)))


def _nbytes(shape, dtype):
    n = 1
    for s in shape:
        n *= s
    return n * jnp.dtype(dtype).itemsize


def _ada_kernel(cc_ref, w_ref, b_ref, o_ref):
    a = cc_ref[...]
    s = (a * jax.nn.sigmoid(a)).astype(BF16)
    o_ref[...] = jnp.dot(s, w_ref[...].astype(BF16), preferred_element_type=F32) + b_ref[...]


def _ada(cc, w, b):
    d, n = w.shape
    blocks = _nbytes((MOD_ROWS, d), F32) + _nbytes((d, ADA_TN), F32) + 2 * _nbytes((MOD_ROWS, ADA_TN), F32)
    return pl.pallas_call(
        _ada_kernel,
        grid=(n // ADA_TN,),
        in_specs=[pl.BlockSpec((MOD_ROWS, d), lambda j: (0, 0)),
                  pl.BlockSpec((d, ADA_TN), lambda j: (0, j)),
                  pl.BlockSpec((1, ADA_TN), lambda j: (0, j))],
        out_specs=pl.BlockSpec((MOD_ROWS, ADA_TN), lambda j: (0, j)),
        out_shape=jax.ShapeDtypeStruct((MOD_ROWS, n), F32),
        compiler_params=pltpu.CompilerParams(
            dimension_semantics=("arbitrary",),
            vmem_limit_bytes=_vmem_limit(blocks, _nbytes((d, ADA_TN), BF16))),
        name="ada",
    )(cc, w, b)


def _rms_mod(x, g, shift, scale):
    y = x * lax.rsqrt(jnp.mean(x * x, axis=-1, keepdims=True) + EPS) * g
    return y * (1.0 + scale) + shift


def _norm_mod_kernel(x_ref, g_ref, sh_ref, sc_ref, o_ref, *, row):
    h = _rms_mod(x_ref[...], g_ref[...], sh_ref[row:row + 1, :], sc_ref[row:row + 1, :])
    o_ref[...] = h.astype(o_ref.dtype)


def _norm_mod(x, g, mod, row, shift_chunk, scale_chunk, tm):
    m, d = x.shape
    blocks = _nbytes((tm, d), F32) * 2 + 3 * _nbytes((MOD_ROWS, d), F32)
    return pl.pallas_call(
        functools.partial(_norm_mod_kernel, row=row),
        grid=(m // tm,),
        in_specs=[pl.BlockSpec((tm, d), lambda i: (i, 0)),
                  pl.BlockSpec((1, d), lambda i: (0, 0)),
                  pl.BlockSpec((MOD_ROWS, d), lambda i: (0, shift_chunk)),
                  pl.BlockSpec((MOD_ROWS, d), lambda i: (0, scale_chunk))],
        out_specs=pl.BlockSpec((tm, d), lambda i: (i, 0)),
        out_shape=jax.ShapeDtypeStruct((m, d), BF16),
        compiler_params=pltpu.CompilerParams(
            dimension_semantics=("arbitrary",),
            vmem_limit_bytes=_vmem_limit(blocks, 0)),
        name="norm_mod",
    )(x, g, mod, mod)


def _matmul_acc(a_refs, w_ref, wbf_ref):
    @pl.when(pl.program_id(1) == 0)
    def _():
        wbf_ref[...] = w_ref[...].astype(BF16)

    acc = None
    k0 = 0
    for a_ref in a_refs:
        kk = a_ref.shape[1]
        part = jnp.dot(a_ref[...], wbf_ref[k0:k0 + kk, :], preferred_element_type=F32)
        acc = part if acc is None else acc + part
        k0 += kk
    return acc


def _matmul_kernel(*refs, n_a):
    w_ref, o_ref, wbf_ref = refs[n_a:]
    o_ref[...] = _matmul_acc(refs[:n_a], w_ref, wbf_ref)


def _matmul_res_kernel(*refs, n_a):
    w_ref, x_ref, g_ref, o_ref, wbf_ref = refs[n_a:]
    o_ref[...] = x_ref[...] + g_ref[0:1, :] * _matmul_acc(refs[:n_a], w_ref, wbf_ref)


def _matmul(a_list, w, n_cols, col_off, tm, tn, residual=None):
    m = a_list[0].shape[0]
    k = sum(a.shape[1] for a in a_list)
    n_a = len(a_list)
    joff = col_off // tn
    grid = (n_cols // tn, m // tm)
    in_specs = [pl.BlockSpec((tm, a.shape[1]), lambda j, i: (i, 0)) for a in a_list]
    in_specs.append(pl.BlockSpec((k, tn), lambda j, i: (0, j + joff)))
    args = list(a_list) + [w]
    blocks = _nbytes((tm, k), BF16) + _nbytes((k, tn), F32) + _nbytes((tm, tn), F32)
    if residual is None:
        body = functools.partial(_matmul_kernel, n_a=n_a)
    else:
        x, mod, gate_chunk = residual
        goff = gate_chunk * (n_cols // tn)
        in_specs += [pl.BlockSpec((tm, tn), lambda j, i: (i, j)),
                     pl.BlockSpec((MOD_ROWS, tn), lambda j, i: (0, goff + j))]
        args += [x, mod]
        blocks += _nbytes((tm, tn), F32) + _nbytes((MOD_ROWS, tn), F32)
        body = functools.partial(_matmul_res_kernel, n_a=n_a)
    return pl.pallas_call(
        body,
        grid=grid,
        in_specs=in_specs,
        out_specs=pl.BlockSpec((tm, tn), lambda j, i: (i, j)),
        out_shape=jax.ShapeDtypeStruct((m, n_cols), F32),
        scratch_shapes=[pltpu.VMEM((k, tn), BF16)],
        compiler_params=pltpu.CompilerParams(
            dimension_semantics=("arbitrary", "arbitrary"),
            vmem_limit_bytes=_vmem_limit(blocks, _nbytes((k, tn), BF16))),
        name="matmul_res" if residual is not None else "matmul",
    )(*args)


def _ctx_state_kernel(lgf_ref, lgb_ref, k_ref, v_ref, sf_ref, sb_ref):
    h = pl.program_id(0)
    n = k_ref.shape[0]
    pos = lax.broadcasted_iota(jnp.int32, (n, 1), 0).astype(F32)
    k = k_ref[...] * (RET_DK ** -0.5)
    v = v_ref[...].astype(BF16)
    wf = jnp.exp((n - 1.0 - pos) * lgf_ref[h])
    wb = jnp.exp(pos * lgb_ref[h])
    tn_dims = (((0,), (0,)), ((), ()))
    sf_ref[0] = lax.dot_general((k * wf).astype(BF16), v, tn_dims, preferred_element_type=F32)
    sb_ref[0] = lax.dot_general((k * wb).astype(BF16), v, tn_dims, preferred_element_type=F32)


def _ctx_state(kv, lg_f, lg_b, n_heads):
    n = kv.shape[0]
    smem = pl.BlockSpec(memory_space=pltpu.SMEM)
    st = jax.ShapeDtypeStruct((n_heads, RET_DK, RET_DV), F32)
    return pl.pallas_call(
        _ctx_state_kernel,
        grid=(n_heads,),
        in_specs=[smem, smem,
                  pl.BlockSpec((n, RET_DK), lambda h: (0, h)),
                  pl.BlockSpec((n, RET_DV), lambda h: (0, n_heads + h))],
        out_specs=[pl.BlockSpec((1, RET_DK, RET_DV), lambda h: (h, 0, 0)),
                   pl.BlockSpec((1, RET_DK, RET_DV), lambda h: (h, 0, 0))],
        out_shape=[st, st],
        compiler_params=pltpu.CompilerParams(dimension_semantics=("arbitrary",)),
        name="ctx_state",
    )(lg_f, lg_b, kv, kv)


def _rope(t, cos, sin_signed):
    half = RET_DK // 2
    rot = jnp.concatenate([pltpu.roll(t[:, :half], half // 2, 1),
                           pltpu.roll(t[:, half:], half // 2, 1)], axis=1)
    return t * cos + rot * sin_signed


def _retention_kernel(lgf_ref, lgb_ref, q_ref, k_ref, v_ref, g_ref, cos_ref, sin_ref,
                      s0f_ref, s0b_ref, gn_ref, o_ref, state_ref, yb_ref, *, n_steps):
    h = pl.program_id(0)
    p = pl.program_id(1)
    s = pl.program_id(2)
    c = RET_CHUNK
    n_sub = q_ref.shape[0] // c
    ii = lax.broadcasted_iota(jnp.int32, (c, c), 0)
    jj = lax.broadcasted_iota(jnp.int32, (c, c), 1)
    idx = lax.broadcasted_iota(jnp.int32, (c, 1), 0).astype(F32)
    nt_dims = (((1,), (1,)), ((), ()))
    tn_dims = (((0,), (0,)), ((), ()))

    def chunk(ci, decay, q_decay, k_decay, chunk_decay):
        rows = pl.ds(ci * c, c)
        cos = cos_ref[rows, :]
        sin = sin_ref[rows, :]
        q = _rope(q_ref[rows, :], cos, sin)
        k = _rope(k_ref[rows, :] * (RET_DK ** -0.5), cos, sin)
        v = v_ref[rows, :].astype(BF16)
        qb = q.astype(BF16)
        st = state_ref[...]
        scores = lax.dot_general(qb, k.astype(BF16), nt_dims, preferred_element_type=F32) * decay
        out = (jnp.dot(scores.astype(BF16), v, preferred_element_type=F32)
               + jnp.dot(qb, st.astype(BF16), preferred_element_type=F32) * q_decay)
        state_ref[...] = st * chunk_decay + lax.dot_general(
            (k * k_decay).astype(BF16), v, tn_dims, preferred_element_type=F32)
        return out

    @pl.when(p == 0)
    def _backward():
        lg = lgb_ref[h]
        blk = n_steps - 1 - s

        @pl.when(s == 0)
        def _():
            state_ref[...] = s0b_ref[0]

        mask = jj > ii
        decay = jnp.where(mask, jnp.exp(jnp.where(mask, jj - ii, 0).astype(F32) * lg), 0.0)
        q_decay = jnp.exp((c - idx) * lg)
        k_decay = jnp.exp(idx * lg)
        chunk_decay = jnp.exp(jnp.full((1, RET_DV), c, F32) * lg)
        for ci in reversed(range(n_sub)):
            out = chunk(ci, decay, q_decay, k_decay, chunk_decay)
            start = pl.multiple_of(blk * (n_sub * c) + ci * c, c)
            yb_ref[pl.ds(start, c), :] = out

    @pl.when(p == 1)
    def _forward():
        lg = lgf_ref[h]

        @pl.when(s == 0)
        def _():
            state_ref[...] = s0f_ref[0]

        mask = ii >= jj
        decay = jnp.where(mask, jnp.exp(jnp.where(mask, ii - jj, 0).astype(F32) * lg), 0.0)
        q_decay = jnp.exp((idx + 1.0) * lg)
        k_decay = jnp.exp((c - 1.0 - idx) * lg)
        chunk_decay = jnp.exp(jnp.full((1, RET_DV), c, F32) * lg)
        for ci in range(n_sub):
            out = chunk(ci, decay, q_decay, k_decay, chunk_decay)
            start = pl.multiple_of(s * (n_sub * c) + ci * c, c)
            y = out + yb_ref[pl.ds(start, c), :]
            mu = jnp.mean(y, axis=-1, keepdims=True)
            var = jnp.mean(jnp.square(y - mu), axis=-1, keepdims=True)
            yn = (y - mu) * lax.rsqrt(var + EPS) * gn_ref[...]
            gate = g_ref[pl.ds(ci * c, c), :]
            o_ref[pl.ds(ci * c, c), :] = (gate * jax.nn.sigmoid(gate) * yn).astype(o_ref.dtype)


def _retention(p, cos, sin_signed, lg_f, lg_b, s0_f, s0_b, gn_g, n_heads):
    n_tok = p.shape[0]
    tl = SEQ_TILE
    n_steps = n_tok // tl
    smem = pl.BlockSpec(memory_space=pltpu.SMEM)

    def seq_blk(p_, s_):
        return jnp.where(p_ == 0, n_steps - 1 - s_, s_)

    blocks = 4 * _nbytes((tl, RET_DK), F32) + 2 * _nbytes((tl, RET_DK), F32) \
        + 2 * _nbytes((RET_DK, RET_DV), F32) + _nbytes((tl, RET_DV), BF16)
    scratch = _nbytes((RET_DK, RET_DV), F32) + _nbytes((n_tok, RET_DV), F32)
    return pl.pallas_call(
        functools.partial(_retention_kernel, n_steps=n_steps),
        grid=(n_heads, 2, n_steps),
        in_specs=[smem, smem,
                  pl.BlockSpec((tl, RET_DK), lambda h, p_, s_: (seq_blk(p_, s_), h)),
                  pl.BlockSpec((tl, RET_DK), lambda h, p_, s_: (seq_blk(p_, s_), n_heads + h)),
                  pl.BlockSpec((tl, RET_DV), lambda h, p_, s_: (seq_blk(p_, s_), 2 * n_heads + h)),
                  pl.BlockSpec((tl, RET_DV), lambda h, p_, s_: (s_ * p_, 3 * n_heads + h)),
                  pl.BlockSpec((tl, RET_DK), lambda h, p_, s_: (seq_blk(p_, s_), 0)),
                  pl.BlockSpec((tl, RET_DK), lambda h, p_, s_: (seq_blk(p_, s_), 0)),
                  pl.BlockSpec((1, RET_DK, RET_DV), lambda h, p_, s_: (h, 0, 0)),
                  pl.BlockSpec((1, RET_DK, RET_DV), lambda h, p_, s_: (h, 0, 0)),
                  pl.BlockSpec((1, RET_DV), lambda h, p_, s_: (0, h))],
        out_specs=pl.BlockSpec((tl, RET_DV), lambda h, p_, s_: (s_ * p_, h)),
        out_shape=jax.ShapeDtypeStruct((n_tok, n_heads * RET_DV), BF16),
        scratch_shapes=[pltpu.VMEM((RET_DK, RET_DV), F32), pltpu.VMEM((n_tok, RET_DV), F32)],
        compiler_params=pltpu.CompilerParams(
            dimension_semantics=("arbitrary", "arbitrary", "arbitrary"),
            vmem_limit_bytes=_vmem_limit(blocks, scratch)),
        name="retention",
    )(lg_f, lg_b, p, p, p, p, cos, sin_signed, s0_f, s0_b, gn_g)


def _sgu_kernel(u_ref, sv_ref, lng_ref, lnb_ref, ws_ref, bs_ref, o_ref):
    c = SGU_CHUNK
    ws = ws_ref[0].astype(BF16)
    bs = bs_ref[0]
    for ci in range(u_ref.shape[0] // c):
        rows = pl.ds(ci * c, c)
        v32 = jax.nn.gelu(sv_ref[rows, :])
        mu = jnp.mean(v32, axis=-1, keepdims=True)
        var = jnp.mean(jnp.square(v32 - mu), axis=-1, keepdims=True)
        vn = (v32 - mu) * lax.rsqrt(var + EPS) * lng_ref[...] + lnb_ref[...]
        mixed = jnp.dot(ws, vn.astype(BF16), preferred_element_type=F32) + bs
        o_ref[rows, :] = (jax.nn.gelu(u_ref[rows, :]) * mixed).astype(o_ref.dtype)


def _sgu(p, ln_g, ln_b, w_s, b_s, u_off):
    n_tok = p.shape[0]
    n_groups = w_s.shape[0]
    tl = SEQ_TILE
    gd = SGU_GROUP_DIM
    ub = u_off // gd
    blocks = 2 * _nbytes((tl, gd), F32) + _nbytes((tl, gd), BF16) + 2 * _nbytes((SGU_CHUNK, V7X_LANES), F32)
    return pl.pallas_call(
        _sgu_kernel,
        grid=(n_tok // tl, n_groups),
        in_specs=[pl.BlockSpec((tl, gd), lambda i, g: (i, ub + g)),
                  pl.BlockSpec((tl, gd), lambda i, g: (i, ub + n_groups + g)),
                  pl.BlockSpec((1, gd), lambda i, g: (0, g)),
                  pl.BlockSpec((1, gd), lambda i, g: (0, g)),
                  pl.BlockSpec((1, SGU_CHUNK, SGU_CHUNK), lambda i, g: (g, 0, 0)),
                  pl.BlockSpec((1, SGU_CHUNK, 1), lambda i, g: (g, 0, 0))],
        out_specs=pl.BlockSpec((tl, gd), lambda i, g: (i, g)),
        out_shape=jax.ShapeDtypeStruct((n_tok, n_groups * gd), BF16),
        compiler_params=pltpu.CompilerParams(
            dimension_semantics=("arbitrary", "arbitrary"),
            vmem_limit_bytes=_vmem_limit(blocks, 0)),
        name="sgu",
    )(p, p, ln_g, ln_b, w_s, b_s)


def _split_bf16(t):
    hi = t.astype(BF16)
    lo = (t - hi.astype(F32)).astype(BF16)
    return hi, lo


def _router_kernel(x_ref, g_ref, sh_ref, sc_ref, wr_ref, br_ref, h_ref, eid_ref, ew_ref):
    h = _rms_mod(x_ref[...], g_ref[...], sh_ref[0:1, :], sc_ref[0:1, :])
    h_ref[...] = h
    h_hi, h_lo = _split_bf16(h)
    w_hi, w_lo = _split_bf16(wr_ref[...])
    logits = (jnp.dot(h_hi, w_hi, preferred_element_type=F32)
              + jnp.dot(h_lo, w_hi, preferred_element_type=F32)
              + jnp.dot(h_hi, w_lo, preferred_element_type=F32)) + br_ref[...]
    lane = lax.broadcasted_iota(jnp.int32, logits.shape, 1)
    lane_f = lane.astype(F32)
    neg = -jnp.inf

    def first_lane(hit):
        return jnp.min(jnp.where(hit, lane_f, float(V7X_LANES)), axis=-1, keepdims=True).astype(jnp.int32)

    gl = jnp.where(lane < MOE_GROUPS, logits, neg)
    g_max = jnp.max(gl, axis=-1, keepdims=True)
    g_sel = first_lane(gl == g_max)
    p_g = 1.0 / jnp.sum(jnp.exp(gl - g_max), axis=-1, keepdims=True)
    e_lo = MOE_GROUPS + g_sel * EXPERTS_PER_GROUP
    el = jnp.where((lane >= e_lo) & (lane < e_lo + EXPERTS_PER_GROUP), logits, neg)
    v1 = jnp.max(el, axis=-1, keepdims=True)
    i1 = first_lane(el == v1)
    el2 = jnp.where(lane == i1, neg, el)
    v2 = jnp.max(el2, axis=-1, keepdims=True)
    i2 = first_lane(el2 == v2)
    e2 = jnp.exp(v2 - v1)
    den = 1.0 + e2
    w1 = p_g * (1.0 / den)
    w2 = p_g * (e2 / den)
    eid_ref[...] = jnp.where(lane == 0, i1 - MOE_GROUPS, jnp.where(lane == 1, i2 - MOE_GROUPS, 0))
    ew_ref[...] = jnp.where(lane == 0, w1, jnp.where(lane == 1, w2, 0.0))


def _router(x, g, mod, shift_chunk, scale_chunk, wr, br):
    m, d = x.shape
    tm = ROW_TILE
    blocks = 2 * _nbytes((tm, d), F32) + 2 * _nbytes((MOD_ROWS, d), F32) + _nbytes((d, V7X_LANES), F32) \
        + 2 * _nbytes((tm, V7X_LANES), F32)
    return pl.pallas_call(
        _router_kernel,
        grid=(m // tm,),
        in_specs=[pl.BlockSpec((tm, d), lambda i: (i, 0)),
                  pl.BlockSpec((1, d), lambda i: (0, 0)),
                  pl.BlockSpec((MOD_ROWS, d), lambda i: (0, shift_chunk)),
                  pl.BlockSpec((MOD_ROWS, d), lambda i: (0, scale_chunk)),
                  pl.BlockSpec((d, V7X_LANES), lambda i: (0, 0)),
                  pl.BlockSpec((1, V7X_LANES), lambda i: (0, 0))],
        out_specs=[pl.BlockSpec((tm, d), lambda i: (i, 0)),
                   pl.BlockSpec((tm, V7X_LANES), lambda i: (i, 0)),
                   pl.BlockSpec((tm, V7X_LANES), lambda i: (i, 0))],
        out_shape=[jax.ShapeDtypeStruct((m, d), F32),
                   jax.ShapeDtypeStruct((m, V7X_LANES), jnp.int32),
                   jax.ShapeDtypeStruct((m, V7X_LANES), F32)],
        compiler_params=pltpu.CompilerParams(
            dimension_semantics=("arbitrary",),
            vmem_limit_bytes=_vmem_limit(blocks, 0)),
        name="router",
    )(x, g, mod, mod, wr, br)


def _row_copy(src, src_row, dst, dst_row, sem):
    return pltpu.make_async_copy(src.at[pl.ds(src_row, 1), :], dst.at[pl.ds(dst_row, 1), :], sem)


def _experts_kernel(be_ref, nact_ref, rt_ref, rd_ref, h_hbm, wg_ref, wu_ref, wd_ref, rw_ref, y_hbm,
                    x_ref, acc_ref, sem_in, sem_out):
    del be_ref
    b = pl.program_id(0)
    f = pl.program_id(1)
    rows = x_ref.shape[0]
    base = b * rows

    @pl.when(b < nact_ref[0])
    def _active():
        @pl.when(f == 0)
        def _gather():
            def start(r, carry):
                _row_copy(h_hbm, rt_ref[base + r], x_ref, r, sem_in).start()
                return carry

            def wait(r, carry):
                _row_copy(h_hbm, rt_ref[base + r], x_ref, r, sem_in).wait()
                return carry

            lax.fori_loop(0, rows, start, 0)
            lax.fori_loop(0, rows, wait, 0)

        x = x_ref[...].astype(BF16)
        gate = jnp.dot(x, wg_ref[0].astype(BF16), preferred_element_type=F32)
        up = jnp.dot(x, wu_ref[0].astype(BF16), preferred_element_type=F32)
        hid = (gate * jax.nn.sigmoid(gate) * up).astype(BF16)
        part = jnp.dot(hid, wd_ref[0].astype(BF16), preferred_element_type=F32)

        @pl.when(f == 0)
        def _():
            acc_ref[...] = part

        @pl.when((f > 0) & (f < EXPERT_SPLIT - 1))
        def _():
            acc_ref[...] += part

        @pl.when(f == EXPERT_SPLIT - 1)
        def _scatter():
            acc_ref[...] = (acc_ref[...] + part) * rw_ref[...]

            def start(r, carry):
                dst = rd_ref[base + r]

                @pl.when(dst >= 0)
                def _():
                    _row_copy(acc_ref, r, y_hbm, dst, sem_out).start()
                return carry

            def wait(r, carry):
                dst = rd_ref[base + r]

                @pl.when(dst >= 0)
                def _():
                    _row_copy(acc_ref, r, y_hbm, dst, sem_out).wait()
                return carry

            lax.fori_loop(0, rows, start, 0)
            lax.fori_loop(0, rows, wait, 0)


def _experts(h, w_gate, w_up, w_down, block_e, n_active, row_tok, row_dst, row_w, top_k):
    n_tok, d = h.shape
    de = w_gate.shape[-1]
    des = de // EXPERT_SPLIT
    n_rows = row_tok.shape[0]
    n_blocks = n_rows // EXPERT_ROWS
    assert EXPERT_SPLIT >= 2

    def blk(b, nact):
        return jnp.minimum(b, nact[0] - 1)

    def fs(b, f, nact):
        return jnp.where(b < nact[0], f, EXPERT_SPLIT - 1)

    blocks = 3 * _nbytes((d, des), F32) + _nbytes((EXPERT_ROWS, V7X_LANES), F32)
    scratch = 2 * _nbytes((EXPERT_ROWS, d), F32)
    grid_spec = pltpu.PrefetchScalarGridSpec(
        num_scalar_prefetch=4,
        grid=(n_blocks, EXPERT_SPLIT),
        in_specs=[pl.BlockSpec(memory_space=pl.ANY),
                  pl.BlockSpec((1, d, des), lambda b, f, be, nact, rt, rd: (be[b], 0, fs(b, f, nact))),
                  pl.BlockSpec((1, d, des), lambda b, f, be, nact, rt, rd: (be[b], 0, fs(b, f, nact))),
                  pl.BlockSpec((1, des, d), lambda b, f, be, nact, rt, rd: (be[b], fs(b, f, nact), 0)),
                  pl.BlockSpec((EXPERT_ROWS, 1), lambda b, f, be, nact, rt, rd: (blk(b, nact), 0))],
        out_specs=pl.BlockSpec(memory_space=pl.ANY),
        scratch_shapes=[pltpu.VMEM((EXPERT_ROWS, d), F32), pltpu.VMEM((EXPERT_ROWS, d), F32),
                        pltpu.SemaphoreType.DMA(()), pltpu.SemaphoreType.DMA(())],
    )
    return pl.pallas_call(
        _experts_kernel,
        grid_spec=grid_spec,
        out_shape=jax.ShapeDtypeStruct((top_k * n_tok, d), F32),
        compiler_params=pltpu.CompilerParams(
            dimension_semantics=("arbitrary", "arbitrary"),
            vmem_limit_bytes=_vmem_limit(blocks, scratch)),
        name="experts",
    )(block_e, n_active, row_tok, row_dst, h, w_gate, w_up, w_down, row_w)


def _combine_kernel(*refs, top_k):
    y_refs = refs[:top_k]
    x_ref, g_ref, fg_ref, o_ref = refs[top_k:]
    y = y_refs[0][0]
    for y_ref in y_refs[1:]:
        y = y + y_ref[0]
    x = x_ref[...] + g_ref[0:1, :] * y
    o_ref[...] = x * lax.rsqrt(jnp.mean(x * x, axis=-1, keepdims=True) + EPS) * fg_ref[...]


def _combine(y, x, mod, gate_chunk, final_g, top_k):
    m, d = x.shape
    tm = ROW_TILE
    blocks = (top_k + 2) * _nbytes((tm, d), F32) + _nbytes((MOD_ROWS, d), F32)
    in_specs = [pl.BlockSpec((1, tm, d), functools.partial(lambda k, i: (k, i, 0), k)) for k in range(top_k)]
    in_specs += [pl.BlockSpec((tm, d), lambda i: (i, 0)),
                 pl.BlockSpec((MOD_ROWS, d), lambda i: (0, gate_chunk)),
                 pl.BlockSpec((1, d), lambda i: (0, 0))]
    return pl.pallas_call(
        functools.partial(_combine_kernel, top_k=top_k),
        grid=(m // tm,),
        in_specs=in_specs,
        out_specs=pl.BlockSpec((tm, d), lambda i: (i, 0)),
        out_shape=jax.ShapeDtypeStruct((m, d), F32),
        compiler_params=pltpu.CompilerParams(
            dimension_semantics=("arbitrary",),
            vmem_limit_bytes=_vmem_limit(blocks, 0)),
        name="combine",
    )(*([y] * top_k), x, mod, final_g)


def _rope_tables(n_tokens):
    n_rows = n_tokens // GRID_W
    rows = jnp.repeat(jnp.arange(n_rows), GRID_W)
    cols = jnp.tile(jnp.arange(GRID_W), n_rows)
    n_freq = RET_DK // 4
    freqs = ROPE_BASE ** (-jnp.arange(n_freq, dtype=F32) / n_freq)
    pos = jnp.stack([rows, cols], axis=-1).astype(F32)
    ang = pos[:, :, None, None] * freqs
    ang = jnp.broadcast_to(ang, (n_tokens, 2, 2, n_freq)).reshape(n_tokens, RET_DK)
    sign = jnp.tile(jnp.concatenate([-jnp.ones((n_freq,), F32), jnp.ones((n_freq,), F32)]), 2)
    return jnp.cos(ang), jnp.sin(ang) * sign


def _dispatch(eid, ew, n_tok, top_k):
    m = n_tok * top_k
    n_blocks = -(-m // EXPERT_ROWS) + N_EXPERTS
    n_rows = n_blocks * EXPERT_ROWS
    e_flat = eid[:, :top_k].reshape(-1)
    w_flat = ew[:, :top_k].reshape(-1)
    a_ids = jnp.arange(m, dtype=jnp.int32)
    tok_flat = a_ids // top_k
    dst_flat = (a_ids % top_k) * n_tok + tok_flat
    order = jnp.argsort(e_flat)
    e_sorted = e_flat[order]
    counts = jnp.sum((e_flat[:, None] == jnp.arange(N_EXPERTS, dtype=jnp.int32)[None, :]).astype(jnp.int32), axis=0)
    starts = jnp.cumsum(counts) - counts
    padded = (counts + EXPERT_ROWS - 1) // EXPERT_ROWS * EXPERT_ROWS
    pends = jnp.cumsum(padded)
    pstarts = pends - padded
    dest = pstarts[e_sorted] + a_ids - starts[e_sorted]
    row_tok = jnp.zeros((n_rows,), jnp.int32).at[dest].set(tok_flat[order])
    row_dst = jnp.full((n_rows,), -1, jnp.int32).at[dest].set(dst_flat[order])
    row_w = jnp.zeros((n_rows,), F32).at[dest].set(w_flat[order])
    n_active = (pends[-1] // EXPERT_ROWS).astype(jnp.int32)
    blk_ids = jnp.arange(n_blocks, dtype=jnp.int32)
    block_e = jnp.searchsorted(pends, jnp.minimum(blk_ids, n_active - 1) * EXPERT_ROWS, side='right')
    block_e = jnp.minimum(block_e, N_EXPERTS - 1).astype(jnp.int32)
    return block_e, n_active.reshape(1), row_tok, row_dst, row_w.reshape(n_rows, 1)


def kernel(x, c, ctx, c_ctx, w_ada, b_ada, norm1_g, norm2_g, w_in, ret_decay_f, ret_decay_b, ret_gn_g, sgu_ln_g, sgu_ln_b, sgu_w_s, sgu_b_s, w_out, w_router_group, b_router_group, w_router_expert, b_router_expert, w_gate, w_up, w_down, final_g):
    batch, n_tok, d = x.shape
    assert batch == 1 and w_ada.shape[0] == 1
    n_heads = ret_decay_f.shape[-1]
    n_groups = sgu_w_s.shape[1]
    ret_qk_w = n_heads * RET_DK
    ret_w = n_heads * RET_DV
    sgu_w = n_groups * SGU_GROUP_DIM
    k_off = ret_qk_w
    u_off = 2 * ret_qk_w + 2 * ret_w
    in_w = u_off + 2 * sgu_w
    top_k = 2
    assert w_in.shape == (1, d, in_w) and w_out.shape == (1, ret_w + sgu_w, d)

    cc = jnp.zeros((MOD_ROWS, d), F32).at[0].set(c[0]).at[1].set(c_ctx)
    mod = _ada(cc, w_ada[0], b_ada[0].reshape(1, N_MOD * d))

    lg_f = -jnp.exp(ret_decay_f[0])
    lg_b = -jnp.exp(ret_decay_b[0])

    hc = _norm_mod(ctx[0], norm1_g, mod, 1, 0, 1, ROW_TILE)
    kv_c = _matmul([hc], w_in[0], ret_qk_w + ret_w, k_off, ctx.shape[1], MM_TN)
    s_f, s_b = _ctx_state(kv_c, lg_f, lg_b, n_heads)

    h1 = _norm_mod(x[0], norm1_g, mod, 0, 0, 1, ROW_TILE)
    p = _matmul([h1], w_in[0], in_w, 0, MM_TM, MM_TN)
    cos, sin_signed = _rope_tables(n_tok)
    ret_out = _retention(p, cos, sin_signed, lg_f, lg_b, s_f, s_b, ret_gn_g, n_heads)
    sgu_out = _sgu(p, sgu_ln_g, sgu_ln_b, sgu_w_s[0], sgu_b_s[0].reshape(n_groups, SGU_CHUNK, 1), u_off)
    x1 = _matmul([ret_out, sgu_out], w_out[0], d, 0, MM_TM, MM_TN, residual=(x[0], mod, 2))

    n_router = MOE_GROUPS + N_EXPERTS
    wr = jnp.zeros((d, V7X_LANES), F32).at[:, :MOE_GROUPS].set(w_router_group[0]) \
        .at[:, MOE_GROUPS:n_router].set(w_router_expert[0])
    br = jnp.zeros((1, V7X_LANES), F32).at[0, :MOE_GROUPS].set(b_router_group[0]) \
        .at[0, MOE_GROUPS:n_router].set(b_router_expert[0])
    h2, eid, ew = _router(x1, norm2_g, mod, 3, 4, wr, br)
    block_e, n_active, row_tok, row_dst, row_w = _dispatch(eid, ew, n_tok, top_k)
    y = _experts(h2, w_gate[0], w_up[0], w_down[0], block_e, n_active, row_tok, row_dst, row_w, top_k)
    out = _combine(y.reshape(top_k, n_tok, d), x1, mod, 5, final_g.reshape(1, d), top_k)
    return out.reshape(batch, n_tok, d)
```

```python
import functools

import jax
import jax.numpy as jnp
from jax import lax
from jax.experimental import pallas as pl
from jax.experimental.pallas import tpu as pltpu

F32 = jnp.float32
BF16 = jnp.bfloat16

GRID_W = 64
RET_DK = 256
RET_DV = 256
RET_CHUNK = 128
ROPE_BASE = 10000.0
SGU_GROUP_DIM = 256
SGU_CHUNK = 128
MOE_GROUPS = 8
EXPERTS_PER_GROUP = 8
N_EXPERTS = MOE_GROUPS * EXPERTS_PER_GROUP
N_MOD = 6
EPS = 1e-6

V7X_LANES = 128
V7X_VMEM_BYTES = 64 * 1024 * 1024
MOD_ROWS = 8

ADA_TN = 512
MM_TM = 1024
MM_TN = 512
ROW_TILE = 256
SEQ_TILE = 512
EXPERT_ROWS = 512
EXPERT_SUB = 256
EXPERT_SPLIT = 2
DMA_UNROLL = 8


def _vmem_limit(block_bytes, scratch_bytes):
    want = 2 * block_bytes + scratch_bytes
    return int(min(V7X_VMEM_BYTES - 4 * 1024 * 1024, max(2 * want, 32 * 1024 * 1024)))


def _nbytes(shape, dtype):
    n = 1
    for s in shape:
        n *= s
    return n * jnp.dtype(dtype).itemsize


def _ada_kernel(cc_ref, w_ref, b_ref, o_ref):
    a = cc_ref[...]
    s = (a * jax.nn.sigmoid(a)).astype(BF16)
    o_ref[...] = jnp.dot(s, w_ref[...].astype(BF16), preferred_element_type=F32) + b_ref[...]


def _ada(cc, w, b):
    d, n = w.shape
    blocks = _nbytes((MOD_ROWS, d), F32) + _nbytes((d, ADA_TN), F32) + 2 * _nbytes((MOD_ROWS, ADA_TN), F32)
    return pl.pallas_call(
        _ada_kernel,
        grid=(n // ADA_TN,),
        in_specs=[pl.BlockSpec((MOD_ROWS, d), lambda j: (0, 0)),
                  pl.BlockSpec((d, ADA_TN), lambda j: (0, j)),
                  pl.BlockSpec((1, ADA_TN), lambda j: (0, j))],
        out_specs=pl.BlockSpec((MOD_ROWS, ADA_TN), lambda j: (0, j)),
        out_shape=jax.ShapeDtypeStruct((MOD_ROWS, n), F32),
        compiler_params=pltpu.CompilerParams(
            dimension_semantics=("arbitrary",),
            vmem_limit_bytes=_vmem_limit(blocks, _nbytes((d, ADA_TN), BF16))),
        name="ada",
    )(cc, w, b)


def _rms_mod(x, g, shift, scale):
    y = x * lax.rsqrt(jnp.mean(x * x, axis=-1, keepdims=True) + EPS) * g
    return y * (1.0 + scale) + shift


def _norm_mod_kernel(x_ref, g_ref, sh_ref, sc_ref, o_ref, *, row):
    h = _rms_mod(x_ref[...], g_ref[...], sh_ref[row:row + 1, :], sc_ref[row:row + 1, :])
    o_ref[...] = h.astype(o_ref.dtype)


def _norm_mod(x, g, mod, row, shift_chunk, scale_chunk, tm):
    m, d = x.shape
    blocks = _nbytes((tm, d), F32) * 2 + 3 * _nbytes((MOD_ROWS, d), F32)
    return pl.pallas_call(
        functools.partial(_norm_mod_kernel, row=row),
        grid=(m // tm,),
        in_specs=[pl.BlockSpec((tm, d), lambda i: (i, 0)),
                  pl.BlockSpec((1, d), lambda i: (0, 0)),
                  pl.BlockSpec((MOD_ROWS, d), lambda i: (0, shift_chunk)),
                  pl.BlockSpec((MOD_ROWS, d), lambda i: (0, scale_chunk))],
        out_specs=pl.BlockSpec((tm, d), lambda i: (i, 0)),
        out_shape=jax.ShapeDtypeStruct((m, d), BF16),
        compiler_params=pltpu.CompilerParams(
            dimension_semantics=("arbitrary",),
            vmem_limit_bytes=_vmem_limit(blocks, 0)),
        name="norm_mod",
    )(x, g, mod, mod)


def _matmul_acc(a_refs, w_ref, wbf_ref):
    @pl.when(pl.program_id(1) == 0)
    def _():
        wbf_ref[...] = w_ref[...].astype(BF16)

    acc = None
    k0 = 0
    for a_ref in a_refs:
        kk = a_ref.shape[1]
        part = jnp.dot(a_ref[...], wbf_ref[k0:k0 + kk, :], preferred_element_type=F32)
        acc = part if acc is None else acc + part
        k0 += kk
    return acc


def _matmul_kernel(*refs, n_a):
    w_ref, o_ref, wbf_ref = refs[n_a:]
    o_ref[...] = _matmul_acc(refs[:n_a], w_ref, wbf_ref)


def _matmul_res_kernel(*refs, n_a):
    w_ref, x_ref, g_ref, o_ref, wbf_ref = refs[n_a:]
    o_ref[...] = x_ref[...] + g_ref[0:1, :] * _matmul_acc(refs[:n_a], w_ref, wbf_ref)


def _matmul(a_list, w, n_cols, col_off, tm, tn, residual=None):
    m = a_list[0].shape[0]
    k = sum(a.shape[1] for a in a_list)
    n_a = len(a_list)
    joff = col_off // tn
    grid = (n_cols // tn, m // tm)
    in_specs = [pl.BlockSpec((tm, a.shape[1]), lambda j, i: (i, 0)) for a in a_list]
    in_specs.append(pl.BlockSpec((k, tn), lambda j, i: (0, j + joff)))
    args = list(a_list) + [w]
    blocks = _nbytes((tm, k), BF16) + _nbytes((k, tn), F32) + _nbytes((tm, tn), F32)
    if residual is None:
        body = functools.partial(_matmul_kernel, n_a=n_a)
    else:
        x, mod, gate_chunk = residual
        goff = gate_chunk * (n_cols // tn)
        in_specs += [pl.BlockSpec((tm, tn), lambda j, i: (i, j)),
                     pl.BlockSpec((MOD_ROWS, tn), lambda j, i: (0, goff + j))]
        args += [x, mod]
        blocks += _nbytes((tm, tn), F32) + _nbytes((MOD_ROWS, tn), F32)
        body = functools.partial(_matmul_res_kernel, n_a=n_a)
    return pl.pallas_call(
        body,
        grid=grid,
        in_specs=in_specs,
        out_specs=pl.BlockSpec((tm, tn), lambda j, i: (i, j)),
        out_shape=jax.ShapeDtypeStruct((m, n_cols), F32),
        scratch_shapes=[pltpu.VMEM((k, tn), BF16)],
        compiler_params=pltpu.CompilerParams(
            dimension_semantics=("arbitrary", "arbitrary"),
            vmem_limit_bytes=_vmem_limit(blocks, _nbytes((k, tn), BF16))),
        name="matmul_res" if residual is not None else "matmul",
    )(*args)


def _ctx_state_kernel(lgf_ref, lgb_ref, k_ref, v_ref, sf_ref, sb_ref):
    h = pl.program_id(0)
    n = k_ref.shape[0]
    pos = lax.broadcasted_iota(jnp.int32, (n, 1), 0).astype(F32)
    k = k_ref[...] * (RET_DK ** -0.5)
    v = v_ref[...].astype(BF16)
    wf = jnp.exp((n - 1.0 - pos) * lgf_ref[h])
    wb = jnp.exp(pos * lgb_ref[h])
    tn_dims = (((0,), (0,)), ((), ()))
    sf_ref[0] = lax.dot_general((k * wf).astype(BF16), v, tn_dims, preferred_element_type=F32)
    sb_ref[0] = lax.dot_general((k * wb).astype(BF16), v, tn_dims, preferred_element_type=F32)


def _ctx_state(kv, lg_f, lg_b, n_heads):
    n = kv.shape[0]
    smem = pl.BlockSpec(memory_space=pltpu.SMEM)
    st = jax.ShapeDtypeStruct((n_heads, RET_DK, RET_DV), F32)
    return pl.pallas_call(
        _ctx_state_kernel,
        grid=(n_heads,),
        in_specs=[smem, smem,
                  pl.BlockSpec((n, RET_DK), lambda h: (0, h)),
                  pl.BlockSpec((n, RET_DV), lambda h: (0, n_heads + h))],
        out_specs=[pl.BlockSpec((1, RET_DK, RET_DV), lambda h: (h, 0, 0)),
                   pl.BlockSpec((1, RET_DK, RET_DV), lambda h: (h, 0, 0))],
        out_shape=[st, st],
        compiler_params=pltpu.CompilerParams(dimension_semantics=("arbitrary",)),
        name="ctx_state",
    )(lg_f, lg_b, kv, kv)


def _rope(t, cos, sin_signed):
    half = RET_DK // 2
    rot = jnp.concatenate([pltpu.roll(t[:, :half], half // 2, 1),
                           pltpu.roll(t[:, half:], half // 2, 1)], axis=1)
    return t * cos + rot * sin_signed


def _retention_kernel(lgf_ref, lgb_ref, q_ref, k_ref, v_ref, g_ref, cos_ref, sin_ref,
                      s0f_ref, s0b_ref, gn_ref, o_ref, state_ref, yb_ref, *, n_steps):
    h = pl.program_id(0)
    p = pl.program_id(1)
    s = pl.program_id(2)
    c = RET_CHUNK
    n_sub = q_ref.shape[0] // c
    ii = lax.broadcasted_iota(jnp.int32, (c, c), 0)
    jj = lax.broadcasted_iota(jnp.int32, (c, c), 1)
    idx = lax.broadcasted_iota(jnp.int32, (c, 1), 0).astype(F32)
    nt_dims = (((1,), (1,)), ((), ()))
    tn_dims = (((0,), (0,)), ((), ()))

    def chunk(ci, decay, q_decay, k_decay, chunk_decay):
        rows = pl.ds(ci * c, c)
        cos = cos_ref[rows, :]
        sin = sin_ref[rows, :]
        q = _rope(q_ref[rows, :], cos, sin)
        k = _rope(k_ref[rows, :] * (RET_DK ** -0.5), cos, sin)
        v = v_ref[rows, :].astype(BF16)
        qb = q.astype(BF16)
        st = state_ref[...]
        scores = lax.dot_general(qb, k.astype(BF16), nt_dims, preferred_element_type=F32) * decay
        out = (jnp.dot(scores.astype(BF16), v, preferred_element_type=F32)
               + jnp.dot(qb, st.astype(BF16), preferred_element_type=F32) * q_decay)
        state_ref[...] = st * chunk_decay + lax.dot_general(
            (k * k_decay).astype(BF16), v, tn_dims, preferred_element_type=F32)
        return out

    @pl.when(p == 0)
    def _backward():
        lg = lgb_ref[h]
        blk = n_steps - 1 - s

        @pl.when(s == 0)
        def _():
            state_ref[...] = s0b_ref[0]

        mask = jj > ii
        decay = jnp.where(mask, jnp.exp(jnp.where(mask, jj - ii, 0).astype(F32) * lg), 0.0)
        q_decay = jnp.exp((c - idx) * lg)
        k_decay = jnp.exp(idx * lg)
        chunk_decay = jnp.exp(jnp.full((1, RET_DV), c, F32) * lg)
        for ci in reversed(range(n_sub)):
            out = chunk(ci, decay, q_decay, k_decay, chunk_decay)
            start = pl.multiple_of(blk * (n_sub * c) + ci * c, c)
            yb_ref[pl.ds(start, c), :] = out

    @pl.when(p == 1)
    def _forward():
        lg = lgf_ref[h]

        @pl.when(s == 0)
        def _():
            state_ref[...] = s0f_ref[0]

        mask = ii >= jj
        decay = jnp.where(mask, jnp.exp(jnp.where(mask, ii - jj, 0).astype(F32) * lg), 0.0)
        q_decay = jnp.exp((idx + 1.0) * lg)
        k_decay = jnp.exp((c - 1.0 - idx) * lg)
        chunk_decay = jnp.exp(jnp.full((1, RET_DV), c, F32) * lg)
        for ci in range(n_sub):
            out = chunk(ci, decay, q_decay, k_decay, chunk_decay)
            start = pl.multiple_of(s * (n_sub * c) + ci * c, c)
            y = out + yb_ref[pl.ds(start, c), :]
            mu = jnp.mean(y, axis=-1, keepdims=True)
            var = jnp.mean(jnp.square(y - mu), axis=-1, keepdims=True)
            yn = (y - mu) * lax.rsqrt(var + EPS) * gn_ref[...]
            gate = g_ref[pl.ds(ci * c, c), :]
            o_ref[pl.ds(ci * c, c), :] = (gate * jax.nn.sigmoid(gate) * yn).astype(o_ref.dtype)


def _retention(p, cos, sin_signed, lg_f, lg_b, s0_f, s0_b, gn_g, n_heads):
    n_tok = p.shape[0]
    tl = SEQ_TILE
    n_steps = n_tok // tl
    smem = pl.BlockSpec(memory_space=pltpu.SMEM)

    def seq_blk(p_, s_):
        return jnp.where(p_ == 0, n_steps - 1 - s_, s_)

    blocks = 4 * _nbytes((tl, RET_DK), F32) + 2 * _nbytes((tl, RET_DK), F32) \
        + 2 * _nbytes((RET_DK, RET_DV), F32) + _nbytes((tl, RET_DV), BF16)
    scratch = _nbytes((RET_DK, RET_DV), F32) + _nbytes((n_tok, RET_DV), F32)
    return pl.pallas_call(
        functools.partial(_retention_kernel, n_steps=n_steps),
        grid=(n_heads, 2, n_steps),
        in_specs=[smem, smem,
                  pl.BlockSpec((tl, RET_DK), lambda h, p_, s_: (seq_blk(p_, s_), h)),
                  pl.BlockSpec((tl, RET_DK), lambda h, p_, s_: (seq_blk(p_, s_), n_heads + h)),
                  pl.BlockSpec((tl, RET_DV), lambda h, p_, s_: (seq_blk(p_, s_), 2 * n_heads + h)),
                  pl.BlockSpec((tl, RET_DV), lambda h, p_, s_: (s_ * p_, 3 * n_heads + h)),
                  pl.BlockSpec((tl, RET_DK), lambda h, p_, s_: (seq_blk(p_, s_), 0)),
                  pl.BlockSpec((tl, RET_DK), lambda h, p_, s_: (seq_blk(p_, s_), 0)),
                  pl.BlockSpec((1, RET_DK, RET_DV), lambda h, p_, s_: (h, 0, 0)),
                  pl.BlockSpec((1, RET_DK, RET_DV), lambda h, p_, s_: (h, 0, 0)),
                  pl.BlockSpec((1, RET_DV), lambda h, p_, s_: (0, h))],
        out_specs=pl.BlockSpec((tl, RET_DV), lambda h, p_, s_: (s_ * p_, h)),
        out_shape=jax.ShapeDtypeStruct((n_tok, n_heads * RET_DV), BF16),
        scratch_shapes=[pltpu.VMEM((RET_DK, RET_DV), F32), pltpu.VMEM((n_tok, RET_DV), F32)],
        compiler_params=pltpu.CompilerParams(
            dimension_semantics=("arbitrary", "arbitrary", "arbitrary"),
            vmem_limit_bytes=_vmem_limit(blocks, scratch)),
        name="retention",
    )(lg_f, lg_b, p, p, p, p, cos, sin_signed, s0_f, s0_b, gn_g)


def _sgu_kernel(u_ref, sv_ref, lng_ref, lnb_ref, ws_ref, bs_ref, o_ref):
    c = SGU_CHUNK
    ws = ws_ref[0].astype(BF16)
    bs = bs_ref[0]
    for ci in range(u_ref.shape[0] // c):
        rows = pl.ds(ci * c, c)
        v32 = jax.nn.gelu(sv_ref[rows, :])
        mu = jnp.mean(v32, axis=-1, keepdims=True)
        var = jnp.mean(jnp.square(v32 - mu), axis=-1, keepdims=True)
        vn = (v32 - mu) * lax.rsqrt(var + EPS) * lng_ref[...] + lnb_ref[...]
        mixed = jnp.dot(ws, vn.astype(BF16), preferred_element_type=F32) + bs
        o_ref[rows, :] = (jax.nn.gelu(u_ref[rows, :]) * mixed).astype(o_ref.dtype)


def _sgu(p, ln_g, ln_b, w_s, b_s, u_off):
    n_tok = p.shape[0]
    n_groups = w_s.shape[0]
    tl = SEQ_TILE
    gd = SGU_GROUP_DIM
    ub = u_off // gd
    blocks = 2 * _nbytes((tl, gd), F32) + _nbytes((tl, gd), BF16) + 2 * _nbytes((SGU_CHUNK, V7X_LANES), F32)
    return pl.pallas_call(
        _sgu_kernel,
        grid=(n_tok // tl, n_groups),
        in_specs=[pl.BlockSpec((tl, gd), lambda i, g: (i, ub + g)),
                  pl.BlockSpec((tl, gd), lambda i, g: (i, ub + n_groups + g)),
                  pl.BlockSpec((1, gd), lambda i, g: (0, g)),
                  pl.BlockSpec((1, gd), lambda i, g: (0, g)),
                  pl.BlockSpec((1, SGU_CHUNK, SGU_CHUNK), lambda i, g: (g, 0, 0)),
                  pl.BlockSpec((1, SGU_CHUNK, 1), lambda i, g: (g, 0, 0))],
        out_specs=pl.BlockSpec((tl, gd), lambda i, g: (i, g)),
        out_shape=jax.ShapeDtypeStruct((n_tok, n_groups * gd), BF16),
        compiler_params=pltpu.CompilerParams(
            dimension_semantics=("arbitrary", "arbitrary"),
            vmem_limit_bytes=_vmem_limit(blocks, 0)),
        name="sgu",
    )(p, p, ln_g, ln_b, w_s, b_s)


def _split_bf16(t):
    hi = t.astype(BF16)
    lo = (t - hi.astype(F32)).astype(BF16)
    return hi, lo


def _router_kernel(x_ref, g_ref, sh_ref, sc_ref, wr_ref, br_ref, h_ref, eid_ref, ew_ref):
    h = _rms_mod(x_ref[...], g_ref[...], sh_ref[0:1, :], sc_ref[0:1, :])
    half = h.shape[1] // 2
    bits = lax.bitcast_convert_type(h.astype(BF16).astype(F32), jnp.uint32)
    h_ref[...] = (bits[:, half:] & jnp.uint32(0xFFFF0000)) | (bits[:, :half] >> 16)
    h_hi, h_lo = _split_bf16(h)
    w_hi, w_lo = _split_bf16(wr_ref[...])
    logits = (jnp.dot(h_hi, w_hi, preferred_element_type=F32)
              + jnp.dot(h_lo, w_hi, preferred_element_type=F32)
              + jnp.dot(h_hi, w_lo, preferred_element_type=F32)) + br_ref[...]
    lane = lax.broadcasted_iota(jnp.int32, logits.shape, 1)
    lane_f = lane.astype(F32)
    neg = -jnp.inf

    def first_lane(hit):
        return jnp.min(jnp.where(hit, lane_f, float(V7X_LANES)), axis=-1, keepdims=True).astype(jnp.int32)

    gl = jnp.where(lane < MOE_GROUPS, logits, neg)
    g_max = jnp.max(gl, axis=-1, keepdims=True)
    g_sel = first_lane(gl == g_max)
    p_g = 1.0 / jnp.sum(jnp.exp(gl - g_max), axis=-1, keepdims=True)
    e_lo = MOE_GROUPS + g_sel * EXPERTS_PER_GROUP
    el = jnp.where((lane >= e_lo) & (lane < e_lo + EXPERTS_PER_GROUP), logits, neg)
    v1 = jnp.max(el, axis=-1, keepdims=True)
    i1 = first_lane(el == v1)
    el2 = jnp.where(lane == i1, neg, el)
    v2 = jnp.max(el2, axis=-1, keepdims=True)
    i2 = first_lane(el2 == v2)
    e2 = jnp.exp(v2 - v1)
    den = 1.0 + e2
    w1 = p_g * (1.0 / den)
    w2 = p_g * (e2 / den)
    eid_ref[...] = jnp.where(lane == 0, i1 - MOE_GROUPS, jnp.where(lane == 1, i2 - MOE_GROUPS, 0))
    ew_ref[...] = jnp.where(lane == 0, w1, jnp.where(lane == 1, w2, 0.0))


def _router(x, g, mod, shift_chunk, scale_chunk, wr, br):
    m, d = x.shape
    tm = ROW_TILE
    blocks = 2 * _nbytes((tm, d), F32) + 2 * _nbytes((MOD_ROWS, d), F32) + _nbytes((d, V7X_LANES), F32) \
        + 2 * _nbytes((tm, V7X_LANES), F32)
    return pl.pallas_call(
        _router_kernel,
        grid=(m // tm,),
        in_specs=[pl.BlockSpec((tm, d), lambda i: (i, 0)),
                  pl.BlockSpec((1, d), lambda i: (0, 0)),
                  pl.BlockSpec((MOD_ROWS, d), lambda i: (0, shift_chunk)),
                  pl.BlockSpec((MOD_ROWS, d), lambda i: (0, scale_chunk)),
                  pl.BlockSpec((d, V7X_LANES), lambda i: (0, 0)),
                  pl.BlockSpec((1, V7X_LANES), lambda i: (0, 0))],
        out_specs=[pl.BlockSpec((tm, d // 2), lambda i: (i, 0)),
                   pl.BlockSpec((tm, V7X_LANES), lambda i: (i, 0)),
                   pl.BlockSpec((tm, V7X_LANES), lambda i: (i, 0))],
        out_shape=[jax.ShapeDtypeStruct((m, d // 2), jnp.uint32),
                   jax.ShapeDtypeStruct((m, V7X_LANES), jnp.int32),
                   jax.ShapeDtypeStruct((m, V7X_LANES), F32)],
        compiler_params=pltpu.CompilerParams(
            dimension_semantics=("arbitrary",),
            vmem_limit_bytes=_vmem_limit(blocks, 0)),
        name="router",
    )(x, g, mod, mod, wr, br)


def _row_copy(src, src_row, dst, dst_row, sem):
    return pltpu.make_async_copy(src.at[pl.ds(src_row, 1), :], dst.at[pl.ds(dst_row, 1), :], sem)


def _for_rows(n_rows, fn):
    n_groups = lax.shift_right_logical(n_rows, DMA_UNROLL.bit_length() - 1)

    def group(gi, carry):
        for u in range(DMA_UNROLL):
            fn(gi * DMA_UNROLL + u)
        return carry

    def single(r, carry):
        fn(r)
        return carry

    lax.fori_loop(0, n_groups, group, 0)
    lax.fori_loop(n_groups * DMA_UNROLL, n_rows, single, 0)


def _experts_kernel(be_ref, nact_ref, nv_ref, nr_ref, cs_ref, stok_ref, sdst_ref,
                    h_hbm, wg_ref, wu_ref, wd_ref, y_hbm,
                    x_ref, acc_ref, wgb_ref, wub_ref, wdb_ref, sem_in, sem_out):
    del be_ref
    b = pl.program_id(0)
    f = pl.program_id(1)
    n_active = nact_ref[0]
    active = b < n_active
    slot = lax.rem(b, 2)
    sub = EXPERT_SUB
    last_f = EXPERT_SPLIT - 1

    def gather(bb, slot_, wait):
        def one(r):
            src = stok_ref[cs_ref[bb] + jnp.minimum(r, nv_ref[bb] - 1)]
            cp = _row_copy(h_hbm, src, x_ref.at[slot_], r, sem_in.at[slot_])
            cp.wait() if wait else cp.start()

        _for_rows(nr_ref[bb], one)

    def scatter(bb, wait):
        def one(r):
            cp = _row_copy(acc_ref, r, y_hbm, sdst_ref[cs_ref[bb] + r], sem_out)
            cp.wait() if wait else cp.start()

        _for_rows(nv_ref[bb], one)

    @pl.when(active & (f == 0))
    def _rows_in():
        @pl.when(b == 0)
        def _():
            gather(0, 0, False)

        @pl.when(b + 1 < n_active)
        def _():
            gather(b + 1, 1 - slot, False)

        gather(b, slot, True)

    @pl.when(active)
    def _compute():
        wgb_ref[...] = wg_ref[0].astype(BF16)
        wub_ref[...] = wu_ref[0].astype(BF16)
        wdb_ref[...] = wd_ref[0].astype(BF16)
        half = wgb_ref.shape[0] // 2
        for j in range(x_ref.shape[1] // sub):
            @pl.when(j * sub < nv_ref[b])
            def _sub_block():
                rows = pl.ds(j * sub, sub)
                xw = x_ref[slot, rows, :]
                xa = lax.bitcast_convert_type(xw << 16, F32).astype(BF16)
                xb = lax.bitcast_convert_type(xw & jnp.uint32(0xFFFF0000), F32).astype(BF16)
                gate = (jnp.dot(xa, wgb_ref[:half, :], preferred_element_type=F32)
                        + jnp.dot(xb, wgb_ref[half:, :], preferred_element_type=F32))
                up = (jnp.dot(xa, wub_ref[:half, :], preferred_element_type=F32)
                      + jnp.dot(xb, wub_ref[half:, :], preferred_element_type=F32))
                hid = (gate * jax.nn.sigmoid(gate) * up).astype(BF16)

                if j == 0:
                    @pl.when((f == 0) & (b > 0))
                    def _():
                        scatter(b - 1, True)

                @pl.when(f == 0)
                def _():
                    acc_ref[rows, :] = jnp.dot(hid, wdb_ref[...], preferred_element_type=F32)

                @pl.when(f > 0)
                def _():
                    acc_ref[rows, :] += jnp.dot(hid, wdb_ref[...], preferred_element_type=F32)

    @pl.when(active & (f == last_f))
    def _rows_out():
        scatter(b, False)

        @pl.when(b == n_active - 1)
        def _():
            scatter(b, True)


def _experts(h, w_gate, w_up, w_down, plan, n_out_rows):
    block_e, n_active, n_valid, n_rows, c_start, s_tok, s_dst = plan
    d = w_gate.shape[1]
    de = w_gate.shape[-1]
    des = de // EXPERT_SPLIT
    n_blocks = block_e.shape[0]
    assert h.shape[1] * 2 == d and EXPERT_SPLIT >= 2 and EXPERT_ROWS % EXPERT_SUB == 0

    def w_idx(transpose):
        def index_map(b, f, be, nact, *_):
            fs = jnp.where(b < nact[0], f, EXPERT_SPLIT - 1)
            return (be[b], fs, 0) if transpose else (be[b], 0, fs)
        return index_map

    blocks = 3 * _nbytes((d, des), F32)
    scratch = (_nbytes((2, EXPERT_ROWS, d // 2), jnp.uint32) + _nbytes((EXPERT_ROWS, d), F32)
               + 3 * _nbytes((d, des), BF16))
    grid_spec = pltpu.PrefetchScalarGridSpec(
        num_scalar_prefetch=7,
        grid=(n_blocks, EXPERT_SPLIT),
        in_specs=[pl.BlockSpec(memory_space=pl.ANY),
                  pl.BlockSpec((1, d, des), w_idx(False)),
                  pl.BlockSpec((1, d, des), w_idx(False)),
                  pl.BlockSpec((1, des, d), w_idx(True))],
        out_specs=pl.BlockSpec(memory_space=pl.ANY),
        scratch_shapes=[pltpu.VMEM((2, EXPERT_ROWS, d // 2), jnp.uint32),
                        pltpu.VMEM((EXPERT_ROWS, d), F32),
                        pltpu.VMEM((d, des), BF16), pltpu.VMEM((d, des), BF16), pltpu.VMEM((des, d), BF16),
                        pltpu.SemaphoreType.DMA((2,)), pltpu.SemaphoreType.DMA(())],
    )
    return pl.pallas_call(
        _experts_kernel,
        grid_spec=grid_spec,
        out_shape=jax.ShapeDtypeStruct((n_out_rows, d), F32),
        compiler_params=pltpu.CompilerParams(
            dimension_semantics=("arbitrary", "arbitrary"),
            vmem_limit_bytes=_vmem_limit(blocks, scratch)),
        name="experts",
    )(block_e, n_active, n_valid, n_rows, c_start, s_tok, s_dst, h, w_gate, w_up, w_down)


def _combine_kernel(*refs, top_k):
    y_refs = refs[:top_k]
    w_ref, x_ref, g_ref, fg_ref, o_ref = refs[top_k:]
    y = y_refs[0][0] * w_ref[:, 0:1]
    for k in range(1, top_k):
        y = y + y_refs[k][0] * w_ref[:, k:k + 1]
    x = x_ref[...] + g_ref[0:1, :] * y
    o_ref[...] = x * lax.rsqrt(jnp.mean(x * x, axis=-1, keepdims=True) + EPS) * fg_ref[...]


def _combine(y, ew, x, mod, gate_chunk, final_g, top_k):
    m, d = x.shape
    tm = ROW_TILE
    blocks = (top_k + 2) * _nbytes((tm, d), F32) + _nbytes((MOD_ROWS, d), F32) + _nbytes((tm, V7X_LANES), F32)
    in_specs = [pl.BlockSpec((1, tm, d), functools.partial(lambda k, i: (k, i, 0), k)) for k in range(top_k)]
    in_specs += [pl.BlockSpec((tm, V7X_LANES), lambda i: (i, 0)),
                 pl.BlockSpec((tm, d), lambda i: (i, 0)),
                 pl.BlockSpec((MOD_ROWS, d), lambda i: (0, gate_chunk)),
                 pl.BlockSpec((1, d), lambda i: (0, 0))]
    return pl.pallas_call(
        functools.partial(_combine_kernel, top_k=top_k),
        grid=(m // tm,),
        in_specs=in_specs,
        out_specs=pl.BlockSpec((tm, d), lambda i: (i, 0)),
        out_shape=jax.ShapeDtypeStruct((m, d), F32),
        compiler_params=pltpu.CompilerParams(
            dimension_semantics=("arbitrary",),
            vmem_limit_bytes=_vmem_limit(blocks, 0)),
        name="combine",
    )(*([y] * top_k), ew, x, mod, final_g)


def _rope_tables(n_tokens):
    n_rows = n_tokens // GRID_W
    rows = jnp.repeat(jnp.arange(n_rows), GRID_W)
    cols = jnp.tile(jnp.arange(GRID_W), n_rows)
    n_freq = RET_DK // 4
    freqs = ROPE_BASE ** (-jnp.arange(n_freq, dtype=F32) / n_freq)
    pos = jnp.stack([rows, cols], axis=-1).astype(F32)
    ang = pos[:, :, None, None] * freqs
    ang = jnp.broadcast_to(ang, (n_tokens, 2, 2, n_freq)).reshape(n_tokens, RET_DK)
    sign = jnp.tile(jnp.concatenate([-jnp.ones((n_freq,), F32), jnp.ones((n_freq,), F32)]), 2)
    return jnp.cos(ang), jnp.sin(ang) * sign


def _dispatch(eid, n_tok, top_k):
    m = n_tok * top_k
    n_blocks = -(-m // EXPERT_ROWS) + N_EXPERTS
    e_flat = eid[:, :top_k].reshape(-1)
    order = jnp.argsort(e_flat).astype(jnp.int32)
    s_tok = order // top_k
    s_dst = (order % top_k) * n_tok + s_tok
    counts = jnp.sum((e_flat[:, None] == jnp.arange(N_EXPERTS, dtype=jnp.int32)[None, :]).astype(jnp.int32), axis=0)
    starts = jnp.cumsum(counts) - counts
    e_blocks = (counts + EXPERT_ROWS - 1) // EXPERT_ROWS
    b_ends = jnp.cumsum(e_blocks)
    n_active = b_ends[-1].astype(jnp.int32)
    blk = jnp.arange(n_blocks, dtype=jnp.int32)
    block_e = jnp.searchsorted(b_ends, jnp.minimum(blk, n_active - 1), side='right')
    block_e = jnp.minimum(block_e, N_EXPERTS - 1).astype(jnp.int32)
    within = blk - (b_ends - e_blocks)[block_e]
    c_start = (starts[block_e] + within * EXPERT_ROWS).astype(jnp.int32)
    n_valid = jnp.where(blk < n_active, jnp.clip(counts[block_e] - within * EXPERT_ROWS, 0, EXPERT_ROWS), 0)
    n_valid = n_valid.astype(jnp.int32)
    n_rows = (n_valid + EXPERT_SUB - 1) // EXPERT_SUB * EXPERT_SUB
    return block_e, n_active.reshape(1), n_valid, n_rows, c_start, s_tok, s_dst


def kernel(x, c, ctx, c_ctx, w_ada, b_ada, norm1_g, norm2_g, w_in, ret_decay_f, ret_decay_b, ret_gn_g, sgu_ln_g, sgu_ln_b, sgu_w_s, sgu_b_s, w_out, w_router_group, b_router_group, w_router_expert, b_router_expert, w_gate, w_up, w_down, final_g):
    batch, n_tok, d = x.shape
    assert batch == 1 and w_ada.shape[0] == 1
    n_heads = ret_decay_f.shape[-1]
    n_groups = sgu_w_s.shape[1]
    ret_qk_w = n_heads * RET_DK
    ret_w = n_heads * RET_DV
    sgu_w = n_groups * SGU_GROUP_DIM
    k_off = ret_qk_w
    u_off = 2 * ret_qk_w + 2 * ret_w
    in_w = u_off + 2 * sgu_w
    top_k = 2
    assert w_in.shape == (1, d, in_w) and w_out.shape == (1, ret_w + sgu_w, d)

    cc = jnp.zeros((MOD_ROWS, d), F32).at[0].set(c[0]).at[1].set(c_ctx)
    mod = _ada(cc, w_ada[0], b_ada[0].reshape(1, N_MOD * d))

    lg_f = -jnp.exp(ret_decay_f[0])
    lg_b = -jnp.exp(ret_decay_b[0])

    hc = _norm_mod(ctx[0], norm1_g, mod, 1, 0, 1, ROW_TILE)
    kv_c = _matmul([hc], w_in[0], ret_qk_w + ret_w, k_off, ctx.shape[1], MM_TN)
    s_f, s_b = _ctx_state(kv_c, lg_f, lg_b, n_heads)

    h1 = _norm_mod(x[0], norm1_g, mod, 0, 0, 1, ROW_TILE)
    p = _matmul([h1], w_in[0], in_w, 0, MM_TM, MM_TN)
    cos, sin_signed = _rope_tables(n_tok)
    ret_out = _retention(p, cos, sin_signed, lg_f, lg_b, s_f, s_b, ret_gn_g, n_heads)
    sgu_out = _sgu(p, sgu_ln_g, sgu_ln_b, sgu_w_s[0], sgu_b_s[0].reshape(n_groups, SGU_CHUNK, 1), u_off)
    x1 = _matmul([ret_out, sgu_out], w_out[0], d, 0, MM_TM, MM_TN, residual=(x[0], mod, 2))

    n_router = MOE_GROUPS + N_EXPERTS
    wr = jnp.zeros((d, V7X_LANES), F32).at[:, :MOE_GROUPS].set(w_router_group[0]) \
        .at[:, MOE_GROUPS:n_router].set(w_router_expert[0])
    br = jnp.zeros((1, V7X_LANES), F32).at[0, :MOE_GROUPS].set(b_router_group[0]) \
        .at[0, MOE_GROUPS:n_router].set(b_router_expert[0])
    h2, eid, ew = _router(x1, norm2_g, mod, 3, 4, wr, br)
    plan = _dispatch(eid, n_tok, top_k)
    y = _experts(h2, w_gate[0], w_up[0], w_down[0], plan, top_k * n_tok)
    out = _combine(y.reshape(top_k, n_tok, d), ew, x1, mod, 5, final_g.reshape(1, d), top_k)
    return out.reshape(batch, n_tok, d)
```

```python
import functools

import jax
import jax.numpy as jnp
from jax import lax
from jax.experimental import pallas as pl
from jax.experimental.pallas import tpu as pltpu

F32 = jnp.float32
BF16 = jnp.bfloat16

GRID_W = 64
RET_DK = 256
RET_DV = 256
RET_CHUNK = 128
ROPE_BASE = 10000.0
SGU_GROUP_DIM = 256
SGU_CHUNK = 128
MOE_GROUPS = 8
EXPERTS_PER_GROUP = 8
N_EXPERTS = MOE_GROUPS * EXPERTS_PER_GROUP
N_MOD = 6
EPS = 1e-6

V7X_LANES = 128
V7X_VMEM_BYTES = 64 * 1024 * 1024
MOD_ROWS = 8

ADA_TN = 512
MM_TM = 1024
MM_TN = 512
IN_TM = 512
IN_TN = 1024
ROW_TILE = 256
SEQ_TILE = 512
EXPERT_ROWS = 512
EXPERT_SUB = 256
EXPERT_SPLIT = 2
DMA_UNROLL = 8


def _vmem_limit(block_bytes, scratch_bytes):
    want = 2 * block_bytes + scratch_bytes
    return int(min(V7X_VMEM_BYTES - 4 * 1024 * 1024, max(2 * want, 32 * 1024 * 1024)))


def _nbytes(shape, dtype):
    n = 1
    for s in shape:
        n *= s
    return n * jnp.dtype(dtype).itemsize


def _ada_kernel(cc_ref, w_ref, b_ref, o_ref):
    a = cc_ref[...]
    s = (a * jax.nn.sigmoid(a)).astype(BF16)
    o_ref[...] = jnp.dot(s, w_ref[...].astype(BF16), preferred_element_type=F32) + b_ref[...]


def _ada(cc, w, b):
    d, n = w.shape
    blocks = _nbytes((MOD_ROWS, d), F32) + _nbytes((d, ADA_TN), F32) + 2 * _nbytes((MOD_ROWS, ADA_TN), F32)
    return pl.pallas_call(
        _ada_kernel,
        grid=(n // ADA_TN,),
        in_specs=[pl.BlockSpec((MOD_ROWS, d), lambda j: (0, 0)),
                  pl.BlockSpec((d, ADA_TN), lambda j: (0, j)),
                  pl.BlockSpec((1, ADA_TN), lambda j: (0, j))],
        out_specs=pl.BlockSpec((MOD_ROWS, ADA_TN), lambda j: (0, j)),
        out_shape=jax.ShapeDtypeStruct((MOD_ROWS, n), F32),
        compiler_params=pltpu.CompilerParams(
            dimension_semantics=("arbitrary",),
            vmem_limit_bytes=_vmem_limit(blocks, _nbytes((d, ADA_TN), BF16))),
        name="ada",
    )(cc, w, b)


def _rms_mod(x, g, shift, scale):
    y = x * lax.rsqrt(jnp.mean(x * x, axis=-1, keepdims=True) + EPS) * g
    return y * (1.0 + scale) + shift


def _norm_mod_kernel(x_ref, g_ref, sh_ref, sc_ref, o_ref, *, row):
    h = _rms_mod(x_ref[...], g_ref[...], sh_ref[row:row + 1, :], sc_ref[row:row + 1, :])
    o_ref[...] = h.astype(o_ref.dtype)


def _norm_mod(x, g, mod, row, shift_chunk, scale_chunk, tm):
    m, d = x.shape
    blocks = _nbytes((tm, d), F32) * 2 + 3 * _nbytes((MOD_ROWS, d), F32)
    return pl.pallas_call(
        functools.partial(_norm_mod_kernel, row=row),
        grid=(m // tm,),
        in_specs=[pl.BlockSpec((tm, d), lambda i: (i, 0)),
                  pl.BlockSpec((1, d), lambda i: (0, 0)),
                  pl.BlockSpec((MOD_ROWS, d), lambda i: (0, shift_chunk)),
                  pl.BlockSpec((MOD_ROWS, d), lambda i: (0, scale_chunk))],
        out_specs=pl.BlockSpec((tm, d), lambda i: (i, 0)),
        out_shape=jax.ShapeDtypeStruct((m, d), BF16),
        compiler_params=pltpu.CompilerParams(
            dimension_semantics=("arbitrary",),
            vmem_limit_bytes=_vmem_limit(blocks, 0)),
        name="norm_mod",
    )(x, g, mod, mod)


def _matmul_acc(a_refs, w_ref, wbf_ref):
    @pl.when(pl.program_id(1) == 0)
    def _():
        wbf_ref[...] = w_ref[...].astype(BF16)

    acc = None
    k0 = 0
    for a_ref in a_refs:
        kk = a_ref.shape[1]
        part = jnp.dot(a_ref[...], wbf_ref[k0:k0 + kk, :], preferred_element_type=F32)
        acc = part if acc is None else acc + part
        k0 += kk
    return acc


def _matmul_kernel(*refs, n_a):
    w_ref, o_ref, wbf_ref = refs[n_a:]
    o_ref[...] = _matmul_acc(refs[:n_a], w_ref, wbf_ref).astype(o_ref.dtype)


def _matmul_res_kernel(*refs, n_a):
    w_ref, x_ref, g_ref, o_ref, wbf_ref = refs[n_a:]
    o_ref[...] = x_ref[...] + g_ref[0:1, :] * _matmul_acc(refs[:n_a], w_ref, wbf_ref)


def _matmul(a_list, w, n_cols, col_off, tm, tn, residual=None, out_dtype=F32):
    m = a_list[0].shape[0]
    k = sum(a.shape[1] for a in a_list)
    n_a = len(a_list)
    joff = col_off // tn
    grid = (n_cols // tn, m // tm)
    in_specs = [pl.BlockSpec((tm, a.shape[1]), lambda j, i: (i, 0)) for a in a_list]
    in_specs.append(pl.BlockSpec((k, tn), lambda j, i: (0, j + joff)))
    args = list(a_list) + [w]
    blocks = _nbytes((tm, k), BF16) + _nbytes((k, tn), F32) + _nbytes((tm, tn), F32)
    if residual is None:
        body = functools.partial(_matmul_kernel, n_a=n_a)
    else:
        x, mod, gate_chunk = residual
        goff = gate_chunk * (n_cols // tn)
        in_specs += [pl.BlockSpec((tm, tn), lambda j, i: (i, j)),
                     pl.BlockSpec((MOD_ROWS, tn), lambda j, i: (0, goff + j))]
        args += [x, mod]
        blocks += _nbytes((tm, tn), F32) + _nbytes((MOD_ROWS, tn), F32)
        body = functools.partial(_matmul_res_kernel, n_a=n_a)
    return pl.pallas_call(
        body,
        grid=grid,
        in_specs=in_specs,
        out_specs=pl.BlockSpec((tm, tn), lambda j, i: (i, j)),
        out_shape=jax.ShapeDtypeStruct((m, n_cols), out_dtype),
        scratch_shapes=[pltpu.VMEM((k, tn), BF16)],
        compiler_params=pltpu.CompilerParams(
            dimension_semantics=("arbitrary", "arbitrary"),
            vmem_limit_bytes=_vmem_limit(blocks, _nbytes((k, tn), BF16))),
        name="matmul_res" if residual is not None else "matmul",
    )(*args)


def _ctx_state_kernel(lgf_ref, lgb_ref, k_ref, v_ref, sf_ref, sb_ref):
    h = pl.program_id(0)
    n = k_ref.shape[0]
    pos = lax.broadcasted_iota(jnp.int32, (n, 1), 0).astype(F32)
    k = k_ref[...] * (RET_DK ** -0.5)
    v = v_ref[...].astype(BF16)
    wf = jnp.exp((n - 1.0 - pos) * lgf_ref[h])
    wb = jnp.exp(pos * lgb_ref[h])
    tn_dims = (((0,), (0,)), ((), ()))
    sf_ref[0] = lax.dot_general((k * wf).astype(BF16), v, tn_dims, preferred_element_type=F32)
    sb_ref[0] = lax.dot_general((k * wb).astype(BF16), v, tn_dims, preferred_element_type=F32)


def _ctx_state(kv, lg_f, lg_b, n_heads):
    n = kv.shape[0]
    smem = pl.BlockSpec(memory_space=pltpu.SMEM)
    st = jax.ShapeDtypeStruct((n_heads, RET_DK, RET_DV), F32)
    return pl.pallas_call(
        _ctx_state_kernel,
        grid=(n_heads,),
        in_specs=[smem, smem,
                  pl.BlockSpec((n, RET_DK), lambda h: (0, h)),
                  pl.BlockSpec((n, RET_DV), lambda h: (0, n_heads + h))],
        out_specs=[pl.BlockSpec((1, RET_DK, RET_DV), lambda h: (h, 0, 0)),
                   pl.BlockSpec((1, RET_DK, RET_DV), lambda h: (h, 0, 0))],
        out_shape=[st, st],
        compiler_params=pltpu.CompilerParams(dimension_semantics=("arbitrary",)),
        name="ctx_state",
    )(lg_f, lg_b, kv, kv)


def _rope(t, cos, sin_signed):
    half = RET_DK // 2
    rot = jnp.concatenate([pltpu.roll(t[:, :half], half // 2, 1),
                           pltpu.roll(t[:, half:], half // 2, 1)], axis=1)
    return t * cos + rot * sin_signed


def _retention_kernel(lgf_ref, lgb_ref, q_ref, k_ref, v_ref, g_ref, cos_ref, sin_ref,
                      s0f_ref, s0b_ref, gn_ref, o_ref, state_ref, yb_ref, *, n_steps):
    h = pl.program_id(0)
    p = pl.program_id(1)
    s = pl.program_id(2)
    c = RET_CHUNK
    n_sub = q_ref.shape[0] // c
    ii = lax.broadcasted_iota(jnp.int32, (c, c), 0)
    jj = lax.broadcasted_iota(jnp.int32, (c, c), 1)
    idx = lax.broadcasted_iota(jnp.int32, (c, 1), 0).astype(F32)
    nt_dims = (((1,), (1,)), ((), ()))
    tn_dims = (((0,), (0,)), ((), ()))

    def chunk(ci, decay, q_decay, k_decay, chunk_decay):
        rows = pl.ds(ci * c, c)
        cos = cos_ref[rows, :]
        sin = sin_ref[rows, :]
        q = _rope(q_ref[rows, :].astype(F32), cos, sin)
        k = _rope(k_ref[rows, :].astype(F32) * (RET_DK ** -0.5), cos, sin)
        v = v_ref[rows, :].astype(BF16)
        qb = q.astype(BF16)
        st = state_ref[...]
        scores = lax.dot_general(qb, k.astype(BF16), nt_dims, preferred_element_type=F32) * decay
        out = (jnp.dot(scores.astype(BF16), v, preferred_element_type=F32)
               + jnp.dot(qb, st.astype(BF16), preferred_element_type=F32) * q_decay)
        state_ref[...] = st * chunk_decay + lax.dot_general(
            (k * k_decay).astype(BF16), v, tn_dims, preferred_element_type=F32)
        return out

    @pl.when(p == 0)
    def _backward():
        lg = lgb_ref[h]
        blk = n_steps - 1 - s

        @pl.when(s == 0)
        def _():
            state_ref[...] = s0b_ref[0]

        mask = jj > ii
        decay = jnp.where(mask, jnp.exp(jnp.where(mask, jj - ii, 0).astype(F32) * lg), 0.0)
        q_decay = jnp.exp((c - idx) * lg)
        k_decay = jnp.exp(idx * lg)
        chunk_decay = jnp.exp(jnp.full((1, RET_DV), c, F32) * lg)
        for ci in reversed(range(n_sub)):
            out = chunk(ci, decay, q_decay, k_decay, chunk_decay)
            start = pl.multiple_of(blk * (n_sub * c) + ci * c, c)
            yb_ref[pl.ds(start, c), :] = out

    @pl.when(p == 1)
    def _forward():
        lg = lgf_ref[h]

        @pl.when(s == 0)
        def _():
            state_ref[...] = s0f_ref[0]

        mask = ii >= jj
        decay = jnp.where(mask, jnp.exp(jnp.where(mask, ii - jj, 0).astype(F32) * lg), 0.0)
        q_decay = jnp.exp((idx + 1.0) * lg)
        k_decay = jnp.exp((c - 1.0 - idx) * lg)
        chunk_decay = jnp.exp(jnp.full((1, RET_DV), c, F32) * lg)
        for ci in range(n_sub):
            out = chunk(ci, decay, q_decay, k_decay, chunk_decay)
            start = pl.multiple_of(s * (n_sub * c) + ci * c, c)
            y = out + yb_ref[pl.ds(start, c), :]
            mu = jnp.mean(y, axis=-1, keepdims=True)
            var = jnp.mean(jnp.square(y - mu), axis=-1, keepdims=True)
            yn = (y - mu) * lax.rsqrt(var + EPS) * gn_ref[...]
            gate = g_ref[pl.ds(ci * c, c), :].astype(F32)
            o_ref[pl.ds(ci * c, c), :] = (gate * jax.nn.sigmoid(gate) * yn).astype(o_ref.dtype)


def _retention(p, cos, sin_signed, lg_f, lg_b, s0_f, s0_b, gn_g, n_heads):
    n_tok = p.shape[0]
    tl = SEQ_TILE
    n_steps = n_tok // tl
    smem = pl.BlockSpec(memory_space=pltpu.SMEM)

    def seq_blk(p_, s_):
        return jnp.where(p_ == 0, n_steps - 1 - s_, s_)

    blocks = 4 * _nbytes((tl, RET_DK), F32) + 2 * _nbytes((tl, RET_DK), F32) \
        + 2 * _nbytes((RET_DK, RET_DV), F32) + _nbytes((tl, RET_DV), BF16)
    scratch = _nbytes((RET_DK, RET_DV), F32) + _nbytes((n_tok, RET_DV), F32)
    return pl.pallas_call(
        functools.partial(_retention_kernel, n_steps=n_steps),
        grid=(n_heads, 2, n_steps),
        in_specs=[smem, smem,
                  pl.BlockSpec((tl, RET_DK), lambda h, p_, s_: (seq_blk(p_, s_), h)),
                  pl.BlockSpec((tl, RET_DK), lambda h, p_, s_: (seq_blk(p_, s_), n_heads + h)),
                  pl.BlockSpec((tl, RET_DV), lambda h, p_, s_: (seq_blk(p_, s_), 2 * n_heads + h)),
                  pl.BlockSpec((tl, RET_DV), lambda h, p_, s_: (s_ * p_, 3 * n_heads + h)),
                  pl.BlockSpec((tl, RET_DK), lambda h, p_, s_: (seq_blk(p_, s_), 0)),
                  pl.BlockSpec((tl, RET_DK), lambda h, p_, s_: (seq_blk(p_, s_), 0)),
                  pl.BlockSpec((1, RET_DK, RET_DV), lambda h, p_, s_: (h, 0, 0)),
                  pl.BlockSpec((1, RET_DK, RET_DV), lambda h, p_, s_: (h, 0, 0)),
                  pl.BlockSpec((1, RET_DV), lambda h, p_, s_: (0, h))],
        out_specs=pl.BlockSpec((tl, RET_DV), lambda h, p_, s_: (s_ * p_, h)),
        out_shape=jax.ShapeDtypeStruct((n_tok, n_heads * RET_DV), BF16),
        scratch_shapes=[pltpu.VMEM((RET_DK, RET_DV), F32), pltpu.VMEM((n_tok, RET_DV), F32)],
        compiler_params=pltpu.CompilerParams(
            dimension_semantics=("arbitrary", "arbitrary", "arbitrary"),
            vmem_limit_bytes=_vmem_limit(blocks, scratch)),
        name="retention",
    )(lg_f, lg_b, p, p, p, p, cos, sin_signed, s0_f, s0_b, gn_g)


def _sgu_kernel(u_ref, sv_ref, lng_ref, lnb_ref, ws_ref, bs_ref, o_ref):
    c = SGU_CHUNK
    ws = ws_ref[0].astype(BF16)
    bs = bs_ref[0]
    for ci in range(u_ref.shape[0] // c):
        rows = pl.ds(ci * c, c)
        v32 = jax.nn.gelu(sv_ref[rows, :].astype(F32))
        mu = jnp.mean(v32, axis=-1, keepdims=True)
        var = jnp.mean(jnp.square(v32 - mu), axis=-1, keepdims=True)
        vn = (v32 - mu) * lax.rsqrt(var + EPS) * lng_ref[...] + lnb_ref[...]
        mixed = jnp.dot(ws, vn.astype(BF16), preferred_element_type=F32) + bs
        o_ref[rows, :] = (jax.nn.gelu(u_ref[rows, :].astype(F32)) * mixed).astype(o_ref.dtype)


def _sgu(p, ln_g, ln_b, w_s, b_s, u_off):
    n_tok = p.shape[0]
    n_groups = w_s.shape[0]
    tl = SEQ_TILE
    gd = SGU_GROUP_DIM
    ub = u_off // gd
    blocks = 2 * _nbytes((tl, gd), F32) + _nbytes((tl, gd), BF16) + 2 * _nbytes((SGU_CHUNK, V7X_LANES), F32)
    return pl.pallas_call(
        _sgu_kernel,
        grid=(n_tok // tl, n_groups),
        in_specs=[pl.BlockSpec((tl, gd), lambda i, g: (i, ub + g)),
                  pl.BlockSpec((tl, gd), lambda i, g: (i, ub + n_groups + g)),
                  pl.BlockSpec((1, gd), lambda i, g: (0, g)),
                  pl.BlockSpec((1, gd), lambda i, g: (0, g)),
                  pl.BlockSpec((1, SGU_CHUNK, SGU_CHUNK), lambda i, g: (g, 0, 0)),
                  pl.BlockSpec((1, SGU_CHUNK, 1), lambda i, g: (g, 0, 0))],
        out_specs=pl.BlockSpec((tl, gd), lambda i, g: (i, g)),
        out_shape=jax.ShapeDtypeStruct((n_tok, n_groups * gd), BF16),
        compiler_params=pltpu.CompilerParams(
            dimension_semantics=("arbitrary", "arbitrary"),
            vmem_limit_bytes=_vmem_limit(blocks, 0)),
        name="sgu",
    )(p, p, ln_g, ln_b, w_s, b_s)


def _split_bf16(t):
    hi = t.astype(BF16)
    lo = (t - hi.astype(F32)).astype(BF16)
    return hi, lo


def _router_kernel(x_ref, g_ref, sh_ref, sc_ref, wr_ref, br_ref, h_ref, eid_ref, ew_ref):
    h = _rms_mod(x_ref[...], g_ref[...], sh_ref[0:1, :], sc_ref[0:1, :])
    half = h.shape[1] // 2
    bits = lax.bitcast_convert_type(h.astype(BF16).astype(F32), jnp.uint32)
    h_ref[...] = (bits[:, half:] & jnp.uint32(0xFFFF0000)) | (bits[:, :half] >> 16)
    h_hi, h_lo = _split_bf16(h)
    w_hi, w_lo = _split_bf16(wr_ref[...])
    logits = (jnp.dot(h_hi, w_hi, preferred_element_type=F32)
              + jnp.dot(h_lo, w_hi, preferred_element_type=F32)
              + jnp.dot(h_hi, w_lo, preferred_element_type=F32)) + br_ref[...]
    lane = lax.broadcasted_iota(jnp.int32, logits.shape, 1)
    lane_f = lane.astype(F32)
    neg = -jnp.inf

    def first_lane(hit):
        return jnp.min(jnp.where(hit, lane_f, float(V7X_LANES)), axis=-1, keepdims=True).astype(jnp.int32)

    gl = jnp.where(lane < MOE_GROUPS, logits, neg)
    g_max = jnp.max(gl, axis=-1, keepdims=True)
    g_sel = first_lane(gl == g_max)
    p_g = 1.0 / jnp.sum(jnp.exp(gl - g_max), axis=-1, keepdims=True)
    e_lo = MOE_GROUPS + g_sel * EXPERTS_PER_GROUP
    el = jnp.where((lane >= e_lo) & (lane < e_lo + EXPERTS_PER_GROUP), logits, neg)
    v1 = jnp.max(el, axis=-1, keepdims=True)
    i1 = first_lane(el == v1)
    el2 = jnp.where(lane == i1, neg, el)
    v2 = jnp.max(el2, axis=-1, keepdims=True)
    i2 = first_lane(el2 == v2)
    e2 = jnp.exp(v2 - v1)
    den = 1.0 + e2
    w1 = p_g * (1.0 / den)
    w2 = p_g * (e2 / den)
    eid_ref[...] = jnp.where(lane == 0, i1 - MOE_GROUPS, jnp.where(lane == 1, i2 - MOE_GROUPS, 0))
    ew_ref[...] = jnp.where(lane == 0, w1, jnp.where(lane == 1, w2, 0.0))


def _router(x, g, mod, shift_chunk, scale_chunk, wr, br):
    m, d = x.shape
    tm = ROW_TILE
    blocks = 2 * _nbytes((tm, d), F32) + 2 * _nbytes((MOD_ROWS, d), F32) + _nbytes((d, V7X_LANES), F32) \
        + 2 * _nbytes((tm, V7X_LANES), F32)
    return pl.pallas_call(
        _router_kernel,
        grid=(m // tm,),
        in_specs=[pl.BlockSpec((tm, d), lambda i: (i, 0)),
                  pl.BlockSpec((1, d), lambda i: (0, 0)),
                  pl.BlockSpec((MOD_ROWS, d), lambda i: (0, shift_chunk)),
                  pl.BlockSpec((MOD_ROWS, d), lambda i: (0, scale_chunk)),
                  pl.BlockSpec((d, V7X_LANES), lambda i: (0, 0)),
                  pl.BlockSpec((1, V7X_LANES), lambda i: (0, 0))],
        out_specs=[pl.BlockSpec((tm, d // 2), lambda i: (i, 0)),
                   pl.BlockSpec((tm, V7X_LANES), lambda i: (i, 0)),
                   pl.BlockSpec((tm, V7X_LANES), lambda i: (i, 0))],
        out_shape=[jax.ShapeDtypeStruct((m, d // 2), jnp.uint32),
                   jax.ShapeDtypeStruct((m, V7X_LANES), jnp.int32),
                   jax.ShapeDtypeStruct((m, V7X_LANES), F32)],
        compiler_params=pltpu.CompilerParams(
            dimension_semantics=("arbitrary",),
            vmem_limit_bytes=_vmem_limit(blocks, 0)),
        name="router",
    )(x, g, mod, mod, wr, br)


def _row_copy(src, src_row, dst, dst_row, sem):
    return pltpu.make_async_copy(src.at[pl.ds(src_row, 1), :], dst.at[pl.ds(dst_row, 1), :], sem)


def _for_rows(n_rows, fn):
    n_groups = lax.shift_right_logical(n_rows, DMA_UNROLL.bit_length() - 1)

    def group(gi, carry):
        for u in range(DMA_UNROLL):
            fn(gi * DMA_UNROLL + u)
        return carry

    def single(r, carry):
        fn(r)
        return carry

    lax.fori_loop(0, n_groups, group, 0)
    lax.fori_loop(n_groups * DMA_UNROLL, n_rows, single, 0)


def _experts_kernel(be_ref, nact_ref, nv_ref, nr_ref, cs_ref, stok_ref, sdst_ref,
                    h_hbm, wg_ref, wu_ref, wd_ref, y_hbm,
                    x_ref, acc_ref, wgb_ref, wub_ref, wdb_ref, sem_in, sem_out):
    del be_ref
    b = pl.program_id(0)
    f = pl.program_id(1)
    n_active = nact_ref[0]
    active = b < n_active
    slot = lax.rem(b, 2)
    sub = EXPERT_SUB
    last_f = EXPERT_SPLIT - 1

    def gather(bb, slot_, wait):
        def one(r):
            src = stok_ref[cs_ref[bb] + jnp.minimum(r, nv_ref[bb] - 1)]
            cp = _row_copy(h_hbm, src, x_ref.at[slot_], r, sem_in.at[slot_])
            cp.wait() if wait else cp.start()

        _for_rows(nr_ref[bb], one)

    def scatter(bb, wait):
        def one(r):
            par = lax.rem(bb, 2)
            cp = _row_copy(acc_ref.at[par], r, y_hbm, sdst_ref[cs_ref[bb] + r], sem_out.at[par])
            cp.wait() if wait else cp.start()

        _for_rows(nv_ref[bb], one)

    @pl.when(active & (f == 0))
    def _rows_in():
        @pl.when(b == 0)
        def _():
            gather(0, 0, False)

        @pl.when(b + 1 < n_active)
        def _():
            gather(b + 1, 1 - slot, False)

        gather(b, slot, True)

    @pl.when(active)
    def _compute():
        wgb_ref[...] = wg_ref[0].astype(BF16)
        wub_ref[...] = wu_ref[0].astype(BF16)
        wdb_ref[...] = wd_ref[0].astype(BF16)
        half = wgb_ref.shape[0] // 2
        for j in range(x_ref.shape[1] // sub):
            @pl.when(j * sub < nv_ref[b])
            def _sub_block():
                rows = pl.ds(j * sub, sub)
                xw = x_ref[slot, rows, :]
                xa = lax.bitcast_convert_type(xw << 16, F32).astype(BF16)
                xb = lax.bitcast_convert_type(xw & jnp.uint32(0xFFFF0000), F32).astype(BF16)
                gate = (jnp.dot(xa, wgb_ref[:half, :], preferred_element_type=F32)
                        + jnp.dot(xb, wgb_ref[half:, :], preferred_element_type=F32))
                up = (jnp.dot(xa, wub_ref[:half, :], preferred_element_type=F32)
                      + jnp.dot(xb, wub_ref[half:, :], preferred_element_type=F32))
                hid = (gate * jax.nn.sigmoid(gate) * up).astype(BF16)

                @pl.when(f == 0)
                def _():
                    acc_ref[slot, rows, :] = jnp.dot(hid, wdb_ref[...], preferred_element_type=F32)

                @pl.when(f > 0)
                def _():
                    acc_ref[slot, rows, :] += jnp.dot(hid, wdb_ref[...], preferred_element_type=F32)

    @pl.when(active & (f == last_f))
    def _rows_out():
        @pl.when(b > 0)
        def _():
            scatter(b - 1, True)

        scatter(b, False)

        @pl.when(b == n_active - 1)
        def _():
            scatter(b, True)


def _experts(h, w_gate, w_up, w_down, plan, n_out_rows):
    block_e, n_active, n_valid, n_rows, c_start, s_tok, s_dst = plan
    d = w_gate.shape[1]
    de = w_gate.shape[-1]
    des = de // EXPERT_SPLIT
    n_blocks = block_e.shape[0]
    assert h.shape[1] * 2 == d and EXPERT_SPLIT >= 2 and EXPERT_ROWS % EXPERT_SUB == 0

    def w_idx(transpose):
        def index_map(b, f, be, nact, *_):
            fs = jnp.where(b < nact[0], f, EXPERT_SPLIT - 1)
            return (be[b], fs, 0) if transpose else (be[b], 0, fs)
        return index_map

    blocks = 3 * _nbytes((d, des), F32)
    scratch = (_nbytes((2, EXPERT_ROWS, d // 2), jnp.uint32) + _nbytes((2, EXPERT_ROWS, d), F32)
               + 3 * _nbytes((d, des), BF16))
    grid_spec = pltpu.PrefetchScalarGridSpec(
        num_scalar_prefetch=7,
        grid=(n_blocks, EXPERT_SPLIT),
        in_specs=[pl.BlockSpec(memory_space=pl.ANY),
                  pl.BlockSpec((1, d, des), w_idx(False)),
                  pl.BlockSpec((1, d, des), w_idx(False)),
                  pl.BlockSpec((1, des, d), w_idx(True))],
        out_specs=pl.BlockSpec(memory_space=pl.ANY),
        scratch_shapes=[pltpu.VMEM((2, EXPERT_ROWS, d // 2), jnp.uint32),
                        pltpu.VMEM((2, EXPERT_ROWS, d), F32),
                        pltpu.VMEM((d, des), BF16), pltpu.VMEM((d, des), BF16), pltpu.VMEM((des, d), BF16),
                        pltpu.SemaphoreType.DMA((2,)), pltpu.SemaphoreType.DMA((2,))],
    )
    return pl.pallas_call(
        _experts_kernel,
        grid_spec=grid_spec,
        out_shape=jax.ShapeDtypeStruct((n_out_rows, d), F32),
        compiler_params=pltpu.CompilerParams(
            dimension_semantics=("arbitrary", "arbitrary"),
            vmem_limit_bytes=_vmem_limit(blocks, scratch)),
        name="experts",
    )(block_e, n_active, n_valid, n_rows, c_start, s_tok, s_dst, h, w_gate, w_up, w_down)


def _combine_kernel(*refs, top_k):
    y_refs = refs[:top_k]
    w_ref, x_ref, g_ref, fg_ref, o_ref = refs[top_k:]
    y = y_refs[0][0] * w_ref[:, 0:1]
    for k in range(1, top_k):
        y = y + y_refs[k][0] * w_ref[:, k:k + 1]
    x = x_ref[...] + g_ref[0:1, :] * y
    o_ref[...] = x * lax.rsqrt(jnp.mean(x * x, axis=-1, keepdims=True) + EPS) * fg_ref[...]


def _combine(y, ew, x, mod, gate_chunk, final_g, top_k):
    m, d = x.shape
    tm = ROW_TILE
    blocks = (top_k + 2) * _nbytes((tm, d), F32) + _nbytes((MOD_ROWS, d), F32) + _nbytes((tm, V7X_LANES), F32)
    in_specs = [pl.BlockSpec((1, tm, d), functools.partial(lambda k, i: (k, i, 0), k)) for k in range(top_k)]
    in_specs += [pl.BlockSpec((tm, V7X_LANES), lambda i: (i, 0)),
                 pl.BlockSpec((tm, d), lambda i: (i, 0)),
                 pl.BlockSpec((MOD_ROWS, d), lambda i: (0, gate_chunk)),
                 pl.BlockSpec((1, d), lambda i: (0, 0))]
    return pl.pallas_call(
        functools.partial(_combine_kernel, top_k=top_k),
        grid=(m // tm,),
        in_specs=in_specs,
        out_specs=pl.BlockSpec((tm, d), lambda i: (i, 0)),
        out_shape=jax.ShapeDtypeStruct((m, d), F32),
        compiler_params=pltpu.CompilerParams(
            dimension_semantics=("arbitrary",),
            vmem_limit_bytes=_vmem_limit(blocks, 0)),
        name="combine",
    )(*([y] * top_k), ew, x, mod, final_g)


def _rope_tables(n_tokens):
    n_rows = n_tokens // GRID_W
    rows = jnp.repeat(jnp.arange(n_rows), GRID_W)
    cols = jnp.tile(jnp.arange(GRID_W), n_rows)
    n_freq = RET_DK // 4
    freqs = ROPE_BASE ** (-jnp.arange(n_freq, dtype=F32) / n_freq)
    pos = jnp.stack([rows, cols], axis=-1).astype(F32)
    ang = pos[:, :, None, None] * freqs
    ang = jnp.broadcast_to(ang, (n_tokens, 2, 2, n_freq)).reshape(n_tokens, RET_DK)
    sign = jnp.tile(jnp.concatenate([-jnp.ones((n_freq,), F32), jnp.ones((n_freq,), F32)]), 2)
    return jnp.cos(ang), jnp.sin(ang) * sign


def _dispatch(eid, n_tok, top_k):
    m = n_tok * top_k
    n_blocks = -(-m // EXPERT_ROWS) + N_EXPERTS
    e_flat = eid[:, :top_k].reshape(-1)
    order = jnp.argsort(e_flat).astype(jnp.int32)
    s_tok = order // top_k
    s_dst = (order % top_k) * n_tok + s_tok
    counts = jnp.sum((e_flat[:, None] == jnp.arange(N_EXPERTS, dtype=jnp.int32)[None, :]).astype(jnp.int32), axis=0)
    starts = jnp.cumsum(counts) - counts
    e_blocks = (counts + EXPERT_ROWS - 1) // EXPERT_ROWS
    b_ends = jnp.cumsum(e_blocks)
    n_active = b_ends[-1].astype(jnp.int32)
    blk = jnp.arange(n_blocks, dtype=jnp.int32)
    block_e = jnp.searchsorted(b_ends, jnp.minimum(blk, n_active - 1), side='right')
    block_e = jnp.minimum(block_e, N_EXPERTS - 1).astype(jnp.int32)
    within = blk - (b_ends - e_blocks)[block_e]
    c_start = (starts[block_e] + within * EXPERT_ROWS).astype(jnp.int32)
    n_valid = jnp.where(blk < n_active, jnp.clip(counts[block_e] - within * EXPERT_ROWS, 0, EXPERT_ROWS), 0)
    n_valid = n_valid.astype(jnp.int32)
    n_rows = (n_valid + EXPERT_SUB - 1) // EXPERT_SUB * EXPERT_SUB
    return block_e, n_active.reshape(1), n_valid, n_rows, c_start, s_tok, s_dst


def kernel(x, c, ctx, c_ctx, w_ada, b_ada, norm1_g, norm2_g, w_in, ret_decay_f, ret_decay_b, ret_gn_g, sgu_ln_g, sgu_ln_b, sgu_w_s, sgu_b_s, w_out, w_router_group, b_router_group, w_router_expert, b_router_expert, w_gate, w_up, w_down, final_g):
    batch, n_tok, d = x.shape
    assert batch == 1 and w_ada.shape[0] == 1
    n_heads = ret_decay_f.shape[-1]
    n_groups = sgu_w_s.shape[1]
    ret_qk_w = n_heads * RET_DK
    ret_w = n_heads * RET_DV
    sgu_w = n_groups * SGU_GROUP_DIM
    k_off = ret_qk_w
    u_off = 2 * ret_qk_w + 2 * ret_w
    in_w = u_off + 2 * sgu_w
    top_k = 2
    assert w_in.shape == (1, d, in_w) and w_out.shape == (1, ret_w + sgu_w, d)

    cc = jnp.zeros((MOD_ROWS, d), F32).at[0].set(c[0]).at[1].set(c_ctx)
    mod = _ada(cc, w_ada[0], b_ada[0].reshape(1, N_MOD * d))

    lg_f = -jnp.exp(ret_decay_f[0])
    lg_b = -jnp.exp(ret_decay_b[0])

    hc = _norm_mod(ctx[0], norm1_g, mod, 1, 0, 1, ROW_TILE)
    kv_c = _matmul([hc], w_in[0], ret_qk_w + ret_w, k_off, ctx.shape[1], MM_TN)
    s_f, s_b = _ctx_state(kv_c, lg_f, lg_b, n_heads)

    h1 = _norm_mod(x[0], norm1_g, mod, 0, 0, 1, ROW_TILE)
    p = _matmul([h1], w_in[0], in_w, 0, IN_TM, IN_TN, out_dtype=BF16)
    cos, sin_signed = _rope_tables(n_tok)
    ret_out = _retention(p, cos, sin_signed, lg_f, lg_b, s_f, s_b, ret_gn_g, n_heads)
    sgu_out = _sgu(p, sgu_ln_g, sgu_ln_b, sgu_w_s[0], sgu_b_s[0].reshape(n_groups, SGU_CHUNK, 1), u_off)
    x1 = _matmul([ret_out, sgu_out], w_out[0], d, 0, MM_TM, MM_TN, residual=(x[0], mod, 2))

    n_router = MOE_GROUPS + N_EXPERTS
    wr = jnp.zeros((d, V7X_LANES), F32).at[:, :MOE_GROUPS].set(w_router_group[0]) \
        .at[:, MOE_GROUPS:n_router].set(w_router_expert[0])
    br = jnp.zeros((1, V7X_LANES), F32).at[0, :MOE_GROUPS].set(b_router_group[0]) \
        .at[0, MOE_GROUPS:n_router].set(b_router_expert[0])
    h2, eid, ew = _router(x1, norm2_g, mod, 3, 4, wr, br)
    plan = _dispatch(eid, n_tok, top_k)
    y = _experts(h2, w_gate[0], w_up[0], w_down[0], plan, top_k * n_tok)
    out = _combine(y.reshape(top_k, n_tok, d), ew, x1, mod, 5, final_g.reshape(1, d), top_k)
    return out.reshape(batch, n_tok, d)
```

```python
import functools

import jax
import jax.numpy as jnp
from jax import lax
from jax.experimental import pallas as pl
from jax.experimental.pallas import tpu as pltpu

F32 = jnp.float32
BF16 = jnp.bfloat16

GRID_W = 64
RET_DK = 256
RET_DV = 256
ROPE_BASE = 10000.0
SGU_GROUP_DIM = 256
SGU_CHUNK = 128
MOE_GROUPS = 8
EXPERTS_PER_GROUP = 8
N_EXPERTS = MOE_GROUPS * EXPERTS_PER_GROUP
N_MOD = 6
EPS = 1e-6

V7X_LANES = 128
V7X_VMEM_BYTES = 64 * 1024 * 1024
MOD_ROWS = 8

ADA_TN = 512
MM_TM = 1024
MM_TN = 512
IN_TM = 512
IN_TN = 1024
ROW_TILE = 256
SEQ_TILE = 512
RET_TILE = 1024
RET_SUB = 256
EXPERT_ROWS = 512
EXPERT_SUB = 256
EXPERT_SPLIT = 2
DMA_UNROLL = 8
N_DMA_PRIORITIES = 2


def _vmem_limit(block_bytes, scratch_bytes):
    want = 2 * block_bytes + scratch_bytes
    return int(min(V7X_VMEM_BYTES - 4 * 1024 * 1024, max(2 * want, 32 * 1024 * 1024)))


def _nbytes(shape, dtype):
    n = 1
    for s in shape:
        n *= s
    return n * jnp.dtype(dtype).itemsize


def _ada_kernel(cc_ref, w_ref, b_ref, o_ref):
    a = cc_ref[...]
    s = (a * jax.nn.sigmoid(a)).astype(BF16)
    o_ref[...] = jnp.dot(s, w_ref[...].astype(BF16), preferred_element_type=F32) + b_ref[...]


def _ada(cc, w, b):
    d, n = w.shape
    blocks = _nbytes((MOD_ROWS, d), F32) + _nbytes((d, ADA_TN), F32) + 2 * _nbytes((MOD_ROWS, ADA_TN), F32)
    return pl.pallas_call(
        _ada_kernel,
        grid=(n // ADA_TN,),
        in_specs=[pl.BlockSpec((MOD_ROWS, d), lambda j: (0, 0)),
                  pl.BlockSpec((d, ADA_TN), lambda j: (0, j)),
                  pl.BlockSpec((1, ADA_TN), lambda j: (0, j))],
        out_specs=pl.BlockSpec((MOD_ROWS, ADA_TN), lambda j: (0, j)),
        out_shape=jax.ShapeDtypeStruct((MOD_ROWS, n), F32),
        compiler_params=pltpu.CompilerParams(
            dimension_semantics=("arbitrary",),
            vmem_limit_bytes=_vmem_limit(blocks, _nbytes((d, ADA_TN), BF16))),
        name="ada",
    )(cc, w, b)


def _rms_mod(x, g, shift, scale):
    y = x * lax.rsqrt(jnp.mean(x * x, axis=-1, keepdims=True) + EPS) * g
    return y * (1.0 + scale) + shift


def _norm_mod_kernel(x_ref, g_ref, sh_ref, sc_ref, o_ref, *, row):
    h = _rms_mod(x_ref[...], g_ref[...], sh_ref[row:row + 1, :], sc_ref[row:row + 1, :])
    o_ref[...] = h.astype(o_ref.dtype)


def _norm_mod(x, g, mod, row, shift_chunk, scale_chunk, tm):
    m, d = x.shape
    blocks = _nbytes((tm, d), F32) * 2 + 3 * _nbytes((MOD_ROWS, d), F32)
    return pl.pallas_call(
        functools.partial(_norm_mod_kernel, row=row),
        grid=(m // tm,),
        in_specs=[pl.BlockSpec((tm, d), lambda i: (i, 0)),
                  pl.BlockSpec((1, d), lambda i: (0, 0)),
                  pl.BlockSpec((MOD_ROWS, d), lambda i: (0, shift_chunk)),
                  pl.BlockSpec((MOD_ROWS, d), lambda i: (0, scale_chunk))],
        out_specs=pl.BlockSpec((tm, d), lambda i: (i, 0)),
        out_shape=jax.ShapeDtypeStruct((m, d), BF16),
        compiler_params=pltpu.CompilerParams(
            dimension_semantics=("arbitrary",),
            vmem_limit_bytes=_vmem_limit(blocks, 0)),
        name="norm_mod",
    )(x, g, mod, mod)


def _matmul_acc(a_refs, w_ref, wbf_ref):
    @pl.when(pl.program_id(1) == 0)
    def _():
        wbf_ref[...] = w_ref[...].astype(BF16)

    acc = None
    k0 = 0
    for a_ref in a_refs:
        kk = a_ref.shape[1]
        part = jnp.dot(a_ref[...], wbf_ref[k0:k0 + kk, :], preferred_element_type=F32)
        acc = part if acc is None else acc + part
        k0 += kk
    return acc


def _matmul_kernel(*refs, n_a):
    w_ref, o_ref, wbf_ref = refs[n_a:]
    o_ref[...] = _matmul_acc(refs[:n_a], w_ref, wbf_ref).astype(o_ref.dtype)


def _matmul_res_kernel(*refs, n_a):
    w_ref, x_ref, g_ref, o_ref, wbf_ref = refs[n_a:]
    o_ref[...] = x_ref[...] + g_ref[0:1, :] * _matmul_acc(refs[:n_a], w_ref, wbf_ref)


def _matmul(a_list, w, n_cols, col_off, tm, tn, residual=None, out_dtype=F32):
    m = a_list[0].shape[0]
    k = sum(a.shape[1] for a in a_list)
    n_a = len(a_list)
    joff = col_off // tn
    grid = (n_cols // tn, m // tm)
    in_specs = [pl.BlockSpec((tm, a.shape[1]), lambda j, i: (i, 0)) for a in a_list]
    in_specs.append(pl.BlockSpec((k, tn), lambda j, i: (0, j + joff)))
    args = list(a_list) + [w]
    blocks = _nbytes((tm, k), BF16) + _nbytes((k, tn), F32) + _nbytes((tm, tn), F32)
    if residual is None:
        body = functools.partial(_matmul_kernel, n_a=n_a)
    else:
        x, mod, gate_chunk = residual
        goff = gate_chunk * (n_cols // tn)
        in_specs += [pl.BlockSpec((tm, tn), lambda j, i: (i, j)),
                     pl.BlockSpec((MOD_ROWS, tn), lambda j, i: (0, goff + j))]
        args += [x, mod]
        blocks += _nbytes((tm, tn), F32) + _nbytes((MOD_ROWS, tn), F32)
        body = functools.partial(_matmul_res_kernel, n_a=n_a)
    return pl.pallas_call(
        body,
        grid=grid,
        in_specs=in_specs,
        out_specs=pl.BlockSpec((tm, tn), lambda j, i: (i, j)),
        out_shape=jax.ShapeDtypeStruct((m, n_cols), out_dtype),
        scratch_shapes=[pltpu.VMEM((k, tn), BF16)],
        compiler_params=pltpu.CompilerParams(
            dimension_semantics=("arbitrary", "arbitrary"),
            vmem_limit_bytes=_vmem_limit(blocks, _nbytes((k, tn), BF16))),
        name="matmul_res" if residual is not None else "matmul",
    )(*args)


def _rope(t, cos, sin_signed):
    half = RET_DK // 2
    rot = jnp.concatenate([pltpu.roll(t[:, :half], half // 2, 1),
                           pltpu.roll(t[:, half:], half // 2, 1)], axis=1)
    return t * cos + rot * sin_signed


def _in_proj_kernel(a_ref, w_ref, cos_ref, sin_ref, o_ref, wbf_ref, *, q_tiles, k_tiles):
    j = pl.program_id(0)

    @pl.when(j < q_tiles + k_tiles)
    def _rotated():
        scale = jnp.where(j < q_tiles, 1.0, RET_DK ** -0.5)
        t = _matmul_acc([a_ref], w_ref, wbf_ref) * scale
        cos = cos_ref[...]
        sin = sin_ref[...]
        for hh in range(t.shape[1] // RET_DK):
            cols = slice(hh * RET_DK, (hh + 1) * RET_DK)
            o_ref[:, cols] = _rope(t[:, cols], cos, sin).astype(o_ref.dtype)

    @pl.when(j >= q_tiles + k_tiles)
    def _plain():
        o_ref[...] = _matmul_acc([a_ref], w_ref, wbf_ref).astype(o_ref.dtype)


def _in_proj(a, w, cos, sin_signed, qk_w, tm, tn):
    m, k = a.shape
    n = w.shape[1]
    assert qk_w % tn == 0 and tn % RET_DK == 0
    q_tiles = qk_w // tn
    rot_tiles = 2 * q_tiles

    def table_idx(j, i):
        return (jnp.where(j < rot_tiles, i, 0), 0)

    blocks = (_nbytes((tm, k), BF16) + _nbytes((k, tn), F32) + _nbytes((tm, tn), BF16)
              + 2 * _nbytes((tm, RET_DK), F32))
    return pl.pallas_call(
        functools.partial(_in_proj_kernel, q_tiles=q_tiles, k_tiles=q_tiles),
        grid=(n // tn, m // tm),
        in_specs=[pl.BlockSpec((tm, k), lambda j, i: (i, 0)),
                  pl.BlockSpec((k, tn), lambda j, i: (0, j)),
                  pl.BlockSpec((tm, RET_DK), table_idx),
                  pl.BlockSpec((tm, RET_DK), table_idx)],
        out_specs=pl.BlockSpec((tm, tn), lambda j, i: (i, j)),
        out_shape=jax.ShapeDtypeStruct((m, n), BF16),
        scratch_shapes=[pltpu.VMEM((k, tn), BF16)],
        compiler_params=pltpu.CompilerParams(
            dimension_semantics=("arbitrary", "arbitrary"),
            vmem_limit_bytes=_vmem_limit(blocks, _nbytes((k, tn), BF16))),
        name="in_proj",
    )(a, w, cos, sin_signed)


def _ctx_state_kernel(lgf_ref, lgb_ref, k_ref, v_ref, sf_ref, sb_ref):
    h = pl.program_id(0)
    n = k_ref.shape[0]
    pos = lax.broadcasted_iota(jnp.int32, (n, 1), 0).astype(F32)
    k = k_ref[...] * (RET_DK ** -0.5)
    v = v_ref[...].astype(BF16)
    wf = jnp.exp((n - 1.0 - pos) * lgf_ref[h])
    wb = jnp.exp(pos * lgb_ref[h])
    tn_dims = (((0,), (0,)), ((), ()))
    sf_ref[0] = lax.dot_general((k * wf).astype(BF16), v, tn_dims, preferred_element_type=F32)
    sb_ref[0] = lax.dot_general((k * wb).astype(BF16), v, tn_dims, preferred_element_type=F32)


def _ctx_state(kv, lg_f, lg_b, n_heads):
    n = kv.shape[0]
    smem = pl.BlockSpec(memory_space=pltpu.SMEM)
    st = jax.ShapeDtypeStruct((n_heads, RET_DK, RET_DV), F32)
    return pl.pallas_call(
        _ctx_state_kernel,
        grid=(n_heads,),
        in_specs=[smem, smem,
                  pl.BlockSpec((n, RET_DK), lambda h: (0, h)),
                  pl.BlockSpec((n, RET_DV), lambda h: (0, n_heads + h))],
        out_specs=[pl.BlockSpec((1, RET_DK, RET_DV), lambda h: (h, 0, 0)),
                   pl.BlockSpec((1, RET_DK, RET_DV), lambda h: (h, 0, 0))],
        out_shape=[st, st],
        compiler_params=pltpu.CompilerParams(dimension_semantics=("arbitrary",)),
        name="ctx_state",
    )(lg_f, lg_b, kv, kv)


def _retention_kernel(lgf_ref, lgb_ref, q_ref, k_ref, v_ref, g_ref,
                      s0f_ref, s0b_ref, gn_ref, o_ref, state_ref, yb_ref, *, n_steps):
    h = pl.program_id(0)
    p = pl.program_id(1)
    s = pl.program_id(2)
    c = RET_SUB
    n_sub = q_ref.shape[0] // c
    ii = lax.broadcasted_iota(jnp.int32, (c, c), 0)
    jj = lax.broadcasted_iota(jnp.int32, (c, c), 1)
    idx = lax.broadcasted_iota(jnp.int32, (c, 1), 0).astype(F32)
    nt_dims = (((1,), (1,)), ((), ()))
    tn_dims = (((0,), (0,)), ((), ()))

    def chunk(ci, decay, q_decay, k_decay, chunk_decay):
        rows = pl.ds(ci * c, c)
        qb = q_ref[rows, :]
        kb = k_ref[rows, :]
        v = v_ref[rows, :]
        st = state_ref[...]
        scores = lax.dot_general(qb, kb, nt_dims, preferred_element_type=F32) * decay
        out = (jnp.dot(scores.astype(BF16), v, preferred_element_type=F32)
               + jnp.dot(qb, st.astype(BF16), preferred_element_type=F32) * q_decay)
        state_ref[...] = st * chunk_decay + lax.dot_general(
            (kb.astype(F32) * k_decay).astype(BF16), v, tn_dims, preferred_element_type=F32)
        return out

    @pl.when(p == 0)
    def _backward():
        lg = lgb_ref[h]
        blk = n_steps - 1 - s

        @pl.when(s == 0)
        def _():
            state_ref[...] = s0b_ref[0]

        mask = jj > ii
        decay = jnp.where(mask, jnp.exp(jnp.where(mask, jj - ii, 0).astype(F32) * lg), 0.0)
        q_decay = jnp.exp((c - idx) * lg)
        k_decay = jnp.exp(idx * lg)
        chunk_decay = jnp.exp(jnp.full((1, RET_DV), c, F32) * lg)
        for ci in reversed(range(n_sub)):
            out = chunk(ci, decay, q_decay, k_decay, chunk_decay)
            start = pl.multiple_of(blk * (n_sub * c) + ci * c, c)
            yb_ref[pl.ds(start, c), :] = out

    @pl.when(p == 1)
    def _forward():
        lg = lgf_ref[h]

        @pl.when(s == 0)
        def _():
            state_ref[...] = s0f_ref[0]

        mask = ii >= jj
        decay = jnp.where(mask, jnp.exp(jnp.where(mask, ii - jj, 0).astype(F32) * lg), 0.0)
        q_decay = jnp.exp((idx + 1.0) * lg)
        k_decay = jnp.exp((c - 1.0 - idx) * lg)
        chunk_decay = jnp.exp(jnp.full((1, RET_DV), c, F32) * lg)
        for ci in range(n_sub):
            out = chunk(ci, decay, q_decay, k_decay, chunk_decay)
            start = pl.multiple_of(s * (n_sub * c) + ci * c, c)
            y = out + yb_ref[pl.ds(start, c), :]
            mu = jnp.mean(y, axis=-1, keepdims=True)
            var = jnp.mean(jnp.square(y - mu), axis=-1, keepdims=True)
            yn = (y - mu) * lax.rsqrt(var + EPS) * gn_ref[...]
            gate = g_ref[pl.ds(ci * c, c), :].astype(F32)
            o_ref[pl.ds(ci * c, c), :] = (gate * jax.nn.sigmoid(gate) * yn).astype(o_ref.dtype)


def _retention(p, lg_f, lg_b, s0_f, s0_b, gn_g, n_heads):
    n_tok = p.shape[0]
    tl = RET_TILE
    n_steps = n_tok // tl
    smem = pl.BlockSpec(memory_space=pltpu.SMEM)

    def seq_blk(p_, s_):
        return jnp.where(p_ == 0, n_steps - 1 - s_, s_)

    blocks = 5 * _nbytes((tl, RET_DK), BF16) + 2 * _nbytes((RET_DK, RET_DV), F32)
    scratch = _nbytes((RET_DK, RET_DV), F32) + _nbytes((n_tok, RET_DV), F32)
    return pl.pallas_call(
        functools.partial(_retention_kernel, n_steps=n_steps),
        grid=(n_heads, 2, n_steps),
        in_specs=[smem, smem,
                  pl.BlockSpec((tl, RET_DK), lambda h, p_, s_: (seq_blk(p_, s_), h)),
                  pl.BlockSpec((tl, RET_DK), lambda h, p_, s_: (seq_blk(p_, s_), n_heads + h)),
                  pl.BlockSpec((tl, RET_DV), lambda h, p_, s_: (seq_blk(p_, s_), 2 * n_heads + h)),
                  pl.BlockSpec((tl, RET_DV), lambda h, p_, s_: (s_ * p_, 3 * n_heads + h)),
                  pl.BlockSpec((1, RET_DK, RET_DV), lambda h, p_, s_: (h, 0, 0)),
                  pl.BlockSpec((1, RET_DK, RET_DV), lambda h, p_, s_: (h, 0, 0)),
                  pl.BlockSpec((1, RET_DV), lambda h, p_, s_: (0, h))],
        out_specs=pl.BlockSpec((tl, RET_DV), lambda h, p_, s_: (s_ * p_, h)),
        out_shape=jax.ShapeDtypeStruct((n_tok, n_heads * RET_DV), BF16),
        scratch_shapes=[pltpu.VMEM((RET_DK, RET_DV), F32), pltpu.VMEM((n_tok, RET_DV), F32)],
        compiler_params=pltpu.CompilerParams(
            dimension_semantics=("arbitrary", "arbitrary", "arbitrary"),
            vmem_limit_bytes=_vmem_limit(blocks, scratch)),
        name="retention",
    )(lg_f, lg_b, p, p, p, p, s0_f, s0_b, gn_g)


def _sgu_kernel(u_ref, sv_ref, lng_ref, lnb_ref, ws_ref, bs_ref, o_ref):
    c = SGU_CHUNK
    ws = ws_ref[0].astype(BF16)
    bs = bs_ref[0]
    for ci in range(u_ref.shape[0] // c):
        rows = pl.ds(ci * c, c)
        v32 = jax.nn.gelu(sv_ref[rows, :].astype(F32))
        mu = jnp.mean(v32, axis=-1, keepdims=True)
        var = jnp.mean(jnp.square(v32 - mu), axis=-1, keepdims=True)
        vn = (v32 - mu) * lax.rsqrt(var + EPS) * lng_ref[...] + lnb_ref[...]
        mixed = jnp.dot(ws, vn.astype(BF16), preferred_element_type=F32) + bs
        o_ref[rows, :] = (jax.nn.gelu(u_ref[rows, :].astype(F32)) * mixed).astype(o_ref.dtype)


def _sgu(p, ln_g, ln_b, w_s, b_s, u_off):
    n_tok = p.shape[0]
    n_groups = w_s.shape[0]
    tl = SEQ_TILE
    gd = SGU_GROUP_DIM
    ub = u_off // gd
    blocks = 2 * _nbytes((tl, gd), F32) + _nbytes((tl, gd), BF16) + 2 * _nbytes((SGU_CHUNK, V7X_LANES), F32)
    return pl.pallas_call(
        _sgu_kernel,
        grid=(n_tok // tl, n_groups),
        in_specs=[pl.BlockSpec((tl, gd), lambda i, g: (i, ub + g)),
                  pl.BlockSpec((tl, gd), lambda i, g: (i, ub + n_groups + g)),
                  pl.BlockSpec((1, gd), lambda i, g: (0, g)),
                  pl.BlockSpec((1, gd), lambda i, g: (0, g)),
                  pl.BlockSpec((1, SGU_CHUNK, SGU_CHUNK), lambda i, g: (g, 0, 0)),
                  pl.BlockSpec((1, SGU_CHUNK, 1), lambda i, g: (g, 0, 0))],
        out_specs=pl.BlockSpec((tl, gd), lambda i, g: (i, g)),
        out_shape=jax.ShapeDtypeStruct((n_tok, n_groups * gd), BF16),
        compiler_params=pltpu.CompilerParams(
            dimension_semantics=("arbitrary", "arbitrary"),
            vmem_limit_bytes=_vmem_limit(blocks, 0)),
        name="sgu",
    )(p, p, ln_g, ln_b, w_s, b_s)


def _split_bf16(t):
    hi = t.astype(BF16)
    lo = (t - hi.astype(F32)).astype(BF16)
    return hi, lo


def _router_kernel(x_ref, g_ref, sh_ref, sc_ref, wr_ref, br_ref, h_ref, eid_ref, ew_ref):
    h = _rms_mod(x_ref[...], g_ref[...], sh_ref[0:1, :], sc_ref[0:1, :])
    half = h.shape[1] // 2
    bits = lax.bitcast_convert_type(h.astype(BF16).astype(F32), jnp.uint32)
    h_ref[...] = (bits[:, half:] & jnp.uint32(0xFFFF0000)) | (bits[:, :half] >> 16)
    h_hi, h_lo = _split_bf16(h)
    w_hi, w_lo = _split_bf16(wr_ref[...])
    logits = (jnp.dot(h_hi, w_hi, preferred_element_type=F32)
              + jnp.dot(h_lo, w_hi, preferred_element_type=F32)
              + jnp.dot(h_hi, w_lo, preferred_element_type=F32)) + br_ref[...]
    lane = lax.broadcasted_iota(jnp.int32, logits.shape, 1)
    lane_f = lane.astype(F32)
    neg = -jnp.inf

    def first_lane(hit):
        return jnp.min(jnp.where(hit, lane_f, float(V7X_LANES)), axis=-1, keepdims=True).astype(jnp.int32)

    gl = jnp.where(lane < MOE_GROUPS, logits, neg)
    g_max = jnp.max(gl, axis=-1, keepdims=True)
    g_sel = first_lane(gl == g_max)
    p_g = 1.0 / jnp.sum(jnp.exp(gl - g_max), axis=-1, keepdims=True)
    e_lo = MOE_GROUPS + g_sel * EXPERTS_PER_GROUP
    el = jnp.where((lane >= e_lo) & (lane < e_lo + EXPERTS_PER_GROUP), logits, neg)
    v1 = jnp.max(el, axis=-1, keepdims=True)
    i1 = first_lane(el == v1)
    el2 = jnp.where(lane == i1, neg, el)
    v2 = jnp.max(el2, axis=-1, keepdims=True)
    i2 = first_lane(el2 == v2)
    e2 = jnp.exp(v2 - v1)
    den = 1.0 + e2
    w1 = p_g * (1.0 / den)
    w2 = p_g * (e2 / den)
    eid_ref[...] = jnp.where(lane == 0, i1 - MOE_GROUPS, jnp.where(lane == 1, i2 - MOE_GROUPS, 0))
    ew_ref[...] = jnp.where(lane == 0, w1, jnp.where(lane == 1, w2, 0.0))


def _router(x, g, mod, shift_chunk, scale_chunk, wr, br):
    m, d = x.shape
    tm = ROW_TILE
    blocks = 2 * _nbytes((tm, d), F32) + 2 * _nbytes((MOD_ROWS, d), F32) + _nbytes((d, V7X_LANES), F32) \
        + 2 * _nbytes((tm, V7X_LANES), F32)
    return pl.pallas_call(
        _router_kernel,
        grid=(m // tm,),
        in_specs=[pl.BlockSpec((tm, d), lambda i: (i, 0)),
                  pl.BlockSpec((1, d), lambda i: (0, 0)),
                  pl.BlockSpec((MOD_ROWS, d), lambda i: (0, shift_chunk)),
                  pl.BlockSpec((MOD_ROWS, d), lambda i: (0, scale_chunk)),
                  pl.BlockSpec((d, V7X_LANES), lambda i: (0, 0)),
                  pl.BlockSpec((1, V7X_LANES), lambda i: (0, 0))],
        out_specs=[pl.BlockSpec((tm, d // 2), lambda i: (i, 0)),
                   pl.BlockSpec((tm, V7X_LANES), lambda i: (i, 0)),
                   pl.BlockSpec((tm, V7X_LANES), lambda i: (i, 0))],
        out_shape=[jax.ShapeDtypeStruct((m, d // 2), jnp.uint32),
                   jax.ShapeDtypeStruct((m, V7X_LANES), jnp.int32),
                   jax.ShapeDtypeStruct((m, V7X_LANES), F32)],
        compiler_params=pltpu.CompilerParams(
            dimension_semantics=("arbitrary",),
            vmem_limit_bytes=_vmem_limit(blocks, 0)),
        name="router",
    )(x, g, mod, mod, wr, br)


def _row_copy(src, src_row, dst, dst_row, sem):
    return pltpu.make_async_copy(src.at[pl.ds(src_row, 1), :], dst.at[pl.ds(dst_row, 1), :], sem)


def _for_rows(n_rows, fn):
    n_groups = lax.shift_right_logical(n_rows, DMA_UNROLL.bit_length() - 1)

    def group(gi, carry):
        for u in range(DMA_UNROLL):
            fn(gi * DMA_UNROLL + u, u)
        return carry

    def single(r, carry):
        fn(r, 0)
        return carry

    lax.fori_loop(0, n_groups, group, 0)
    lax.fori_loop(n_groups * DMA_UNROLL, n_rows, single, 0)


def _experts_kernel(be_ref, nact_ref, nv_ref, cs_ref, stok_ref, sdst_ref,
                    h_hbm, wg_ref, wu_ref, wd_ref, y_hbm,
                    x_ref, acc_ref, wgb_ref, wub_ref, wdb_ref, sem_in, sem_out):
    del be_ref
    b = pl.program_id(0)
    f = pl.program_id(1)
    n_active = nact_ref[0]
    active = b < n_active
    slot = lax.rem(b, 2)
    sub = EXPERT_SUB
    last_f = EXPERT_SPLIT - 1

    def gather(bb, slot_, wait):
        def one(r, u):
            cp = _row_copy(h_hbm, stok_ref[cs_ref[bb] + r], x_ref.at[slot_], r, sem_in.at[slot_])
            cp.wait() if wait else cp.start(priority=u % N_DMA_PRIORITIES)

        _for_rows(nv_ref[bb], one)

    def scatter(bb, wait):
        def one(r, u):
            par = lax.rem(bb, 2)
            cp = _row_copy(acc_ref.at[par], r, y_hbm, sdst_ref[cs_ref[bb] + r], sem_out.at[par])
            cp.wait() if wait else cp.start(priority=u % N_DMA_PRIORITIES)

        _for_rows(nv_ref[bb], one)

    @pl.when(active & (f == 0))
    def _rows_in():
        @pl.when(b == 0)
        def _():
            x_ref[...] = jnp.zeros(x_ref.shape, x_ref.dtype)
            gather(0, 0, False)

        @pl.when(b + 1 < n_active)
        def _():
            gather(b + 1, 1 - slot, False)

        gather(b, slot, True)

    @pl.when(active)
    def _compute():
        wgb_ref[...] = wg_ref[0].astype(BF16)
        wub_ref[...] = wu_ref[0].astype(BF16)
        wdb_ref[...] = wd_ref[0].astype(BF16)
        half = wgb_ref.shape[0] // 2
        for j in range(x_ref.shape[1] // sub):
            @pl.when(j * sub < nv_ref[b])
            def _sub_block():
                rows = pl.ds(j * sub, sub)
                xw = x_ref[slot, rows, :]
                xa = lax.bitcast_convert_type(xw << 16, F32).astype(BF16)
                xb = lax.bitcast_convert_type(xw & jnp.uint32(0xFFFF0000), F32).astype(BF16)
                gate = (jnp.dot(xa, wgb_ref[:half, :], preferred_element_type=F32)
                        + jnp.dot(xb, wgb_ref[half:, :], preferred_element_type=F32))
                up = (jnp.dot(xa, wub_ref[:half, :], preferred_element_type=F32)
                      + jnp.dot(xb, wub_ref[half:, :], preferred_element_type=F32))
                hid = (gate * jax.nn.sigmoid(gate) * up).astype(BF16)

                @pl.when(f == 0)
                def _():
                    acc_ref[slot, rows, :] = jnp.dot(hid, wdb_ref[...], preferred_element_type=F32)

                @pl.when(f > 0)
                def _():
                    acc_ref[slot, rows, :] += jnp.dot(hid, wdb_ref[...], preferred_element_type=F32)

    @pl.when(active & (f == last_f))
    def _rows_out():
        @pl.when(b > 0)
        def _():
            scatter(b - 1, True)

        scatter(b, False)

        @pl.when(b == n_active - 1)
        def _():
            scatter(b, True)


def _experts(h, w_gate, w_up, w_down, plan, n_out_rows):
    block_e, n_active, n_valid, c_start, s_tok, s_dst = plan
    d = w_gate.shape[1]
    de = w_gate.shape[-1]
    des = de // EXPERT_SPLIT
    n_blocks = block_e.shape[0]
    assert h.shape[1] * 2 == d and EXPERT_SPLIT >= 2 and EXPERT_ROWS % EXPERT_SUB == 0

    def w_idx(transpose):
        def index_map(b, f, be, nact, *_):
            fs = jnp.where(b < nact[0], f, EXPERT_SPLIT - 1)
            return (be[b], fs, 0) if transpose else (be[b], 0, fs)
        return index_map

    blocks = 3 * _nbytes((d, des), F32)
    scratch = (_nbytes((2, EXPERT_ROWS, d // 2), jnp.uint32) + _nbytes((2, EXPERT_ROWS, d), F32)
               + 3 * _nbytes((d, des), BF16))
    grid_spec = pltpu.PrefetchScalarGridSpec(
        num_scalar_prefetch=6,
        grid=(n_blocks, EXPERT_SPLIT),
        in_specs=[pl.BlockSpec(memory_space=pl.ANY),
                  pl.BlockSpec((1, d, des), w_idx(False)),
                  pl.BlockSpec((1, d, des), w_idx(False)),
                  pl.BlockSpec((1, des, d), w_idx(True))],
        out_specs=pl.BlockSpec(memory_space=pl.ANY),
        scratch_shapes=[pltpu.VMEM((2, EXPERT_ROWS, d // 2), jnp.uint32),
                        pltpu.VMEM((2, EXPERT_ROWS, d), F32),
                        pltpu.VMEM((d, des), BF16), pltpu.VMEM((d, des), BF16), pltpu.VMEM((des, d), BF16),
                        pltpu.SemaphoreType.DMA((2,)), pltpu.SemaphoreType.DMA((2,))],
    )
    return pl.pallas_call(
        _experts_kernel,
        grid_spec=grid_spec,
        out_shape=jax.ShapeDtypeStruct((n_out_rows, d), F32),
        compiler_params=pltpu.CompilerParams(
            dimension_semantics=("arbitrary", "arbitrary"),
            vmem_limit_bytes=_vmem_limit(blocks, scratch)),
        name="experts",
    )(block_e, n_active, n_valid, c_start, s_tok, s_dst, h, w_gate, w_up, w_down)


def _combine_kernel(*refs, top_k):
    y_refs = refs[:top_k]
    w_ref, x_ref, g_ref, fg_ref, o_ref = refs[top_k:]
    y = y_refs[0][0] * w_ref[:, 0:1]
    for k in range(1, top_k):
        y = y + y_refs[k][0] * w_ref[:, k:k + 1]
    x = x_ref[...] + g_ref[0:1, :] * y
    o_ref[...] = x * lax.rsqrt(jnp.mean(x * x, axis=-1, keepdims=True) + EPS) * fg_ref[...]


def _combine(y, ew, x, mod, gate_chunk, final_g, top_k):
    m, d = x.shape
    tm = ROW_TILE
    blocks = (top_k + 2) * _nbytes((tm, d), F32) + _nbytes((MOD_ROWS, d), F32) + _nbytes((tm, V7X_LANES), F32)
    in_specs = [pl.BlockSpec((1, tm, d), functools.partial(lambda k, i: (k, i, 0), k)) for k in range(top_k)]
    in_specs += [pl.BlockSpec((tm, V7X_LANES), lambda i: (i, 0)),
                 pl.BlockSpec((tm, d), lambda i: (i, 0)),
                 pl.BlockSpec((MOD_ROWS, d), lambda i: (0, gate_chunk)),
                 pl.BlockSpec((1, d), lambda i: (0, 0))]
    return pl.pallas_call(
        functools.partial(_combine_kernel, top_k=top_k),
        grid=(m // tm,),
        in_specs=in_specs,
        out_specs=pl.BlockSpec((tm, d), lambda i: (i, 0)),
        out_shape=jax.ShapeDtypeStruct((m, d), F32),
        compiler_params=pltpu.CompilerParams(
            dimension_semantics=("arbitrary",),
            vmem_limit_bytes=_vmem_limit(blocks, 0)),
        name="combine",
    )(*([y] * top_k), ew, x, mod, final_g)


def _rope_tables(n_tokens):
    n_rows = n_tokens // GRID_W
    rows = jnp.repeat(jnp.arange(n_rows), GRID_W)
    cols = jnp.tile(jnp.arange(GRID_W), n_rows)
    n_freq = RET_DK // 4
    freqs = ROPE_BASE ** (-jnp.arange(n_freq, dtype=F32) / n_freq)
    pos = jnp.stack([rows, cols], axis=-1).astype(F32)
    ang = pos[:, :, None, None] * freqs
    ang = jnp.broadcast_to(ang, (n_tokens, 2, 2, n_freq)).reshape(n_tokens, RET_DK)
    sign = jnp.tile(jnp.concatenate([-jnp.ones((n_freq,), F32), jnp.ones((n_freq,), F32)]), 2)
    return jnp.cos(ang), jnp.sin(ang) * sign


def _dispatch(eid, n_tok, top_k):
    m = n_tok * top_k
    n_blocks = -(-m // EXPERT_ROWS) + N_EXPERTS
    e_flat = eid[:, :top_k].reshape(-1)
    order = jnp.argsort(e_flat).astype(jnp.int32)
    s_tok = order // top_k
    s_dst = (order % top_k) * n_tok + s_tok
    counts = jnp.sum((e_flat[:, None] == jnp.arange(N_EXPERTS, dtype=jnp.int32)[None, :]).astype(jnp.int32), axis=0)
    starts = jnp.cumsum(counts) - counts
    e_blocks = (counts + EXPERT_ROWS - 1) // EXPERT_ROWS
    b_ends = jnp.cumsum(e_blocks)
    n_active = b_ends[-1].astype(jnp.int32)
    blk = jnp.arange(n_blocks, dtype=jnp.int32)
    block_e = jnp.searchsorted(b_ends, jnp.minimum(blk, n_active - 1), side='right')
    block_e = jnp.minimum(block_e, N_EXPERTS - 1).astype(jnp.int32)
    within = blk - (b_ends - e_blocks)[block_e]
    c_start = (starts[block_e] + within * EXPERT_ROWS).astype(jnp.int32)
    n_valid = jnp.where(blk < n_active, jnp.clip(counts[block_e] - within * EXPERT_ROWS, 0, EXPERT_ROWS), 0)
    n_valid = n_valid.astype(jnp.int32)
    return block_e, n_active.reshape(1), n_valid, c_start, s_tok, s_dst


def kernel(x, c, ctx, c_ctx, w_ada, b_ada, norm1_g, norm2_g, w_in, ret_decay_f, ret_decay_b, ret_gn_g, sgu_ln_g, sgu_ln_b, sgu_w_s, sgu_b_s, w_out, w_router_group, b_router_group, w_router_expert, b_router_expert, w_gate, w_up, w_down, final_g):
    batch, n_tok, d = x.shape
    assert batch == 1 and w_ada.shape[0] == 1
    n_heads = ret_decay_f.shape[-1]
    n_groups = sgu_w_s.shape[1]
    ret_qk_w = n_heads * RET_DK
    ret_w = n_heads * RET_DV
    sgu_w = n_groups * SGU_GROUP_DIM
    k_off = ret_qk_w
    u_off = 2 * ret_qk_w + 2 * ret_w
    in_w = u_off + 2 * sgu_w
    top_k = 2
    assert w_in.shape == (1, d, in_w) and w_out.shape == (1, ret_w + sgu_w, d)

    cc = jnp.zeros((MOD_ROWS, d), F32).at[0].set(c[0]).at[1].set(c_ctx)
    mod = _ada(cc, w_ada[0], b_ada[0].reshape(1, N_MOD * d))

    lg_f = -jnp.exp(ret_decay_f[0])
    lg_b = -jnp.exp(ret_decay_b[0])

    hc = _norm_mod(ctx[0], norm1_g, mod, 1, 0, 1, ROW_TILE)
    kv_c = _matmul([hc], w_in[0], ret_qk_w + ret_w, k_off, ctx.shape[1], MM_TN)
    s_f, s_b = _ctx_state(kv_c, lg_f, lg_b, n_heads)

    h1 = _norm_mod(x[0], norm1_g, mod, 0, 0, 1, ROW_TILE)
    cos, sin_signed = _rope_tables(n_tok)
    p = _in_proj(h1, w_in[0], cos, sin_signed, ret_qk_w, IN_TM, IN_TN)
    ret_out = _retention(p, lg_f, lg_b, s_f, s_b, ret_gn_g, n_heads)
    sgu_out = _sgu(p, sgu_ln_g, sgu_ln_b, sgu_w_s[0], sgu_b_s[0].reshape(n_groups, SGU_CHUNK, 1), u_off)
    x1 = _matmul([ret_out, sgu_out], w_out[0], d, 0, MM_TM, MM_TN, residual=(x[0], mod, 2))

    n_router = MOE_GROUPS + N_EXPERTS
    wr = jnp.zeros((d, V7X_LANES), F32).at[:, :MOE_GROUPS].set(w_router_group[0]) \
        .at[:, MOE_GROUPS:n_router].set(w_router_expert[0])
    br = jnp.zeros((1, V7X_LANES), F32).at[0, :MOE_GROUPS].set(b_router_group[0]) \
        .at[0, MOE_GROUPS:n_router].set(b_router_expert[0])
    h2, eid, ew = _router(x1, norm2_g, mod, 3, 4, wr, br)
    plan = _dispatch(eid, n_tok, top_k)
    y = _experts(h2, w_gate[0], w_up[0], w_down[0], plan, top_k * n_tok)
    out = _combine(y.reshape(top_k, n_tok, d), ew, x1, mod, 5, final_g.reshape(1, d), top_k)
    return out.reshape(batch, n_tok, d)
```

```python
import functools

import jax
import jax.numpy as jnp
from jax import lax
from jax.experimental import pallas as pl
from jax.experimental.pallas import tpu as pltpu

F32 = jnp.float32
BF16 = jnp.bfloat16

GRID_W = 64
RET_DK = 256
RET_DV = 256
ROPE_BASE = 10000.0
SGU_GROUP_DIM = 256
SGU_CHUNK = 128
MOE_GROUPS = 8
EXPERTS_PER_GROUP = 8
N_EXPERTS = MOE_GROUPS * EXPERTS_PER_GROUP
N_MOD = 6
EPS = 1e-6

V7X_LANES = 128
V7X_VMEM_BYTES = 64 * 1024 * 1024
MOD_ROWS = 8

ADA_TN = 512
MM_TN = 512
IN_TM = 512
IN_TN = 1024
ROW_TILE = 256
SEQ_TILE = 1024
RET_TILE = 1024
RET_SUB = 256
EXPERT_ROWS = 512
EXPERT_SUB = 256
EXPERT_SPLIT = 2
DMA_UNROLL = 8
N_DMA_PRIORITIES = 2


def _vmem_limit(block_bytes, scratch_bytes):
    want = 2 * block_bytes + scratch_bytes
    return int(min(V7X_VMEM_BYTES - 4 * 1024 * 1024, max(2 * want, 32 * 1024 * 1024)))


def _nbytes(shape, dtype):
    n = 1
    for s in shape:
        n *= s
    return n * jnp.dtype(dtype).itemsize


def _ada_kernel(cc_ref, w_ref, b_ref, o_ref):
    a = cc_ref[...]
    s = (a * jax.nn.sigmoid(a)).astype(BF16)
    o_ref[...] = jnp.dot(s, w_ref[...].astype(BF16), preferred_element_type=F32) + b_ref[...]


def _ada(cc, w, b):
    d, n = w.shape
    blocks = _nbytes((MOD_ROWS, d), F32) + _nbytes((d, ADA_TN), F32) + 2 * _nbytes((MOD_ROWS, ADA_TN), F32)
    return pl.pallas_call(
        _ada_kernel,
        grid=(n // ADA_TN,),
        in_specs=[pl.BlockSpec((MOD_ROWS, d), lambda j: (0, 0)),
                  pl.BlockSpec((d, ADA_TN), lambda j: (0, j)),
                  pl.BlockSpec((1, ADA_TN), lambda j: (0, j))],
        out_specs=pl.BlockSpec((MOD_ROWS, ADA_TN), lambda j: (0, j)),
        out_shape=jax.ShapeDtypeStruct((MOD_ROWS, n), F32),
        compiler_params=pltpu.CompilerParams(
            dimension_semantics=("arbitrary",),
            vmem_limit_bytes=_vmem_limit(blocks, _nbytes((d, ADA_TN), BF16))),
        name="ada",
    )(cc, w, b)


def _rms_mod(x, g, shift, scale):
    y = x * lax.rsqrt(jnp.mean(x * x, axis=-1, keepdims=True) + EPS) * g
    return y * (1.0 + scale) + shift


def _norm_mod_kernel(x_ref, g_ref, sh_ref, sc_ref, o_ref, *, row):
    h = _rms_mod(x_ref[...], g_ref[...], sh_ref[row:row + 1, :], sc_ref[row:row + 1, :])
    o_ref[...] = h.astype(o_ref.dtype)


def _norm_mod(x, g, mod, row, shift_chunk, scale_chunk, tm):
    m, d = x.shape
    blocks = _nbytes((tm, d), F32) * 2 + 3 * _nbytes((MOD_ROWS, d), F32)
    return pl.pallas_call(
        functools.partial(_norm_mod_kernel, row=row),
        grid=(m // tm,),
        in_specs=[pl.BlockSpec((tm, d), lambda i: (i, 0)),
                  pl.BlockSpec((1, d), lambda i: (0, 0)),
                  pl.BlockSpec((MOD_ROWS, d), lambda i: (0, shift_chunk)),
                  pl.BlockSpec((MOD_ROWS, d), lambda i: (0, scale_chunk))],
        out_specs=pl.BlockSpec((tm, d), lambda i: (i, 0)),
        out_shape=jax.ShapeDtypeStruct((m, d), BF16),
        compiler_params=pltpu.CompilerParams(
            dimension_semantics=("arbitrary",),
            vmem_limit_bytes=_vmem_limit(blocks, 0)),
        name="norm_mod",
    )(x, g, mod, mod)


def _matmul_acc(a_refs, w_ref, wbf_ref):
    @pl.when(pl.program_id(1) == 0)
    def _():
        wbf_ref[...] = w_ref[...].astype(BF16)

    acc = None
    k0 = 0
    for a_ref in a_refs:
        kk = a_ref.shape[1]
        part = jnp.dot(a_ref[...], wbf_ref[k0:k0 + kk, :], preferred_element_type=F32)
        acc = part if acc is None else acc + part
        k0 += kk
    return acc


def _matmul_kernel(*refs, n_a):
    w_ref, o_ref, wbf_ref = refs[n_a:]
    o_ref[...] = _matmul_acc(refs[:n_a], w_ref, wbf_ref).astype(o_ref.dtype)


def _matmul_res_kernel(*refs, n_a):
    w_ref, x_ref, g_ref, o_ref, wbf_ref = refs[n_a:]
    o_ref[...] = x_ref[...] + g_ref[0:1, :] * _matmul_acc(refs[:n_a], w_ref, wbf_ref)


def _matmul(a_list, w, n_cols, col_off, tm, tn, residual=None, out_dtype=F32):
    m = a_list[0].shape[0]
    k = sum(a.shape[1] for a in a_list)
    n_a = len(a_list)
    joff = col_off // tn
    grid = (n_cols // tn, m // tm)
    in_specs = [pl.BlockSpec((tm, a.shape[1]), lambda j, i: (i, 0)) for a in a_list]
    in_specs.append(pl.BlockSpec((k, tn), lambda j, i: (0, j + joff)))
    args = list(a_list) + [w]
    blocks = _nbytes((tm, k), BF16) + _nbytes((k, tn), F32) + _nbytes((tm, tn), F32)
    if residual is None:
        body = functools.partial(_matmul_kernel, n_a=n_a)
    else:
        x, mod, gate_chunk = residual
        goff = gate_chunk * (n_cols // tn)
        in_specs += [pl.BlockSpec((tm, tn), lambda j, i: (i, j)),
                     pl.BlockSpec((MOD_ROWS, tn), lambda j, i: (0, goff + j))]
        args += [x, mod]
        blocks += _nbytes((tm, tn), F32) + _nbytes((MOD_ROWS, tn), F32)
        body = functools.partial(_matmul_res_kernel, n_a=n_a)
    return pl.pallas_call(
        body,
        grid=grid,
        in_specs=in_specs,
        out_specs=pl.BlockSpec((tm, tn), lambda j, i: (i, j)),
        out_shape=jax.ShapeDtypeStruct((m, n_cols), out_dtype),
        scratch_shapes=[pltpu.VMEM((k, tn), BF16)],
        compiler_params=pltpu.CompilerParams(
            dimension_semantics=("arbitrary", "arbitrary"),
            vmem_limit_bytes=_vmem_limit(blocks, _nbytes((k, tn), BF16))),
        name="matmul_res" if residual is not None else "matmul",
    )(*args)


def _rope(t, cos, sin_signed):
    half = RET_DK // 2
    rot = jnp.concatenate([pltpu.roll(t[:, :half], half // 2, 1),
                           pltpu.roll(t[:, half:], half // 2, 1)], axis=1)
    return t * cos + rot * sin_signed


def _in_proj_kernel(a_ref, w_ref, cos_ref, sin_ref, lng_ref, lnb_ref, o_ref, wbf_ref, *, tile_ends):
    j = pl.program_id(0)
    q_end, k_end, plain_end, u_end = tile_ends

    def acc():
        return _matmul_acc([a_ref], w_ref, wbf_ref)

    @pl.when(j < k_end)
    def _rotated():
        t = acc() * jnp.where(j < q_end, 1.0, RET_DK ** -0.5)
        cos = cos_ref[...]
        sin = sin_ref[...]
        for hh in range(t.shape[1] // RET_DK):
            cols = slice(hh * RET_DK, (hh + 1) * RET_DK)
            o_ref[:, cols] = _rope(t[:, cols], cos, sin).astype(o_ref.dtype)

    @pl.when((j >= k_end) & (j < plain_end))
    def _plain():
        o_ref[...] = acc().astype(o_ref.dtype)

    @pl.when((j >= plain_end) & (j < u_end))
    def _gelu():
        o_ref[...] = jax.nn.gelu(acc()).astype(o_ref.dtype)

    @pl.when(j >= u_end)
    def _gelu_norm():
        t = jax.nn.gelu(acc())
        for gg in range(t.shape[1] // SGU_GROUP_DIM):
            cols = slice(gg * SGU_GROUP_DIM, (gg + 1) * SGU_GROUP_DIM)
            v32 = t[:, cols]
            mu = jnp.mean(v32, axis=-1, keepdims=True)
            var = jnp.mean(jnp.square(v32 - mu), axis=-1, keepdims=True)
            vn = (v32 - mu) * lax.rsqrt(var + EPS) * lng_ref[:, cols] + lnb_ref[:, cols]
            o_ref[:, cols] = vn.astype(o_ref.dtype)


def _in_proj(a, w, cos, sin_signed, ln_g, ln_b, qk_w, v_gate_w, sgu_w, tm, tn):
    m, k = a.shape
    n = w.shape[1]
    assert n == 2 * qk_w + v_gate_w + 2 * sgu_w
    assert qk_w % tn == 0 and v_gate_w % tn == 0 and sgu_w % tn == 0
    assert tn % RET_DK == 0 and tn % SGU_GROUP_DIM == 0
    q_end = qk_w // tn
    k_end = 2 * q_end
    plain_end = k_end + v_gate_w // tn
    u_end = plain_end + sgu_w // tn

    def table_idx(j, i):
        return (jnp.where(j < k_end, i, 0), 0)

    def ln_idx(j, i):
        return (0, jnp.maximum(j - u_end, 0))

    blocks = (_nbytes((tm, k), BF16) + _nbytes((k, tn), F32) + _nbytes((tm, tn), BF16)
              + 2 * _nbytes((tm, RET_DK), F32))
    return pl.pallas_call(
        functools.partial(_in_proj_kernel, tile_ends=(q_end, k_end, plain_end, u_end)),
        grid=(n // tn, m // tm),
        in_specs=[pl.BlockSpec((tm, k), lambda j, i: (i, 0)),
                  pl.BlockSpec((k, tn), lambda j, i: (0, j)),
                  pl.BlockSpec((tm, RET_DK), table_idx),
                  pl.BlockSpec((tm, RET_DK), table_idx),
                  pl.BlockSpec((1, tn), ln_idx),
                  pl.BlockSpec((1, tn), ln_idx)],
        out_specs=pl.BlockSpec((tm, tn), lambda j, i: (i, j)),
        out_shape=jax.ShapeDtypeStruct((m, n), BF16),
        scratch_shapes=[pltpu.VMEM((k, tn), BF16)],
        compiler_params=pltpu.CompilerParams(
            dimension_semantics=("arbitrary", "arbitrary"),
            vmem_limit_bytes=_vmem_limit(blocks, _nbytes((k, tn), BF16))),
        name="in_proj",
    )(a, w, cos, sin_signed, ln_g, ln_b)


def _ctx_state_kernel(lgf_ref, lgb_ref, k_ref, v_ref, sf_ref, sb_ref):
    h = pl.program_id(0)
    n = k_ref.shape[0]
    pos = lax.broadcasted_iota(jnp.int32, (n, 1), 0).astype(F32)
    k = k_ref[...] * (RET_DK ** -0.5)
    v = v_ref[...].astype(BF16)
    wf = jnp.exp((n - 1.0 - pos) * lgf_ref[h])
    wb = jnp.exp(pos * lgb_ref[h])
    tn_dims = (((0,), (0,)), ((), ()))
    sf_ref[0] = lax.dot_general((k * wf).astype(BF16), v, tn_dims, preferred_element_type=F32)
    sb_ref[0] = lax.dot_general((k * wb).astype(BF16), v, tn_dims, preferred_element_type=F32)


def _ctx_state(kv, lg_f, lg_b, n_heads):
    n = kv.shape[0]
    smem = pl.BlockSpec(memory_space=pltpu.SMEM)
    st = jax.ShapeDtypeStruct((n_heads, RET_DK, RET_DV), F32)
    return pl.pallas_call(
        _ctx_state_kernel,
        grid=(n_heads,),
        in_specs=[smem, smem,
                  pl.BlockSpec((n, RET_DK), lambda h: (0, h)),
                  pl.BlockSpec((n, RET_DV), lambda h: (0, n_heads + h))],
        out_specs=[pl.BlockSpec((1, RET_DK, RET_DV), lambda h: (h, 0, 0)),
                   pl.BlockSpec((1, RET_DK, RET_DV), lambda h: (h, 0, 0))],
        out_shape=[st, st],
        compiler_params=pltpu.CompilerParams(dimension_semantics=("arbitrary",)),
        name="ctx_state",
    )(lg_f, lg_b, kv, kv)


def _retention_kernel(lgf_ref, lgb_ref, q_ref, k_ref, v_ref, g_ref,
                      s0f_ref, s0b_ref, gn_ref, o_ref, state_ref, yb_ref, *, n_steps):
    h = pl.program_id(0)
    p = pl.program_id(1)
    s = pl.program_id(2)
    c = RET_SUB
    n_sub = q_ref.shape[0] // c
    ii = lax.broadcasted_iota(jnp.int32, (c, c), 0)
    jj = lax.broadcasted_iota(jnp.int32, (c, c), 1)
    idx = lax.broadcasted_iota(jnp.int32, (c, 1), 0).astype(F32)
    nt_dims = (((1,), (1,)), ((), ()))
    tn_dims = (((0,), (0,)), ((), ()))

    def chunk(ci, decay, q_decay, k_decay, chunk_decay):
        rows = pl.ds(ci * c, c)
        qb = q_ref[rows, :]
        kb = k_ref[rows, :]
        v = v_ref[rows, :]
        st = state_ref[...]
        scores = lax.dot_general(qb, kb, nt_dims, preferred_element_type=F32) * decay
        out = (jnp.dot(scores.astype(BF16), v, preferred_element_type=F32)
               + jnp.dot(qb, st.astype(BF16), preferred_element_type=F32) * q_decay)
        state_ref[...] = st * chunk_decay + lax.dot_general(
            (kb.astype(F32) * k_decay).astype(BF16), v, tn_dims, preferred_element_type=F32)
        return out

    @pl.when(p == 0)
    def _backward():
        lg = lgb_ref[h]
        blk = n_steps - 1 - s

        @pl.when(s == 0)
        def _():
            state_ref[...] = s0b_ref[0]

        mask = jj > ii
        decay = jnp.where(mask, jnp.exp(jnp.where(mask, jj - ii, 0).astype(F32) * lg), 0.0)
        q_decay = jnp.exp((c - idx) * lg)
        k_decay = jnp.exp(idx * lg)
        chunk_decay = jnp.exp(jnp.full((1, RET_DV), c, F32) * lg)
        for ci in reversed(range(n_sub)):
            out = chunk(ci, decay, q_decay, k_decay, chunk_decay)
            start = pl.multiple_of(blk * (n_sub * c) + ci * c, c)
            yb_ref[pl.ds(start, c), :] = out

    @pl.when(p == 1)
    def _forward():
        lg = lgf_ref[h]

        @pl.when(s == 0)
        def _():
            state_ref[...] = s0f_ref[0]

        mask = ii >= jj
        decay = jnp.where(mask, jnp.exp(jnp.where(mask, ii - jj, 0).astype(F32) * lg), 0.0)
        q_decay = jnp.exp((idx + 1.0) * lg)
        k_decay = jnp.exp((c - 1.0 - idx) * lg)
        chunk_decay = jnp.exp(jnp.full((1, RET_DV), c, F32) * lg)
        for ci in range(n_sub):
            out = chunk(ci, decay, q_decay, k_decay, chunk_decay)
            start = pl.multiple_of(s * (n_sub * c) + ci * c, c)
            y = out + yb_ref[pl.ds(start, c), :]
            mu = jnp.mean(y, axis=-1, keepdims=True)
            var = jnp.mean(jnp.square(y - mu), axis=-1, keepdims=True)
            yn = (y - mu) * lax.rsqrt(var + EPS) * gn_ref[...]
            gate = g_ref[pl.ds(ci * c, c), :].astype(F32)
            o_ref[pl.ds(ci * c, c), :] = (gate * jax.nn.sigmoid(gate) * yn).astype(o_ref.dtype)


def _retention(p, lg_f, lg_b, s0_f, s0_b, gn_g, n_heads):
    n_tok = p.shape[0]
    tl = RET_TILE
    n_steps = n_tok // tl
    smem = pl.BlockSpec(memory_space=pltpu.SMEM)

    def seq_blk(p_, s_):
        return jnp.where(p_ == 0, n_steps - 1 - s_, s_)

    blocks = 5 * _nbytes((tl, RET_DK), BF16) + 2 * _nbytes((RET_DK, RET_DV), F32)
    scratch = _nbytes((RET_DK, RET_DV), F32) + _nbytes((n_tok, RET_DV), F32)
    return pl.pallas_call(
        functools.partial(_retention_kernel, n_steps=n_steps),
        grid=(n_heads, 2, n_steps),
        in_specs=[smem, smem,
                  pl.BlockSpec((tl, RET_DK), lambda h, p_, s_: (seq_blk(p_, s_), h)),
                  pl.BlockSpec((tl, RET_DK), lambda h, p_, s_: (seq_blk(p_, s_), n_heads + h)),
                  pl.BlockSpec((tl, RET_DV), lambda h, p_, s_: (seq_blk(p_, s_), 2 * n_heads + h)),
                  pl.BlockSpec((tl, RET_DV), lambda h, p_, s_: (s_ * p_, 3 * n_heads + h)),
                  pl.BlockSpec((1, RET_DK, RET_DV), lambda h, p_, s_: (h, 0, 0)),
                  pl.BlockSpec((1, RET_DK, RET_DV), lambda h, p_, s_: (h, 0, 0)),
                  pl.BlockSpec((1, RET_DV), lambda h, p_, s_: (0, h))],
        out_specs=pl.BlockSpec((tl, RET_DV), lambda h, p_, s_: (s_ * p_, h)),
        out_shape=jax.ShapeDtypeStruct((n_tok, n_heads * RET_DV), BF16),
        scratch_shapes=[pltpu.VMEM((RET_DK, RET_DV), F32), pltpu.VMEM((n_tok, RET_DV), F32)],
        compiler_params=pltpu.CompilerParams(
            dimension_semantics=("arbitrary", "arbitrary", "arbitrary"),
            vmem_limit_bytes=_vmem_limit(blocks, scratch)),
        name="retention",
    )(lg_f, lg_b, p, p, p, p, s0_f, s0_b, gn_g)


def _sgu_kernel(u_ref, vn_ref, ws_ref, bs_ref, o_ref):
    c = SGU_CHUNK
    ws = ws_ref[0].astype(BF16)
    bs = bs_ref[0]
    for ci in range(u_ref.shape[0] // c):
        rows = pl.ds(ci * c, c)
        mixed = jnp.dot(ws, vn_ref[rows, :], preferred_element_type=F32) + bs
        o_ref[rows, :] = (u_ref[rows, :].astype(F32) * mixed).astype(o_ref.dtype)


def _sgu(p, w_s, b_s, u_off):
    n_tok = p.shape[0]
    n_groups = w_s.shape[0]
    tl = SEQ_TILE
    gd = SGU_GROUP_DIM
    ub = u_off // gd
    blocks = 3 * _nbytes((tl, gd), BF16) + 2 * _nbytes((SGU_CHUNK, V7X_LANES), F32)
    return pl.pallas_call(
        _sgu_kernel,
        grid=(n_tok // tl, n_groups),
        in_specs=[pl.BlockSpec((tl, gd), lambda i, g: (i, ub + g)),
                  pl.BlockSpec((tl, gd), lambda i, g: (i, ub + n_groups + g)),
                  pl.BlockSpec((1, SGU_CHUNK, SGU_CHUNK), lambda i, g: (g, 0, 0)),
                  pl.BlockSpec((1, SGU_CHUNK, 1), lambda i, g: (g, 0, 0))],
        out_specs=pl.BlockSpec((tl, gd), lambda i, g: (i, g)),
        out_shape=jax.ShapeDtypeStruct((n_tok, n_groups * gd), BF16),
        compiler_params=pltpu.CompilerParams(
            dimension_semantics=("arbitrary", "arbitrary"),
            vmem_limit_bytes=_vmem_limit(blocks, 0)),
        name="sgu",
    )(p, p, w_s, b_s)


def _split_bf16(t):
    hi = t.astype(BF16)
    lo = (t - hi.astype(F32)).astype(BF16)
    return hi, lo


def _router_kernel(x_ref, g_ref, sh_ref, sc_ref, wr_ref, br_ref, h_ref, eid_ref, ew_ref):
    h = _rms_mod(x_ref[...], g_ref[...], sh_ref[0:1, :], sc_ref[0:1, :])
    half = h.shape[1] // 2
    bits = lax.bitcast_convert_type(h.astype(BF16).astype(F32), jnp.uint32)
    h_ref[...] = (bits[:, half:] & jnp.uint32(0xFFFF0000)) | (bits[:, :half] >> 16)
    h_hi, h_lo = _split_bf16(h)
    w_hi, w_lo = _split_bf16(wr_ref[...])
    logits = (jnp.dot(h_hi, w_hi, preferred_element_type=F32)
              + jnp.dot(h_lo, w_hi, preferred_element_type=F32)
              + jnp.dot(h_hi, w_lo, preferred_element_type=F32)) + br_ref[...]
    lane = lax.broadcasted_iota(jnp.int32, logits.shape, 1)
    lane_f = lane.astype(F32)
    neg = -jnp.inf

    def first_lane(hit):
        return jnp.min(jnp.where(hit, lane_f, float(V7X_LANES)), axis=-1, keepdims=True).astype(jnp.int32)

    gl = jnp.where(lane < MOE_GROUPS, logits, neg)
    g_max = jnp.max(gl, axis=-1, keepdims=True)
    g_sel = first_lane(gl == g_max)
    p_g = 1.0 / jnp.sum(jnp.exp(gl - g_max), axis=-1, keepdims=True)
    e_lo = MOE_GROUPS + g_sel * EXPERTS_PER_GROUP
    el = jnp.where((lane >= e_lo) & (lane < e_lo + EXPERTS_PER_GROUP), logits, neg)
    v1 = jnp.max(el, axis=-1, keepdims=True)
    i1 = first_lane(el == v1)
    el2 = jnp.where(lane == i1, neg, el)
    v2 = jnp.max(el2, axis=-1, keepdims=True)
    i2 = first_lane(el2 == v2)
    e2 = jnp.exp(v2 - v1)
    den = 1.0 + e2
    w1 = p_g * (1.0 / den)
    w2 = p_g * (e2 / den)
    eid_ref[...] = jnp.where(lane == 0, i1 - MOE_GROUPS, jnp.where(lane == 1, i2 - MOE_GROUPS, 0))
    ew_ref[...] = jnp.where(lane == 0, w1, jnp.where(lane == 1, w2, 0.0))


def _router(x, g, mod, shift_chunk, scale_chunk, wr, br):
    m, d = x.shape
    tm = ROW_TILE
    blocks = 2 * _nbytes((tm, d), F32) + 2 * _nbytes((MOD_ROWS, d), F32) + _nbytes((d, V7X_LANES), F32) \
        + 2 * _nbytes((tm, V7X_LANES), F32)
    return pl.pallas_call(
        _router_kernel,
        grid=(m // tm,),
        in_specs=[pl.BlockSpec((tm, d), lambda i: (i, 0)),
                  pl.BlockSpec((1, d), lambda i: (0, 0)),
                  pl.BlockSpec((MOD_ROWS, d), lambda i: (0, shift_chunk)),
                  pl.BlockSpec((MOD_ROWS, d), lambda i: (0, scale_chunk)),
                  pl.BlockSpec((d, V7X_LANES), lambda i: (0, 0)),
                  pl.BlockSpec((1, V7X_LANES), lambda i: (0, 0))],
        out_specs=[pl.BlockSpec((tm, d // 2), lambda i: (i, 0)),
                   pl.BlockSpec((tm, V7X_LANES), lambda i: (i, 0)),
                   pl.BlockSpec((tm, V7X_LANES), lambda i: (i, 0))],
        out_shape=[jax.ShapeDtypeStruct((m, d // 2), jnp.uint32),
                   jax.ShapeDtypeStruct((m, V7X_LANES), jnp.int32),
                   jax.ShapeDtypeStruct((m, V7X_LANES), F32)],
        compiler_params=pltpu.CompilerParams(
            dimension_semantics=("arbitrary",),
            vmem_limit_bytes=_vmem_limit(blocks, 0)),
        name="router",
    )(x, g, mod, mod, wr, br)


def _row_copy(src, src_row, dst, dst_row, sem):
    return pltpu.make_async_copy(src.at[pl.ds(src_row, 1), :], dst.at[pl.ds(dst_row, 1), :], sem)


def _for_rows(n_rows, fn):
    n_groups = lax.shift_right_logical(n_rows, DMA_UNROLL.bit_length() - 1)

    def group(gi, carry):
        for u in range(DMA_UNROLL):
            fn(gi * DMA_UNROLL + u, u)
        return carry

    def single(r, carry):
        fn(r, 0)
        return carry

    lax.fori_loop(0, n_groups, group, 0)
    lax.fori_loop(n_groups * DMA_UNROLL, n_rows, single, 0)


def _experts_kernel(be_ref, nact_ref, nv_ref, cs_ref, stok_ref, sdst_ref,
                    h_hbm, wg_ref, wu_ref, wd_ref, y_hbm,
                    x_ref, acc_ref, wgb_ref, wub_ref, wdb_ref, sem_in, sem_out):
    del be_ref
    b = pl.program_id(0)
    f = pl.program_id(1)
    n_active = nact_ref[0]
    active = b < n_active
    slot = lax.rem(b, 2)
    sub = EXPERT_SUB
    last_f = EXPERT_SPLIT - 1

    def gather(bb, slot_, wait):
        def one(r, u):
            cp = _row_copy(h_hbm, stok_ref[cs_ref[bb] + r], x_ref.at[slot_], r, sem_in.at[slot_])
            cp.wait() if wait else cp.start(priority=u % N_DMA_PRIORITIES)

        _for_rows(nv_ref[bb], one)

    def scatter(bb, wait):
        def one(r, u):
            par = lax.rem(bb, 2)
            cp = _row_copy(acc_ref.at[par], r, y_hbm, sdst_ref[cs_ref[bb] + r], sem_out.at[par])
            cp.wait() if wait else cp.start(priority=u % N_DMA_PRIORITIES)

        _for_rows(nv_ref[bb], one)

    @pl.when(active & (f == 0))
    def _rows_in():
        @pl.when(b == 0)
        def _():
            x_ref[...] = jnp.zeros(x_ref.shape, x_ref.dtype)
            gather(0, 0, False)

        @pl.when(b + 1 < n_active)
        def _():
            gather(b + 1, 1 - slot, False)

        gather(b, slot, True)

    @pl.when(active)
    def _compute():
        wgb_ref[...] = wg_ref[0].astype(BF16)
        wub_ref[...] = wu_ref[0].astype(BF16)
        wdb_ref[...] = wd_ref[0].astype(BF16)
        half = wgb_ref.shape[0] // 2
        for j in range(x_ref.shape[1] // sub):
            @pl.when(j * sub < nv_ref[b])
            def _sub_block():
                rows = pl.ds(j * sub, sub)
                xw = x_ref[slot, rows, :]
                xa = lax.bitcast_convert_type(xw << 16, F32).astype(BF16)
                xb = lax.bitcast_convert_type(xw & jnp.uint32(0xFFFF0000), F32).astype(BF16)
                gate = (jnp.dot(xa, wgb_ref[:half, :], preferred_element_type=F32)
                        + jnp.dot(xb, wgb_ref[half:, :], preferred_element_type=F32))
                up = (jnp.dot(xa, wub_ref[:half, :], preferred_element_type=F32)
                      + jnp.dot(xb, wub_ref[half:, :], preferred_element_type=F32))
                hid = (gate * jax.nn.sigmoid(gate) * up).astype(BF16)

                @pl.when(f == 0)
                def _():
                    acc_ref[slot, rows, :] = jnp.dot(hid, wdb_ref[...], preferred_element_type=F32)

                @pl.when(f > 0)
                def _():
                    acc_ref[slot, rows, :] += jnp.dot(hid, wdb_ref[...], preferred_element_type=F32)

    @pl.when(active & (f == last_f))
    def _rows_out():
        @pl.when(b > 0)
        def _():
            scatter(b - 1, True)

        scatter(b, False)

        @pl.when(b == n_active - 1)
        def _():
            scatter(b, True)


def _experts(h, w_gate, w_up, w_down, plan, n_out_rows):
    block_e, n_active, n_valid, c_start, s_tok, s_dst = plan
    d = w_gate.shape[1]
    de = w_gate.shape[-1]
    des = de // EXPERT_SPLIT
    n_blocks = block_e.shape[0]
    assert h.shape[1] * 2 == d and EXPERT_SPLIT >= 2 and EXPERT_ROWS % EXPERT_SUB == 0

    def w_idx(transpose):
        def index_map(b, f, be, nact, *_):
            fs = jnp.where(b < nact[0], f, EXPERT_SPLIT - 1)
            return (be[b], fs, 0) if transpose else (be[b], 0, fs)
        return index_map

    blocks = 3 * _nbytes((d, des), F32)
    scratch = (_nbytes((2, EXPERT_ROWS, d // 2), jnp.uint32) + _nbytes((2, EXPERT_ROWS, d), F32)
               + 3 * _nbytes((d, des), BF16))
    grid_spec = pltpu.PrefetchScalarGridSpec(
        num_scalar_prefetch=6,
        grid=(n_blocks, EXPERT_SPLIT),
        in_specs=[pl.BlockSpec(memory_space=pl.ANY),
                  pl.BlockSpec((1, d, des), w_idx(False)),
                  pl.BlockSpec((1, d, des), w_idx(False)),
                  pl.BlockSpec((1, des, d), w_idx(True))],
        out_specs=pl.BlockSpec(memory_space=pl.ANY),
        scratch_shapes=[pltpu.VMEM((2, EXPERT_ROWS, d // 2), jnp.uint32),
                        pltpu.VMEM((2, EXPERT_ROWS, d), F32),
                        pltpu.VMEM((d, des), BF16), pltpu.VMEM((d, des), BF16), pltpu.VMEM((des, d), BF16),
                        pltpu.SemaphoreType.DMA((2,)), pltpu.SemaphoreType.DMA((2,))],
    )
    return pl.pallas_call(
        _experts_kernel,
        grid_spec=grid_spec,
        out_shape=jax.ShapeDtypeStruct((n_out_rows, d), F32),
        compiler_params=pltpu.CompilerParams(
            dimension_semantics=("arbitrary", "arbitrary"),
            vmem_limit_bytes=_vmem_limit(blocks, scratch)),
        name="experts",
    )(block_e, n_active, n_valid, c_start, s_tok, s_dst, h, w_gate, w_up, w_down)


def _combine_kernel(*refs, top_k):
    y_refs = refs[:top_k]
    w_ref, x_ref, g_ref, fg_ref, o_ref = refs[top_k:]
    y = y_refs[0][0] * w_ref[:, 0:1]
    for k in range(1, top_k):
        y = y + y_refs[k][0] * w_ref[:, k:k + 1]
    x = x_ref[...] + g_ref[0:1, :] * y
    o_ref[...] = x * lax.rsqrt(jnp.mean(x * x, axis=-1, keepdims=True) + EPS) * fg_ref[...]


def _combine(y, ew, x, mod, gate_chunk, final_g, top_k):
    m, d = x.shape
    tm = ROW_TILE
    blocks = (top_k + 2) * _nbytes((tm, d), F32) + _nbytes((MOD_ROWS, d), F32) + _nbytes((tm, V7X_LANES), F32)
    in_specs = [pl.BlockSpec((1, tm, d), functools.partial(lambda k, i: (k, i, 0), k)) for k in range(top_k)]
    in_specs += [pl.BlockSpec((tm, V7X_LANES), lambda i: (i, 0)),
                 pl.BlockSpec((tm, d), lambda i: (i, 0)),
                 pl.BlockSpec((MOD_ROWS, d), lambda i: (0, gate_chunk)),
                 pl.BlockSpec((1, d), lambda i: (0, 0))]
    return pl.pallas_call(
        functools.partial(_combine_kernel, top_k=top_k),
        grid=(m // tm,),
        in_specs=in_specs,
        out_specs=pl.BlockSpec((tm, d), lambda i: (i, 0)),
        out_shape=jax.ShapeDtypeStruct((m, d), F32),
        compiler_params=pltpu.CompilerParams(
            dimension_semantics=("arbitrary",),
            vmem_limit_bytes=_vmem_limit(blocks, 0)),
        name="combine",
    )(*([y] * top_k), ew, x, mod, final_g)


def _rope_tables(n_tokens):
    n_rows = n_tokens // GRID_W
    n_freq = RET_DK // 4
    freqs = ROPE_BASE ** (-jnp.arange(n_freq, dtype=F32) / n_freq)
    sign = jnp.concatenate([-jnp.ones((n_freq,), F32), jnp.ones((n_freq,), F32)])

    def half_tables(n_pos):
        ang = jnp.arange(n_pos, dtype=F32)[:, None] * freqs
        return jnp.tile(jnp.cos(ang), (1, 2)), jnp.tile(jnp.sin(ang), (1, 2)) * sign

    cos_r, sin_r = half_tables(n_rows)
    cos_c, sin_c = half_tables(GRID_W)

    def expand(by_row, by_col):
        by_row = jnp.broadcast_to(by_row[:, None, :], (n_rows, GRID_W, 2 * n_freq))
        by_col = jnp.broadcast_to(by_col[None, :, :], (n_rows, GRID_W, 2 * n_freq))
        return jnp.concatenate([by_row, by_col], axis=-1).reshape(n_tokens, RET_DK)

    return expand(cos_r, cos_c), expand(sin_r, sin_c)


def _dispatch(eid, n_tok, top_k):
    m = n_tok * top_k
    n_blocks = -(-m // EXPERT_ROWS) + N_EXPERTS
    e_flat = eid[:, :top_k].reshape(-1)
    order = jnp.argsort(e_flat).astype(jnp.int32)
    s_tok = order // top_k
    s_dst = (order % top_k) * n_tok + s_tok
    counts = jnp.sum((e_flat[:, None] == jnp.arange(N_EXPERTS, dtype=jnp.int32)[None, :]).astype(jnp.int32), axis=0)
    starts = jnp.cumsum(counts) - counts
    e_blocks = (counts + EXPERT_ROWS - 1) // EXPERT_ROWS
    b_ends = jnp.cumsum(e_blocks)
    n_active = b_ends[-1].astype(jnp.int32)
    blk = jnp.arange(n_blocks, dtype=jnp.int32)
    block_e = jnp.searchsorted(b_ends, jnp.minimum(blk, n_active - 1), side='right')
    block_e = jnp.minimum(block_e, N_EXPERTS - 1).astype(jnp.int32)
    within = blk - (b_ends - e_blocks)[block_e]
    c_start = (starts[block_e] + within * EXPERT_ROWS).astype(jnp.int32)
    n_valid = jnp.where(blk < n_active, jnp.clip(counts[block_e] - within * EXPERT_ROWS, 0, EXPERT_ROWS), 0)
    n_valid = n_valid.astype(jnp.int32)
    return block_e, n_active.reshape(1), n_valid, c_start, s_tok, s_dst


def kernel(x, c, ctx, c_ctx, w_ada, b_ada, norm1_g, norm2_g, w_in, ret_decay_f, ret_decay_b, ret_gn_g, sgu_ln_g, sgu_ln_b, sgu_w_s, sgu_b_s, w_out, w_router_group, b_router_group, w_router_expert, b_router_expert, w_gate, w_up, w_down, final_g):
    batch, n_tok, d = x.shape
    assert batch == 1 and w_ada.shape[0] == 1
    n_heads = ret_decay_f.shape[-1]
    n_groups = sgu_w_s.shape[1]
    ret_qk_w = n_heads * RET_DK
    ret_w = n_heads * RET_DV
    sgu_w = n_groups * SGU_GROUP_DIM
    k_off = ret_qk_w
    u_off = 2 * ret_qk_w + 2 * ret_w
    in_w = u_off + 2 * sgu_w
    top_k = 2
    assert w_in.shape == (1, d, in_w) and w_out.shape == (1, ret_w + sgu_w, d)

    cc = jnp.zeros((MOD_ROWS, d), F32).at[0].set(c[0]).at[1].set(c_ctx)
    mod = _ada(cc, w_ada[0], b_ada[0].reshape(1, N_MOD * d))

    lg_f = -jnp.exp(ret_decay_f[0])
    lg_b = -jnp.exp(ret_decay_b[0])

    hc = _norm_mod(ctx[0], norm1_g, mod, 1, 0, 1, ROW_TILE)
    kv_c = _matmul([hc], w_in[0], ret_qk_w + ret_w, k_off, ctx.shape[1], MM_TN)
    s_f, s_b = _ctx_state(kv_c, lg_f, lg_b, n_heads)

    h1 = _norm_mod(x[0], norm1_g, mod, 0, 0, 1, ROW_TILE)
    cos, sin_signed = _rope_tables(n_tok)
    p = _in_proj(h1, w_in[0], cos, sin_signed, sgu_ln_g, sgu_ln_b, ret_qk_w, 2 * ret_w, sgu_w, IN_TM, IN_TN)
    ret_out = _retention(p, lg_f, lg_b, s_f, s_b, ret_gn_g, n_heads)
    sgu_out = _sgu(p, sgu_w_s[0], sgu_b_s[0].reshape(n_groups, SGU_CHUNK, 1), u_off)
    x1 = _matmul([ret_out, sgu_out], w_out[0], d, 0, IN_TM, IN_TN, residual=(x[0], mod, 2))

    n_router = MOE_GROUPS + N_EXPERTS
    wr = jnp.zeros((d, V7X_LANES), F32).at[:, :MOE_GROUPS].set(w_router_group[0]) \
        .at[:, MOE_GROUPS:n_router].set(w_router_expert[0])
    br = jnp.zeros((1, V7X_LANES), F32).at[0, :MOE_GROUPS].set(b_router_group[0]) \
        .at[0, MOE_GROUPS:n_router].set(b_router_expert[0])
    h2, eid, ew = _router(x1, norm2_g, mod, 3, 4, wr, br)
    plan = _dispatch(eid, n_tok, top_k)
    y = _experts(h2, w_gate[0], w_up[0], w_down[0], plan, top_k * n_tok)
    out = _combine(y.reshape(top_k, n_tok, d), ew, x1, mod, 5, final_g.reshape(1, d), top_k)
    return out.reshape(batch, n_tok, d)
```

```python
import functools

import jax
import jax.numpy as jnp
from jax import lax
from jax.experimental import pallas as pl
from jax.experimental.pallas import tpu as pltpu

F32 = jnp.float32
BF16 = jnp.bfloat16

GRID_W = 64
RET_DK = 256
RET_DV = 256
ROPE_BASE = 10000.0
SGU_GROUP_DIM = 256
SGU_CHUNK = 128
MOE_GROUPS = 8
EXPERTS_PER_GROUP = 8
N_EXPERTS = MOE_GROUPS * EXPERTS_PER_GROUP
N_MOD = 6
EPS = 1e-6

V7X_LANES = 128
V7X_VMEM_BYTES = 64 * 1024 * 1024
MOD_ROWS = 8

ADA_TN = 512
MM_TN = 512
IN_TM = 512
IN_TN = 1024
ROW_TILE = 256
SEQ_TILE = 1024
RET_TILE = 1024
RET_SUB = 256
EXPERT_ROWS = 512
EXPERT_SUB = 256
EXPERT_SPLIT = 2
DMA_UNROLL = 8
N_DMA_PRIORITIES = 2


def _vmem_limit(block_bytes, scratch_bytes):
    want = 2 * block_bytes + scratch_bytes
    return int(min(V7X_VMEM_BYTES - 4 * 1024 * 1024, max(2 * want, 32 * 1024 * 1024)))


def _nbytes(shape, dtype):
    n = 1
    for s in shape:
        n *= s
    return n * jnp.dtype(dtype).itemsize


def _ada_kernel(cc_ref, w_ref, b_ref, o_ref):
    a = cc_ref[...]
    s = (a * jax.nn.sigmoid(a)).astype(BF16)
    o_ref[...] = jnp.dot(s, w_ref[...].astype(BF16), preferred_element_type=F32) + b_ref[...]


def _ada(cc, w, b):
    d, n = w.shape
    blocks = _nbytes((MOD_ROWS, d), F32) + _nbytes((d, ADA_TN), F32) + 2 * _nbytes((MOD_ROWS, ADA_TN), F32)
    return pl.pallas_call(
        _ada_kernel,
        grid=(n // ADA_TN,),
        in_specs=[pl.BlockSpec((MOD_ROWS, d), lambda j: (0, 0)),
                  pl.BlockSpec((d, ADA_TN), lambda j: (0, j)),
                  pl.BlockSpec((1, ADA_TN), lambda j: (0, j))],
        out_specs=pl.BlockSpec((MOD_ROWS, ADA_TN), lambda j: (0, j)),
        out_shape=jax.ShapeDtypeStruct((MOD_ROWS, n), F32),
        compiler_params=pltpu.CompilerParams(
            dimension_semantics=("arbitrary",),
            vmem_limit_bytes=_vmem_limit(blocks, _nbytes((d, ADA_TN), BF16))),
        name="ada",
    )(cc, w, b)


def _rms_mod(x, g, shift, scale):
    y = x * lax.rsqrt(jnp.mean(x * x, axis=-1, keepdims=True) + EPS) * g
    return y * (1.0 + scale) + shift


def _norm_mod_kernel(x_ref, g_ref, sh_ref, sc_ref, o_ref, *, row):
    h = _rms_mod(x_ref[...], g_ref[...], sh_ref[row:row + 1, :], sc_ref[row:row + 1, :])
    o_ref[...] = h.astype(o_ref.dtype)


def _norm_mod(x, g, mod, row, shift_chunk, scale_chunk, tm):
    m, d = x.shape
    blocks = _nbytes((tm, d), F32) * 2 + 3 * _nbytes((MOD_ROWS, d), F32)
    return pl.pallas_call(
        functools.partial(_norm_mod_kernel, row=row),
        grid=(m // tm,),
        in_specs=[pl.BlockSpec((tm, d), lambda i: (i, 0)),
                  pl.BlockSpec((1, d), lambda i: (0, 0)),
                  pl.BlockSpec((MOD_ROWS, d), lambda i: (0, shift_chunk)),
                  pl.BlockSpec((MOD_ROWS, d), lambda i: (0, scale_chunk))],
        out_specs=pl.BlockSpec((tm, d), lambda i: (i, 0)),
        out_shape=jax.ShapeDtypeStruct((m, d), BF16),
        compiler_params=pltpu.CompilerParams(
            dimension_semantics=("arbitrary",),
            vmem_limit_bytes=_vmem_limit(blocks, 0)),
        name="norm_mod",
    )(x, g, mod, mod)


def _matmul_acc(a_refs, w_ref, wbf_ref):
    @pl.when(pl.program_id(1) == 0)
    def _():
        wbf_ref[...] = w_ref[...].astype(BF16)

    acc = None
    k0 = 0
    for a_ref in a_refs:
        kk = a_ref.shape[1]
        part = jnp.dot(a_ref[...], wbf_ref[k0:k0 + kk, :], preferred_element_type=F32)
        acc = part if acc is None else acc + part
        k0 += kk
    return acc


def _matmul_kernel(*refs, n_a):
    w_ref, o_ref, wbf_ref = refs[n_a:]
    o_ref[...] = _matmul_acc(refs[:n_a], w_ref, wbf_ref).astype(o_ref.dtype)


def _matmul_res_kernel(*refs, n_a):
    w_ref, x_ref, g_ref, o_ref, wbf_ref = refs[n_a:]
    o_ref[...] = x_ref[...] + g_ref[0:1, :] * _matmul_acc(refs[:n_a], w_ref, wbf_ref)


def _matmul(a_list, w, n_cols, col_off, tm, tn, residual=None, out_dtype=F32):
    m = a_list[0].shape[0]
    k = sum(a.shape[1] for a in a_list)
    n_a = len(a_list)
    joff = col_off // tn
    grid = (n_cols // tn, m // tm)
    in_specs = [pl.BlockSpec((tm, a.shape[1]), lambda j, i: (i, 0)) for a in a_list]
    in_specs.append(pl.BlockSpec((k, tn), lambda j, i: (0, j + joff)))
    args = list(a_list) + [w]
    blocks = _nbytes((tm, k), BF16) + _nbytes((k, tn), F32) + _nbytes((tm, tn), F32)
    if residual is None:
        body = functools.partial(_matmul_kernel, n_a=n_a)
    else:
        x, mod, gate_chunk = residual
        goff = gate_chunk * (n_cols // tn)
        in_specs += [pl.BlockSpec((tm, tn), lambda j, i: (i, j)),
                     pl.BlockSpec((MOD_ROWS, tn), lambda j, i: (0, goff + j))]
        args += [x, mod]
        blocks += _nbytes((tm, tn), F32) + _nbytes((MOD_ROWS, tn), F32)
        body = functools.partial(_matmul_res_kernel, n_a=n_a)
    return pl.pallas_call(
        body,
        grid=grid,
        in_specs=in_specs,
        out_specs=pl.BlockSpec((tm, tn), lambda j, i: (i, j)),
        out_shape=jax.ShapeDtypeStruct((m, n_cols), out_dtype),
        scratch_shapes=[pltpu.VMEM((k, tn), BF16)],
        compiler_params=pltpu.CompilerParams(
            dimension_semantics=("arbitrary", "arbitrary"),
            vmem_limit_bytes=_vmem_limit(blocks, _nbytes((k, tn), BF16))),
        name="matmul_res" if residual is not None else "matmul",
    )(*args)


def _rope(t, cos, sin_signed):
    half = RET_DK // 2
    rot = jnp.concatenate([pltpu.roll(t[:, :half], half // 2, 1),
                           pltpu.roll(t[:, half:], half // 2, 1)], axis=1)
    return t * cos + rot * sin_signed


def _in_proj_kernel(a_ref, w_ref, cos_ref, sin_ref, lng_ref, lnb_ref, o_ref, wbf_ref, *, tile_ends):
    j = pl.program_id(0)
    q_end, k_end, plain_end, u_end = tile_ends

    def acc():
        return _matmul_acc([a_ref], w_ref, wbf_ref)

    @pl.when(j < k_end)
    def _rotated():
        t = acc() * jnp.where(j < q_end, 1.0, RET_DK ** -0.5)
        cos = cos_ref[...]
        sin = sin_ref[...]
        for hh in range(t.shape[1] // RET_DK):
            cols = slice(hh * RET_DK, (hh + 1) * RET_DK)
            o_ref[:, cols] = _rope(t[:, cols], cos, sin).astype(o_ref.dtype)

    @pl.when((j >= k_end) & (j < plain_end))
    def _plain():
        o_ref[...] = acc().astype(o_ref.dtype)

    @pl.when((j >= plain_end) & (j < u_end))
    def _gelu():
        o_ref[...] = jax.nn.gelu(acc()).astype(o_ref.dtype)

    @pl.when(j >= u_end)
    def _gelu_norm():
        t = jax.nn.gelu(acc())
        for gg in range(t.shape[1] // SGU_GROUP_DIM):
            cols = slice(gg * SGU_GROUP_DIM, (gg + 1) * SGU_GROUP_DIM)
            v32 = t[:, cols]
            mu = jnp.mean(v32, axis=-1, keepdims=True)
            var = jnp.mean(jnp.square(v32 - mu), axis=-1, keepdims=True)
            vn = (v32 - mu) * lax.rsqrt(var + EPS) * lng_ref[:, cols] + lnb_ref[:, cols]
            o_ref[:, cols] = vn.astype(o_ref.dtype)


def _in_proj(a, w, cos, sin_signed, ln_g, ln_b, qk_w, v_gate_w, sgu_w, tm, tn):
    m, k = a.shape
    n = w.shape[1]
    assert n == 2 * qk_w + v_gate_w + 2 * sgu_w
    assert qk_w % tn == 0 and v_gate_w % tn == 0 and sgu_w % tn == 0
    assert tn % RET_DK == 0 and tn % SGU_GROUP_DIM == 0
    q_end = qk_w // tn
    k_end = 2 * q_end
    plain_end = k_end + v_gate_w // tn
    u_end = plain_end + sgu_w // tn

    def table_idx(j, i):
        return (jnp.where(j < k_end, i, 0), 0)

    def ln_idx(j, i):
        return (0, jnp.maximum(j - u_end, 0))

    blocks = (_nbytes((tm, k), BF16) + _nbytes((k, tn), F32) + _nbytes((tm, tn), BF16)
              + 2 * _nbytes((tm, RET_DK), F32))
    return pl.pallas_call(
        functools.partial(_in_proj_kernel, tile_ends=(q_end, k_end, plain_end, u_end)),
        grid=(n // tn, m // tm),
        in_specs=[pl.BlockSpec((tm, k), lambda j, i: (i, 0)),
                  pl.BlockSpec((k, tn), lambda j, i: (0, j)),
                  pl.BlockSpec((tm, RET_DK), table_idx),
                  pl.BlockSpec((tm, RET_DK), table_idx),
                  pl.BlockSpec((1, tn), ln_idx),
                  pl.BlockSpec((1, tn), ln_idx)],
        out_specs=pl.BlockSpec((tm, tn), lambda j, i: (i, j)),
        out_shape=jax.ShapeDtypeStruct((m, n), BF16),
        scratch_shapes=[pltpu.VMEM((k, tn), BF16)],
        compiler_params=pltpu.CompilerParams(
            dimension_semantics=("arbitrary", "arbitrary"),
            vmem_limit_bytes=_vmem_limit(blocks, _nbytes((k, tn), BF16))),
        name="in_proj",
    )(a, w, cos, sin_signed, ln_g, ln_b)


def _ctx_state_kernel(lgf_ref, lgb_ref, k_ref, v_ref, sf_ref, sb_ref):
    h = pl.program_id(0)
    n = k_ref.shape[0]
    pos = lax.broadcasted_iota(jnp.int32, (n, 1), 0).astype(F32)
    k = k_ref[...] * (RET_DK ** -0.5)
    v = v_ref[...].astype(BF16)
    wf = jnp.exp((n - 1.0 - pos) * lgf_ref[h])
    wb = jnp.exp(pos * lgb_ref[h])
    tn_dims = (((0,), (0,)), ((), ()))
    sf_ref[0] = lax.dot_general((k * wf).astype(BF16), v, tn_dims, preferred_element_type=F32)
    sb_ref[0] = lax.dot_general((k * wb).astype(BF16), v, tn_dims, preferred_element_type=F32)


def _ctx_state(kv, lg_f, lg_b, n_heads):
    n = kv.shape[0]
    smem = pl.BlockSpec(memory_space=pltpu.SMEM)
    st = jax.ShapeDtypeStruct((n_heads, RET_DK, RET_DV), F32)
    return pl.pallas_call(
        _ctx_state_kernel,
        grid=(n_heads,),
        in_specs=[smem, smem,
                  pl.BlockSpec((n, RET_DK), lambda h: (0, h)),
                  pl.BlockSpec((n, RET_DV), lambda h: (0, n_heads + h))],
        out_specs=[pl.BlockSpec((1, RET_DK, RET_DV), lambda h: (h, 0, 0)),
                   pl.BlockSpec((1, RET_DK, RET_DV), lambda h: (h, 0, 0))],
        out_shape=[st, st],
        compiler_params=pltpu.CompilerParams(dimension_semantics=("arbitrary",)),
        name="ctx_state",
    )(lg_f, lg_b, kv, kv)


def _retention_kernel(lgf_ref, lgb_ref, q_ref, k_ref, v_ref, g_ref,
                      s0f_ref, s0b_ref, gn_ref, o_ref, state_ref, yb_ref, *, n_steps):
    h = pl.program_id(0)
    p = pl.program_id(1)
    s = pl.program_id(2)
    c = RET_SUB
    n_sub = q_ref.shape[0] // c
    ii = lax.broadcasted_iota(jnp.int32, (c, c), 0)
    jj = lax.broadcasted_iota(jnp.int32, (c, c), 1)
    idx = lax.broadcasted_iota(jnp.int32, (c, 1), 0).astype(F32)
    nt_dims = (((1,), (1,)), ((), ()))
    tn_dims = (((0,), (0,)), ((), ()))

    def chunk(ci, decay, q_decay, k_decay, chunk_decay):
        rows = pl.ds(ci * c, c)
        qb = q_ref[rows, :]
        kb = k_ref[rows, :]
        v = v_ref[rows, :]
        st = state_ref[...]
        scores = lax.dot_general(qb, kb, nt_dims, preferred_element_type=F32) * decay
        out = (jnp.dot(scores.astype(BF16), v, preferred_element_type=F32)
               + jnp.dot(qb, st.astype(BF16), preferred_element_type=F32) * q_decay)
        state_ref[...] = st * chunk_decay + lax.dot_general(
            (kb.astype(F32) * k_decay).astype(BF16), v, tn_dims, preferred_element_type=F32)
        return out

    @pl.when(p == 0)
    def _backward():
        lg = lgb_ref[h]
        blk = n_steps - 1 - s

        @pl.when(s == 0)
        def _():
            state_ref[...] = s0b_ref[0]

        mask = jj > ii
        decay = jnp.where(mask, jnp.exp(jnp.where(mask, jj - ii, 0).astype(F32) * lg), 0.0)
        q_decay = jnp.exp((c - idx) * lg)
        k_decay = jnp.exp(idx * lg)
        chunk_decay = jnp.exp(jnp.full((1, RET_DV), c, F32) * lg)
        for ci in reversed(range(n_sub)):
            out = chunk(ci, decay, q_decay, k_decay, chunk_decay)
            start = pl.multiple_of(blk * (n_sub * c) + ci * c, c)
            yb_ref[pl.ds(start, c), :] = out

    @pl.when(p == 1)
    def _forward():
        lg = lgf_ref[h]

        @pl.when(s == 0)
        def _():
            state_ref[...] = s0f_ref[0]

        mask = ii >= jj
        decay = jnp.where(mask, jnp.exp(jnp.where(mask, ii - jj, 0).astype(F32) * lg), 0.0)
        q_decay = jnp.exp((idx + 1.0) * lg)
        k_decay = jnp.exp((c - 1.0 - idx) * lg)
        chunk_decay = jnp.exp(jnp.full((1, RET_DV), c, F32) * lg)
        for ci in range(n_sub):
            out = chunk(ci, decay, q_decay, k_decay, chunk_decay)
            start = pl.multiple_of(s * (n_sub * c) + ci * c, c)
            y = out + yb_ref[pl.ds(start, c), :]
            mu = jnp.mean(y, axis=-1, keepdims=True)
            var = jnp.mean(jnp.square(y - mu), axis=-1, keepdims=True)
            yn = (y - mu) * lax.rsqrt(var + EPS) * gn_ref[...]
            gate = g_ref[pl.ds(ci * c, c), :].astype(F32)
            o_ref[pl.ds(ci * c, c), :] = (gate * jax.nn.sigmoid(gate) * yn).astype(o_ref.dtype)


def _retention(p, lg_f, lg_b, s0_f, s0_b, gn_g, n_heads):
    n_tok = p.shape[0]
    tl = RET_TILE
    n_steps = n_tok // tl
    smem = pl.BlockSpec(memory_space=pltpu.SMEM)

    def seq_blk(p_, s_):
        return jnp.where(p_ == 0, n_steps - 1 - s_, s_)

    blocks = 5 * _nbytes((tl, RET_DK), BF16) + 2 * _nbytes((RET_DK, RET_DV), F32)
    scratch = _nbytes((RET_DK, RET_DV), F32) + _nbytes((n_tok, RET_DV), F32)
    return pl.pallas_call(
        functools.partial(_retention_kernel, n_steps=n_steps),
        grid=(n_heads, 2, n_steps),
        in_specs=[smem, smem,
                  pl.BlockSpec((tl, RET_DK), lambda h, p_, s_: (seq_blk(p_, s_), h)),
                  pl.BlockSpec((tl, RET_DK), lambda h, p_, s_: (seq_blk(p_, s_), n_heads + h)),
                  pl.BlockSpec((tl, RET_DV), lambda h, p_, s_: (seq_blk(p_, s_), 2 * n_heads + h)),
                  pl.BlockSpec((tl, RET_DV), lambda h, p_, s_: (s_ * p_, 3 * n_heads + h)),
                  pl.BlockSpec((1, RET_DK, RET_DV), lambda h, p_, s_: (h, 0, 0)),
                  pl.BlockSpec((1, RET_DK, RET_DV), lambda h, p_, s_: (h, 0, 0)),
                  pl.BlockSpec((1, RET_DV), lambda h, p_, s_: (0, h))],
        out_specs=pl.BlockSpec((tl, RET_DV), lambda h, p_, s_: (s_ * p_, h)),
        out_shape=jax.ShapeDtypeStruct((n_tok, n_heads * RET_DV), BF16),
        scratch_shapes=[pltpu.VMEM((RET_DK, RET_DV), F32), pltpu.VMEM((n_tok, RET_DV), F32)],
        compiler_params=pltpu.CompilerParams(
            dimension_semantics=("arbitrary", "arbitrary", "arbitrary"),
            vmem_limit_bytes=_vmem_limit(blocks, scratch)),
        name="retention",
    )(lg_f, lg_b, p, p, p, p, s0_f, s0_b, gn_g)


def _sgu_kernel(u_ref, vn_ref, ws_ref, bs_ref, o_ref):
    c = SGU_CHUNK
    ws = ws_ref[0].astype(BF16)
    bs = bs_ref[0]
    for ci in range(u_ref.shape[0] // c):
        rows = pl.ds(ci * c, c)
        mixed = jnp.dot(ws, vn_ref[rows, :], preferred_element_type=F32) + bs
        o_ref[rows, :] = (u_ref[rows, :].astype(F32) * mixed).astype(o_ref.dtype)


def _sgu(p, w_s, b_s, u_off):
    n_tok = p.shape[0]
    n_groups = w_s.shape[0]
    tl = SEQ_TILE
    gd = SGU_GROUP_DIM
    ub = u_off // gd
    blocks = 3 * _nbytes((tl, gd), BF16) + 2 * _nbytes((SGU_CHUNK, V7X_LANES), F32)
    return pl.pallas_call(
        _sgu_kernel,
        grid=(n_tok // tl, n_groups),
        in_specs=[pl.BlockSpec((tl, gd), lambda i, g: (i, ub + g)),
                  pl.BlockSpec((tl, gd), lambda i, g: (i, ub + n_groups + g)),
                  pl.BlockSpec((1, SGU_CHUNK, SGU_CHUNK), lambda i, g: (g, 0, 0)),
                  pl.BlockSpec((1, SGU_CHUNK, 1), lambda i, g: (g, 0, 0))],
        out_specs=pl.BlockSpec((tl, gd), lambda i, g: (i, g)),
        out_shape=jax.ShapeDtypeStruct((n_tok, n_groups * gd), BF16),
        compiler_params=pltpu.CompilerParams(
            dimension_semantics=("arbitrary", "arbitrary"),
            vmem_limit_bytes=_vmem_limit(blocks, 0)),
        name="sgu",
    )(p, p, w_s, b_s)


def _split_bf16(t):
    hi = t.astype(BF16)
    lo = (t - hi.astype(F32)).astype(BF16)
    return hi, lo


def _router_kernel(x_ref, g_ref, sh_ref, sc_ref, wr_ref, br_ref, h_ref, eid_ref, ew_ref):
    h = _rms_mod(x_ref[...], g_ref[...], sh_ref[0:1, :], sc_ref[0:1, :])
    half = h.shape[1] // 2
    bits = lax.bitcast_convert_type(h.astype(BF16).astype(F32), jnp.uint32)
    h_ref[...] = (bits[:, half:] & jnp.uint32(0xFFFF0000)) | (bits[:, :half] >> 16)
    h_hi, h_lo = _split_bf16(h)
    w_hi, w_lo = _split_bf16(wr_ref[...])
    logits = (jnp.dot(h_hi, w_hi, preferred_element_type=F32)
              + jnp.dot(h_lo, w_hi, preferred_element_type=F32)
              + jnp.dot(h_hi, w_lo, preferred_element_type=F32)) + br_ref[...]
    lane = lax.broadcasted_iota(jnp.int32, logits.shape, 1)
    lane_f = lane.astype(F32)
    neg = -jnp.inf

    def first_lane(hit):
        return jnp.min(jnp.where(hit, lane_f, float(V7X_LANES)), axis=-1, keepdims=True).astype(jnp.int32)

    gl = jnp.where(lane < MOE_GROUPS, logits, neg)
    g_max = jnp.max(gl, axis=-1, keepdims=True)
    g_sel = first_lane(gl == g_max)
    p_g = 1.0 / jnp.sum(jnp.exp(gl - g_max), axis=-1, keepdims=True)
    e_lo = MOE_GROUPS + g_sel * EXPERTS_PER_GROUP
    el = jnp.where((lane >= e_lo) & (lane < e_lo + EXPERTS_PER_GROUP), logits, neg)
    v1 = jnp.max(el, axis=-1, keepdims=True)
    i1 = first_lane(el == v1)
    el2 = jnp.where(lane == i1, neg, el)
    v2 = jnp.max(el2, axis=-1, keepdims=True)
    i2 = first_lane(el2 == v2)
    e2 = jnp.exp(v2 - v1)
    den = 1.0 + e2
    w1 = p_g * (1.0 / den)
    w2 = p_g * (e2 / den)
    eid_ref[...] = jnp.where(lane == 0, i1 - MOE_GROUPS, jnp.where(lane == 1, i2 - MOE_GROUPS, 0))
    ew_ref[...] = jnp.where(lane == 0, w1, jnp.where(lane == 1, w2, 0.0))


def _router(x, g, mod, shift_chunk, scale_chunk, wr, br):
    m, d = x.shape
    tm = ROW_TILE
    blocks = 2 * _nbytes((tm, d), F32) + 2 * _nbytes((MOD_ROWS, d), F32) + _nbytes((d, V7X_LANES), F32) \
        + 2 * _nbytes((tm, V7X_LANES), F32)
    return pl.pallas_call(
        _router_kernel,
        grid=(m // tm,),
        in_specs=[pl.BlockSpec((tm, d), lambda i: (i, 0)),
                  pl.BlockSpec((1, d), lambda i: (0, 0)),
                  pl.BlockSpec((MOD_ROWS, d), lambda i: (0, shift_chunk)),
                  pl.BlockSpec((MOD_ROWS, d), lambda i: (0, scale_chunk)),
                  pl.BlockSpec((d, V7X_LANES), lambda i: (0, 0)),
                  pl.BlockSpec((1, V7X_LANES), lambda i: (0, 0))],
        out_specs=[pl.BlockSpec((tm, d // 2), lambda i: (i, 0)),
                   pl.BlockSpec((tm, V7X_LANES), lambda i: (i, 0)),
                   pl.BlockSpec((tm, V7X_LANES), lambda i: (i, 0))],
        out_shape=[jax.ShapeDtypeStruct((m, d // 2), jnp.uint32),
                   jax.ShapeDtypeStruct((m, V7X_LANES), jnp.int32),
                   jax.ShapeDtypeStruct((m, V7X_LANES), F32)],
        compiler_params=pltpu.CompilerParams(
            dimension_semantics=("arbitrary",),
            vmem_limit_bytes=_vmem_limit(blocks, 0)),
        name="router",
    )(x, g, mod, mod, wr, br)


def _row_copy(src, src_row, dst, dst_row, sem):
    return pltpu.make_async_copy(src.at[pl.ds(src_row, 1), :], dst.at[pl.ds(dst_row, 1), :], sem)


def _for_rows(n_rows, fn):
    n_groups = lax.shift_right_logical(n_rows, DMA_UNROLL.bit_length() - 1)

    def group(gi, carry):
        for u in range(DMA_UNROLL):
            fn(gi * DMA_UNROLL + u, u)
        return carry

    def single(r, carry):
        fn(r, 0)
        return carry

    lax.fori_loop(0, n_groups, group, 0)
    lax.fori_loop(n_groups * DMA_UNROLL, n_rows, single, 0)


def _experts_kernel(be_ref, nact_ref, nv_ref, cs_ref, stok_ref, sdst_ref,
                    h_hbm, wg_ref, wu_ref, wd_ref, y_hbm,
                    x_ref, acc_ref, wgb_ref, wub_ref, wdb_ref, sem_in, sem_out):
    del be_ref
    b = pl.program_id(0)
    f = pl.program_id(1)
    n_active = nact_ref[0]
    active = b < n_active
    slot = lax.rem(b, 2)
    sub = EXPERT_SUB
    last_f = EXPERT_SPLIT - 1

    def gather(bb, slot_, wait):
        def one(r, u):
            cp = _row_copy(h_hbm, stok_ref[cs_ref[bb] + r], x_ref.at[slot_], r, sem_in.at[slot_])
            cp.wait() if wait else cp.start(priority=u % N_DMA_PRIORITIES)

        _for_rows(nv_ref[bb], one)

    def scatter(bb, wait):
        def one(r, u):
            par = lax.rem(bb, 2)
            cp = _row_copy(acc_ref.at[par], r, y_hbm, sdst_ref[cs_ref[bb] + r], sem_out.at[par])
            cp.wait() if wait else cp.start(priority=u % N_DMA_PRIORITIES)

        _for_rows(nv_ref[bb], one)

    @pl.when(active & (f == 0))
    def _rows_in():
        @pl.when(b == 0)
        def _():
            x_ref[...] = jnp.zeros(x_ref.shape, x_ref.dtype)
            gather(0, 0, False)

        @pl.when(b + 1 < n_active)
        def _():
            gather(b + 1, 1 - slot, False)

        gather(b, slot, True)

    @pl.when(active)
    def _compute():
        half = wgb_ref.shape[0] // 2

        def sub_block(j, wg, wu):
            rows = pl.ds(j * sub, sub)
            xw = x_ref[slot, rows, :]
            xa = lax.bitcast_convert_type(xw << 16, F32).astype(BF16)
            xb = lax.bitcast_convert_type(xw & jnp.uint32(0xFFFF0000), F32).astype(BF16)
            gate = (jnp.dot(xa, wg[:half, :], preferred_element_type=F32)
                    + jnp.dot(xb, wg[half:, :], preferred_element_type=F32))
            up = (jnp.dot(xa, wu[:half, :], preferred_element_type=F32)
                  + jnp.dot(xb, wu[half:, :], preferred_element_type=F32))
            hid = (gate * jax.nn.sigmoid(gate) * up).astype(BF16)

            @pl.when(f == 0)
            def _():
                acc_ref[slot, rows, :] = jnp.dot(hid, wdb_ref[...], preferred_element_type=F32)

            @pl.when(f > 0)
            def _():
                acc_ref[slot, rows, :] += jnp.dot(hid, wdb_ref[...], preferred_element_type=F32)

        wg = wg_ref[0].astype(BF16)
        wu = wu_ref[0].astype(BF16)
        wgb_ref[...] = wg
        wub_ref[...] = wu
        wdb_ref[...] = wd_ref[0].astype(BF16)
        sub_block(0, wg, wu)
        for j in range(1, x_ref.shape[1] // sub):
            @pl.when(j * sub < nv_ref[b])
            def _():
                sub_block(j, wgb_ref[...], wub_ref[...])

    @pl.when(active & (f == last_f))
    def _rows_out():
        @pl.when(b > 0)
        def _():
            scatter(b - 1, True)

        scatter(b, False)

        @pl.when(b == n_active - 1)
        def _():
            scatter(b, True)


def _experts(h, w_gate, w_up, w_down, plan, n_out_rows):
    block_e, n_active, n_valid, c_start, s_tok, s_dst = plan
    d = w_gate.shape[1]
    de = w_gate.shape[-1]
    des = de // EXPERT_SPLIT
    n_blocks = block_e.shape[0]
    assert h.shape[1] * 2 == d and EXPERT_SPLIT >= 2 and EXPERT_ROWS % EXPERT_SUB == 0

    def w_idx(transpose):
        def index_map(b, f, be, nact, *_):
            fs = jnp.where(b < nact[0], f, EXPERT_SPLIT - 1)
            return (be[b], fs, 0) if transpose else (be[b], 0, fs)
        return index_map

    blocks = 3 * _nbytes((d, des), F32)
    scratch = (_nbytes((2, EXPERT_ROWS, d // 2), jnp.uint32) + _nbytes((2, EXPERT_ROWS, d), F32)
               + 3 * _nbytes((d, des), BF16))
    grid_spec = pltpu.PrefetchScalarGridSpec(
        num_scalar_prefetch=6,
        grid=(n_blocks, EXPERT_SPLIT),
        in_specs=[pl.BlockSpec(memory_space=pl.ANY),
                  pl.BlockSpec((1, d, des), w_idx(False)),
                  pl.BlockSpec((1, d, des), w_idx(False)),
                  pl.BlockSpec((1, des, d), w_idx(True))],
        out_specs=pl.BlockSpec(memory_space=pl.ANY),
        scratch_shapes=[pltpu.VMEM((2, EXPERT_ROWS, d // 2), jnp.uint32),
                        pltpu.VMEM((2, EXPERT_ROWS, d), F32),
                        pltpu.VMEM((d, des), BF16), pltpu.VMEM((d, des), BF16), pltpu.VMEM((des, d), BF16),
                        pltpu.SemaphoreType.DMA((2,)), pltpu.SemaphoreType.DMA((2,))],
    )
    return pl.pallas_call(
        _experts_kernel,
        grid_spec=grid_spec,
        out_shape=jax.ShapeDtypeStruct((n_out_rows, d), F32),
        compiler_params=pltpu.CompilerParams(
            dimension_semantics=("arbitrary", "arbitrary"),
            vmem_limit_bytes=_vmem_limit(blocks, scratch)),
        name="experts",
    )(block_e, n_active, n_valid, c_start, s_tok, s_dst, h, w_gate, w_up, w_down)


def _combine_kernel(*refs, top_k):
    y_refs = refs[:top_k]
    w_ref, x_ref, g_ref, fg_ref, o_ref = refs[top_k:]
    y = y_refs[0][0] * w_ref[:, 0:1]
    for k in range(1, top_k):
        y = y + y_refs[k][0] * w_ref[:, k:k + 1]
    x = x_ref[...] + g_ref[0:1, :] * y
    o_ref[...] = x * lax.rsqrt(jnp.mean(x * x, axis=-1, keepdims=True) + EPS) * fg_ref[...]


def _combine(y, ew, x, mod, gate_chunk, final_g, top_k):
    m, d = x.shape
    tm = ROW_TILE
    blocks = (top_k + 2) * _nbytes((tm, d), F32) + _nbytes((MOD_ROWS, d), F32) + _nbytes((tm, V7X_LANES), F32)
    in_specs = [pl.BlockSpec((1, tm, d), functools.partial(lambda k, i: (k, i, 0), k)) for k in range(top_k)]
    in_specs += [pl.BlockSpec((tm, V7X_LANES), lambda i: (i, 0)),
                 pl.BlockSpec((tm, d), lambda i: (i, 0)),
                 pl.BlockSpec((MOD_ROWS, d), lambda i: (0, gate_chunk)),
                 pl.BlockSpec((1, d), lambda i: (0, 0))]
    return pl.pallas_call(
        functools.partial(_combine_kernel, top_k=top_k),
        grid=(m // tm,),
        in_specs=in_specs,
        out_specs=pl.BlockSpec((tm, d), lambda i: (i, 0)),
        out_shape=jax.ShapeDtypeStruct((m, d), F32),
        compiler_params=pltpu.CompilerParams(
            dimension_semantics=("arbitrary",),
            vmem_limit_bytes=_vmem_limit(blocks, 0)),
        name="combine",
    )(*([y] * top_k), ew, x, mod, final_g)


def _rope_tables(n_tokens):
    n_rows = n_tokens // GRID_W
    n_freq = RET_DK // 4
    freqs = ROPE_BASE ** (-jnp.arange(n_freq, dtype=F32) / n_freq)
    sign = jnp.concatenate([-jnp.ones((n_freq,), F32), jnp.ones((n_freq,), F32)])

    def half_tables(n_pos):
        ang = jnp.arange(n_pos, dtype=F32)[:, None] * freqs
        return jnp.tile(jnp.cos(ang), (1, 2)), jnp.tile(jnp.sin(ang), (1, 2)) * sign

    cos_r, sin_r = half_tables(n_rows)
    cos_c, sin_c = half_tables(GRID_W)

    def expand(by_row, by_col):
        by_row = jnp.broadcast_to(by_row[:, None, :], (n_rows, GRID_W, 2 * n_freq))
        by_col = jnp.broadcast_to(by_col[None, :, :], (n_rows, GRID_W, 2 * n_freq))
        return jnp.concatenate([by_row, by_col], axis=-1).reshape(n_tokens, RET_DK)

    return expand(cos_r, cos_c), expand(sin_r, sin_c)


def _dispatch(eid, n_tok, top_k):
    m = n_tok * top_k
    n_blocks = -(-m // EXPERT_ROWS) + N_EXPERTS
    e_flat = eid[:, :top_k].reshape(-1)
    order = jnp.argsort(e_flat).astype(jnp.int32)
    s_tok = order // top_k
    s_dst = (order % top_k) * n_tok + s_tok
    counts = jnp.sum((e_flat[:, None] == jnp.arange(N_EXPERTS, dtype=jnp.int32)[None, :]).astype(jnp.int32), axis=0)
    starts = jnp.cumsum(counts) - counts
    e_blocks = (counts + EXPERT_ROWS - 1) // EXPERT_ROWS
    b_ends = jnp.cumsum(e_blocks)
    n_active = b_ends[-1].astype(jnp.int32)
    blk = jnp.arange(n_blocks, dtype=jnp.int32)
    block_e = jnp.sum((jnp.minimum(blk, n_active - 1)[:, None] >= b_ends[None, :]).astype(jnp.int32), axis=1)
    block_e = jnp.minimum(block_e, N_EXPERTS - 1)
    own = (block_e[:, None] == jnp.arange(N_EXPERTS, dtype=jnp.int32)[None, :]).astype(jnp.int32)

    def of_block(per_expert):
        return jnp.sum(own * per_expert[None, :], axis=1)

    within = blk - of_block(b_ends - e_blocks)
    c_start = of_block(starts) + within * EXPERT_ROWS
    n_valid = jnp.where(blk < n_active, jnp.clip(of_block(counts) - within * EXPERT_ROWS, 0, EXPERT_ROWS), 0)
    return block_e, n_active.reshape(1), n_valid, c_start, s_tok, s_dst


def kernel(x, c, ctx, c_ctx, w_ada, b_ada, norm1_g, norm2_g, w_in, ret_decay_f, ret_decay_b, ret_gn_g, sgu_ln_g, sgu_ln_b, sgu_w_s, sgu_b_s, w_out, w_router_group, b_router_group, w_router_expert, b_router_expert, w_gate, w_up, w_down, final_g):
    batch, n_tok, d = x.shape
    assert batch == 1 and w_ada.shape[0] == 1
    n_heads = ret_decay_f.shape[-1]
    n_groups = sgu_w_s.shape[1]
    ret_qk_w = n_heads * RET_DK
    ret_w = n_heads * RET_DV
    sgu_w = n_groups * SGU_GROUP_DIM
    k_off = ret_qk_w
    u_off = 2 * ret_qk_w + 2 * ret_w
    in_w = u_off + 2 * sgu_w
    top_k = 2
    assert w_in.shape == (1, d, in_w) and w_out.shape == (1, ret_w + sgu_w, d)

    cc = jnp.concatenate([c, c_ctx[None, :], jnp.zeros((MOD_ROWS - 2, d), F32)], axis=0)
    mod = _ada(cc, w_ada[0], b_ada[0].reshape(1, N_MOD * d))

    lg_f = -jnp.exp(ret_decay_f[0])
    lg_b = -jnp.exp(ret_decay_b[0])

    hc = _norm_mod(ctx[0], norm1_g, mod, 1, 0, 1, ROW_TILE)
    kv_c = _matmul([hc], w_in[0], ret_qk_w + ret_w, k_off, ctx.shape[1], MM_TN)
    s_f, s_b = _ctx_state(kv_c, lg_f, lg_b, n_heads)

    h1 = _norm_mod(x[0], norm1_g, mod, 0, 0, 1, ROW_TILE)
    cos, sin_signed = _rope_tables(n_tok)
    p = _in_proj(h1, w_in[0], cos, sin_signed, sgu_ln_g, sgu_ln_b, ret_qk_w, 2 * ret_w, sgu_w, IN_TM, IN_TN)
    ret_out = _retention(p, lg_f, lg_b, s_f, s_b, ret_gn_g, n_heads)
    sgu_out = _sgu(p, sgu_w_s[0], sgu_b_s[0].reshape(n_groups, SGU_CHUNK, 1), u_off)
    x1 = _matmul([ret_out, sgu_out], w_out[0], d, 0, IN_TM, IN_TN, residual=(x[0], mod, 2))

    n_router = MOE_GROUPS + N_EXPERTS
    wr = jnp.concatenate([w_router_group[0], w_router_expert[0],
                          jnp.zeros((d, V7X_LANES - n_router), F32)], axis=1)
    br = jnp.concatenate([b_router_group, b_router_expert,
                          jnp.zeros((1, V7X_LANES - n_router), F32)], axis=1)
    h2, eid, ew = _router(x1, norm2_g, mod, 3, 4, wr, br)
    plan = _dispatch(eid, n_tok, top_k)
    y = _experts(h2, w_gate[0], w_up[0], w_down[0], plan, top_k * n_tok)
    out = _combine(y.reshape(top_k, n_tok, d), ew, x1, mod, 5, final_g.reshape(1, d), top_k)
    return out.reshape(batch, n_tok, d)
```

```python
import functools

import jax
import jax.numpy as jnp
from jax import lax
from jax.experimental import pallas as pl
from jax.experimental.pallas import tpu as pltpu

F32 = jnp.float32
BF16 = jnp.bfloat16

GRID_W = 64
RET_DK = 256
RET_DV = 256
ROPE_BASE = 10000.0
SGU_GROUP_DIM = 256
SGU_CHUNK = 128
MOE_GROUPS = 8
EXPERTS_PER_GROUP = 8
N_EXPERTS = MOE_GROUPS * EXPERTS_PER_GROUP
N_MOD = 6
EPS = 1e-6

V7X_LANES = 128
V7X_VMEM_BYTES = 64 * 1024 * 1024
MOD_ROWS = 8

ADA_TN = 512
MM_TN = 512
IN_TM = 512
IN_TN = 1024
ROW_TILE = 256
RET_TILE = 1024
RET_SUB = 256
EXPERT_ROWS = 512
EXPERT_SUB = 256
EXPERT_SPLIT = 2
DMA_UNROLL = 8
N_DMA_PRIORITIES = 2


def _vmem_limit(block_bytes, scratch_bytes):
    want = 2 * block_bytes + scratch_bytes
    return int(min(V7X_VMEM_BYTES - 4 * 1024 * 1024, max(2 * want, 32 * 1024 * 1024)))


def _nbytes(shape, dtype):
    n = 1
    for s in shape:
        n *= s
    return n * jnp.dtype(dtype).itemsize


def _rms_mod(x, g, shift, scale):
    y = x * lax.rsqrt(jnp.mean(x * x, axis=-1, keepdims=True) + EPS) * g
    return y * (1.0 + scale) + shift


def _ada_norm_kernel(cc_ref, w_ref, b_ref, x_ref, g_ref, mod_ref, h_ref, lead_ref, *, n_lead):
    j = pl.program_id(0)
    a = cc_ref[...]
    s = (a * jax.nn.sigmoid(a)).astype(BF16)
    m = jnp.dot(s, w_ref[...].astype(BF16), preferred_element_type=F32) + b_ref[...]
    mod_ref[...] = m

    @pl.when(j < n_lead)
    def _():
        lead_ref[j] = m

    @pl.when(j >= n_lead)
    def _():
        half = n_lead // 2
        shift = jnp.concatenate([lead_ref[t][0:1, :] for t in range(half)], axis=1)
        scale = jnp.concatenate([lead_ref[t][0:1, :] for t in range(half, n_lead)], axis=1)
        h_ref[...] = _rms_mod(x_ref[...], g_ref[...], shift, scale).astype(h_ref.dtype)


def _ada_norm(cc, w, b, x, g):
    d, n = w.shape
    m_rows = x.shape[0]
    n_tiles = n // ADA_TN
    n_lead = 2 * d // ADA_TN
    assert m_rows % (n_tiles - n_lead) == 0
    tm = m_rows // (n_tiles - n_lead)
    assert tm % 16 == 0

    def row_idx(j):
        return (jnp.maximum(j - n_lead, 0), 0)

    blocks = (_nbytes((MOD_ROWS, d), F32) + _nbytes((d, ADA_TN), F32) + 2 * _nbytes((MOD_ROWS, ADA_TN), F32)
              + _nbytes((tm, d), F32) + _nbytes((tm, d), BF16))
    scratch = _nbytes((n_lead, MOD_ROWS, ADA_TN), F32) + _nbytes((d, ADA_TN), BF16)
    return pl.pallas_call(
        functools.partial(_ada_norm_kernel, n_lead=n_lead),
        grid=(n_tiles,),
        in_specs=[pl.BlockSpec((MOD_ROWS, d), lambda j: (0, 0)),
                  pl.BlockSpec((d, ADA_TN), lambda j: (0, j)),
                  pl.BlockSpec((1, ADA_TN), lambda j: (0, j)),
                  pl.BlockSpec((tm, d), row_idx),
                  pl.BlockSpec((1, d), lambda j: (0, 0))],
        out_specs=[pl.BlockSpec((MOD_ROWS, ADA_TN), lambda j: (0, j)),
                   pl.BlockSpec((tm, d), row_idx)],
        out_shape=[jax.ShapeDtypeStruct((MOD_ROWS, n), F32), jax.ShapeDtypeStruct((m_rows, d), BF16)],
        scratch_shapes=[pltpu.VMEM((n_lead, MOD_ROWS, ADA_TN), F32)],
        compiler_params=pltpu.CompilerParams(
            dimension_semantics=("arbitrary",),
            vmem_limit_bytes=_vmem_limit(blocks, scratch)),
        name="ada_norm",
    )(cc, w, b, x, g)


def _norm_mod_kernel(x_ref, g_ref, sh_ref, sc_ref, o_ref, *, row):
    h = _rms_mod(x_ref[...], g_ref[...], sh_ref[row:row + 1, :], sc_ref[row:row + 1, :])
    o_ref[...] = h.astype(o_ref.dtype)


def _norm_mod(x, g, mod, row, shift_chunk, scale_chunk, tm):
    m, d = x.shape
    blocks = _nbytes((tm, d), F32) * 2 + 3 * _nbytes((MOD_ROWS, d), F32)
    return pl.pallas_call(
        functools.partial(_norm_mod_kernel, row=row),
        grid=(m // tm,),
        in_specs=[pl.BlockSpec((tm, d), lambda i: (i, 0)),
                  pl.BlockSpec((1, d), lambda i: (0, 0)),
                  pl.BlockSpec((MOD_ROWS, d), lambda i: (0, shift_chunk)),
                  pl.BlockSpec((MOD_ROWS, d), lambda i: (0, scale_chunk))],
        out_specs=pl.BlockSpec((tm, d), lambda i: (i, 0)),
        out_shape=jax.ShapeDtypeStruct((m, d), BF16),
        compiler_params=pltpu.CompilerParams(
            dimension_semantics=("arbitrary",),
            vmem_limit_bytes=_vmem_limit(blocks, 0)),
        name="norm_mod",
    )(x, g, mod, mod)


def _matmul_acc(a_refs, w_ref, wbf_ref):
    @pl.when(pl.program_id(1) == 0)
    def _():
        wbf_ref[...] = w_ref[...].astype(BF16)

    acc = None
    k0 = 0
    for a_ref in a_refs:
        kk = a_ref.shape[1]
        part = jnp.dot(a_ref[...], wbf_ref[k0:k0 + kk, :], preferred_element_type=F32)
        acc = part if acc is None else acc + part
        k0 += kk
    return acc


def _matmul_kernel(*refs, n_a):
    w_ref, o_ref, wbf_ref = refs[n_a:]
    o_ref[...] = _matmul_acc(refs[:n_a], w_ref, wbf_ref).astype(o_ref.dtype)


def _matmul_res_kernel(*refs, n_a):
    w_ref, x_ref, g_ref, o_ref, wbf_ref = refs[n_a:]
    o_ref[...] = x_ref[...] + g_ref[0:1, :] * _matmul_acc(refs[:n_a], w_ref, wbf_ref)


def _matmul(a_list, w, n_cols, col_off, tm, tn, residual=None, out_dtype=F32):
    m = a_list[0].shape[0]
    k = sum(a.shape[1] for a in a_list)
    n_a = len(a_list)
    joff = col_off // tn
    grid = (n_cols // tn, m // tm)
    in_specs = [pl.BlockSpec((tm, a.shape[1]), lambda j, i: (i, 0)) for a in a_list]
    in_specs.append(pl.BlockSpec((k, tn), lambda j, i: (0, j + joff)))
    args = list(a_list) + [w]
    blocks = _nbytes((tm, k), BF16) + _nbytes((k, tn), F32) + _nbytes((tm, tn), F32)
    if residual is None:
        body = functools.partial(_matmul_kernel, n_a=n_a)
    else:
        x, mod, gate_chunk = residual
        goff = gate_chunk * (n_cols // tn)
        in_specs += [pl.BlockSpec((tm, tn), lambda j, i: (i, j)),
                     pl.BlockSpec((MOD_ROWS, tn), lambda j, i: (0, goff + j))]
        args += [x, mod]
        blocks += _nbytes((tm, tn), F32) + _nbytes((MOD_ROWS, tn), F32)
        body = functools.partial(_matmul_res_kernel, n_a=n_a)
    return pl.pallas_call(
        body,
        grid=grid,
        in_specs=in_specs,
        out_specs=pl.BlockSpec((tm, tn), lambda j, i: (i, j)),
        out_shape=jax.ShapeDtypeStruct((m, n_cols), out_dtype),
        scratch_shapes=[pltpu.VMEM((k, tn), BF16)],
        compiler_params=pltpu.CompilerParams(
            dimension_semantics=("arbitrary", "arbitrary"),
            vmem_limit_bytes=_vmem_limit(blocks, _nbytes((k, tn), BF16))),
        name="matmul_res" if residual is not None else "matmul",
    )(*args)


def _rope(t, cos, sin_signed):
    half = RET_DK // 2
    rot = jnp.concatenate([pltpu.roll(t[:, :half], half // 2, 1),
                           pltpu.roll(t[:, half:], half // 2, 1)], axis=1)
    return t * cos + rot * sin_signed


def _in_proj_kernel(a_ref, w_ref, cos_ref, sin_ref, lng_ref, lnb_ref, o_ref, wbf_ref, *, tile_ends):
    j = pl.program_id(0)
    q_end, k_end, plain_end, u_end = tile_ends

    def acc():
        return _matmul_acc([a_ref], w_ref, wbf_ref)

    @pl.when(j < k_end)
    def _rotated():
        t = acc() * jnp.where(j < q_end, 1.0, RET_DK ** -0.5)
        cos = cos_ref[...]
        sin = sin_ref[...]
        for hh in range(t.shape[1] // RET_DK):
            cols = slice(hh * RET_DK, (hh + 1) * RET_DK)
            o_ref[:, cols] = _rope(t[:, cols], cos, sin).astype(o_ref.dtype)

    @pl.when((j >= k_end) & (j < plain_end))
    def _plain():
        o_ref[...] = acc().astype(o_ref.dtype)

    @pl.when((j >= plain_end) & (j < u_end))
    def _gelu():
        o_ref[...] = jax.nn.gelu(acc()).astype(o_ref.dtype)

    @pl.when(j >= u_end)
    def _gelu_norm():
        t = jax.nn.gelu(acc())
        for gg in range(t.shape[1] // SGU_GROUP_DIM):
            cols = slice(gg * SGU_GROUP_DIM, (gg + 1) * SGU_GROUP_DIM)
            v32 = t[:, cols]
            mu = jnp.mean(v32, axis=-1, keepdims=True)
            var = jnp.mean(jnp.square(v32 - mu), axis=-1, keepdims=True)
            vn = (v32 - mu) * lax.rsqrt(var + EPS) * lng_ref[:, cols] + lnb_ref[:, cols]
            o_ref[:, cols] = vn.astype(o_ref.dtype)


def _in_proj(a, w, cos, sin_signed, ln_g, ln_b, qk_w, v_gate_w, sgu_w, tm, tn):
    m, k = a.shape
    n = w.shape[1]
    assert n == 2 * qk_w + v_gate_w + 2 * sgu_w
    assert qk_w % tn == 0 and v_gate_w % tn == 0 and sgu_w % tn == 0
    assert tn % RET_DK == 0 and tn % SGU_GROUP_DIM == 0
    q_end = qk_w // tn
    k_end = 2 * q_end
    plain_end = k_end + v_gate_w // tn
    u_end = plain_end + sgu_w // tn

    def table_idx(j, i):
        return (jnp.where(j < k_end, i, 0), 0)

    def ln_idx(j, i):
        return (0, jnp.maximum(j - u_end, 0))

    blocks = (_nbytes((tm, k), BF16) + _nbytes((k, tn), F32) + _nbytes((tm, tn), BF16)
              + 2 * _nbytes((tm, RET_DK), F32))
    return pl.pallas_call(
        functools.partial(_in_proj_kernel, tile_ends=(q_end, k_end, plain_end, u_end)),
        grid=(n // tn, m // tm),
        in_specs=[pl.BlockSpec((tm, k), lambda j, i: (i, 0)),
                  pl.BlockSpec((k, tn), lambda j, i: (0, j)),
                  pl.BlockSpec((tm, RET_DK), table_idx),
                  pl.BlockSpec((tm, RET_DK), table_idx),
                  pl.BlockSpec((1, tn), ln_idx),
                  pl.BlockSpec((1, tn), ln_idx)],
        out_specs=pl.BlockSpec((tm, tn), lambda j, i: (i, j)),
        out_shape=jax.ShapeDtypeStruct((m, n), BF16),
        scratch_shapes=[pltpu.VMEM((k, tn), BF16)],
        compiler_params=pltpu.CompilerParams(
            dimension_semantics=("arbitrary", "arbitrary"),
            vmem_limit_bytes=_vmem_limit(blocks, _nbytes((k, tn), BF16))),
        name="in_proj",
    )(a, w, cos, sin_signed, ln_g, ln_b)


def _ctx_state_kernel(lgf_ref, lgb_ref, k_ref, v_ref, sf_ref, sb_ref):
    h = pl.program_id(0)
    n = k_ref.shape[0]
    pos = lax.broadcasted_iota(jnp.int32, (n, 1), 0).astype(F32)
    k = k_ref[...] * (RET_DK ** -0.5)
    v = v_ref[...].astype(BF16)
    wf = jnp.exp((n - 1.0 - pos) * lgf_ref[h])
    wb = jnp.exp(pos * lgb_ref[h])
    tn_dims = (((0,), (0,)), ((), ()))
    sf_ref[0] = lax.dot_general((k * wf).astype(BF16), v, tn_dims, preferred_element_type=F32)
    sb_ref[0] = lax.dot_general((k * wb).astype(BF16), v, tn_dims, preferred_element_type=F32)


def _ctx_state(kv, lg_f, lg_b, n_heads):
    n = kv.shape[0]
    smem = pl.BlockSpec(memory_space=pltpu.SMEM)
    st = jax.ShapeDtypeStruct((n_heads, RET_DK, RET_DV), F32)
    return pl.pallas_call(
        _ctx_state_kernel,
        grid=(n_heads,),
        in_specs=[smem, smem,
                  pl.BlockSpec((n, RET_DK), lambda h: (0, h)),
                  pl.BlockSpec((n, RET_DV), lambda h: (0, n_heads + h))],
        out_specs=[pl.BlockSpec((1, RET_DK, RET_DV), lambda h: (h, 0, 0)),
                   pl.BlockSpec((1, RET_DK, RET_DV), lambda h: (h, 0, 0))],
        out_shape=[st, st],
        compiler_params=pltpu.CompilerParams(dimension_semantics=("arbitrary",)),
        name="ctx_state",
    )(lg_f, lg_b, kv, kv)


def _retention_kernel(lgf_ref, lgb_ref, q_ref, k_ref, v_ref, g_ref, s0f_ref, s0b_ref, gn_ref,
                      u_ref, vn_ref, ws_ref, bs_ref, o_ref, o2_ref, state_ref, yb_ref, *, n_steps):
    h = pl.program_id(0)
    p = pl.program_id(1)
    s = pl.program_id(2)
    c = RET_SUB
    n_sub = q_ref.shape[0] // c
    ii = lax.broadcasted_iota(jnp.int32, (c, c), 0)
    jj = lax.broadcasted_iota(jnp.int32, (c, c), 1)
    idx = lax.broadcasted_iota(jnp.int32, (c, 1), 0).astype(F32)
    nt_dims = (((1,), (1,)), ((), ()))
    tn_dims = (((0,), (0,)), ((), ()))

    def chunk(ci, decay, q_decay, k_decay, chunk_decay):
        rows = pl.ds(ci * c, c)
        qb = q_ref[rows, :]
        kb = k_ref[rows, :]
        v = v_ref[rows, :]
        st = state_ref[...]
        scores = lax.dot_general(qb, kb, nt_dims, preferred_element_type=F32) * decay
        out = (jnp.dot(scores.astype(BF16), v, preferred_element_type=F32)
               + jnp.dot(qb, st.astype(BF16), preferred_element_type=F32) * q_decay)
        state_ref[...] = st * chunk_decay + lax.dot_general(
            (kb.astype(F32) * k_decay).astype(BF16), v, tn_dims, preferred_element_type=F32)
        return out

    @pl.when(p == 0)
    def _backward():
        lg = lgb_ref[h]
        blk = n_steps - 1 - s

        @pl.when(s == 0)
        def _():
            state_ref[...] = s0b_ref[0]

        mask = jj > ii
        decay = jnp.where(mask, jnp.exp(jnp.where(mask, jj - ii, 0).astype(F32) * lg), 0.0)
        q_decay = jnp.exp((c - idx) * lg)
        k_decay = jnp.exp(idx * lg)
        chunk_decay = jnp.exp(jnp.full((1, RET_DV), c, F32) * lg)
        for ci in reversed(range(n_sub)):
            out = chunk(ci, decay, q_decay, k_decay, chunk_decay)
            start = pl.multiple_of(blk * (n_sub * c) + ci * c, c)
            yb_ref[pl.ds(start, c), :] = out

    @pl.when(p == 1)
    def _forward():
        lg = lgf_ref[h]

        @pl.when(s == 0)
        def _():
            state_ref[...] = s0f_ref[0]

        mask = ii >= jj
        decay = jnp.where(mask, jnp.exp(jnp.where(mask, ii - jj, 0).astype(F32) * lg), 0.0)
        q_decay = jnp.exp((idx + 1.0) * lg)
        k_decay = jnp.exp((c - 1.0 - idx) * lg)
        chunk_decay = jnp.exp(jnp.full((1, RET_DV), c, F32) * lg)
        for ci in range(n_sub):
            out = chunk(ci, decay, q_decay, k_decay, chunk_decay)
            start = pl.multiple_of(s * (n_sub * c) + ci * c, c)
            y = out + yb_ref[pl.ds(start, c), :]
            mu = jnp.mean(y, axis=-1, keepdims=True)
            var = jnp.mean(jnp.square(y - mu), axis=-1, keepdims=True)
            yn = (y - mu) * lax.rsqrt(var + EPS) * gn_ref[...]
            gate = g_ref[pl.ds(ci * c, c), :].astype(F32)
            o_ref[pl.ds(ci * c, c), :] = (gate * jax.nn.sigmoid(gate) * yn).astype(o_ref.dtype)

        ws = ws_ref[0].astype(BF16)
        bs = bs_ref[0]
        for ci in range(u_ref.shape[0] // SGU_CHUNK):
            rows = pl.ds(ci * SGU_CHUNK, SGU_CHUNK)
            mixed = jnp.dot(ws, vn_ref[rows, :], preferred_element_type=F32) + bs
            o2_ref[rows, :] = (u_ref[rows, :].astype(F32) * mixed).astype(o2_ref.dtype)


def _mixer(p, lg_f, lg_b, s0_f, s0_b, gn_g, w_s, b_s, n_heads, u_off):
    n_tok = p.shape[0]
    tl = RET_TILE
    n_steps = n_tok // tl
    ub = u_off // SGU_GROUP_DIM
    assert w_s.shape[0] == n_heads and RET_DV == SGU_GROUP_DIM and tl % RET_SUB == 0 and tl % SGU_CHUNK == 0
    smem = pl.BlockSpec(memory_space=pltpu.SMEM)

    def seq_blk(p_, s_):
        return jnp.where(p_ == 0, n_steps - 1 - s_, s_)

    def fwd_blk(col):
        return pl.BlockSpec((tl, RET_DV), lambda h, p_, s_: (s_ * p_, col + h))

    blocks = 8 * _nbytes((tl, RET_DK), BF16) + 2 * _nbytes((RET_DK, RET_DV), F32)
    scratch = _nbytes((RET_DK, RET_DV), F32) + _nbytes((n_tok, RET_DV), F32)
    out = jax.ShapeDtypeStruct((n_tok, n_heads * RET_DV), BF16)
    return pl.pallas_call(
        functools.partial(_retention_kernel, n_steps=n_steps),
        grid=(n_heads, 2, n_steps),
        in_specs=[smem, smem,
                  pl.BlockSpec((tl, RET_DK), lambda h, p_, s_: (seq_blk(p_, s_), h)),
                  pl.BlockSpec((tl, RET_DK), lambda h, p_, s_: (seq_blk(p_, s_), n_heads + h)),
                  pl.BlockSpec((tl, RET_DV), lambda h, p_, s_: (seq_blk(p_, s_), 2 * n_heads + h)),
                  fwd_blk(3 * n_heads),
                  pl.BlockSpec((1, RET_DK, RET_DV), lambda h, p_, s_: (h, 0, 0)),
                  pl.BlockSpec((1, RET_DK, RET_DV), lambda h, p_, s_: (h, 0, 0)),
                  pl.BlockSpec((1, RET_DV), lambda h, p_, s_: (0, h)),
                  fwd_blk(ub),
                  fwd_blk(ub + n_heads),
                  pl.BlockSpec((1, SGU_CHUNK, SGU_CHUNK), lambda h, p_, s_: (h, 0, 0)),
                  pl.BlockSpec((1, SGU_CHUNK, 1), lambda h, p_, s_: (h, 0, 0))],
        out_specs=[fwd_blk(0), fwd_blk(0)],
        out_shape=[out, out],
        scratch_shapes=[pltpu.VMEM((RET_DK, RET_DV), F32), pltpu.VMEM((n_tok, RET_DV), F32)],
        compiler_params=pltpu.CompilerParams(
            dimension_semantics=("arbitrary", "arbitrary", "arbitrary"),
            vmem_limit_bytes=_vmem_limit(blocks, scratch)),
        name="mixer",
    )(lg_f, lg_b, p, p, p, p, s0_f, s0_b, gn_g, p, p, w_s, b_s)


def _split_bf16(t):
    hi = t.astype(BF16)
    lo = (t - hi.astype(F32)).astype(BF16)
    return hi, lo


def _router_kernel(x_ref, g_ref, sh_ref, sc_ref, wr_ref, br_ref, h_ref, eid_ref, ew_ref):
    h = _rms_mod(x_ref[...], g_ref[...], sh_ref[0:1, :], sc_ref[0:1, :])
    half = h.shape[1] // 2
    bits = lax.bitcast_convert_type(h.astype(BF16).astype(F32), jnp.uint32)
    h_ref[...] = (bits[:, half:] & jnp.uint32(0xFFFF0000)) | (bits[:, :half] >> 16)
    h_hi, h_lo = _split_bf16(h)
    w_hi, w_lo = _split_bf16(wr_ref[...])
    logits = (jnp.dot(h_hi, w_hi, preferred_element_type=F32)
              + jnp.dot(h_lo, w_hi, preferred_element_type=F32)
              + jnp.dot(h_hi, w_lo, preferred_element_type=F32)) + br_ref[...]
    lane = lax.broadcasted_iota(jnp.int32, logits.shape, 1)
    lane_f = lane.astype(F32)
    neg = -jnp.inf

    def first_lane(hit):
        return jnp.min(jnp.where(hit, lane_f, float(V7X_LANES)), axis=-1, keepdims=True).astype(jnp.int32)

    gl = jnp.where(lane < MOE_GROUPS, logits, neg)
    g_max = jnp.max(gl, axis=-1, keepdims=True)
    g_sel = first_lane(gl == g_max)
    p_g = 1.0 / jnp.sum(jnp.exp(gl - g_max), axis=-1, keepdims=True)
    e_lo = MOE_GROUPS + g_sel * EXPERTS_PER_GROUP
    el = jnp.where((lane >= e_lo) & (lane < e_lo + EXPERTS_PER_GROUP), logits, neg)
    v1 = jnp.max(el, axis=-1, keepdims=True)
    i1 = first_lane(el == v1)
    el2 = jnp.where(lane == i1, neg, el)
    v2 = jnp.max(el2, axis=-1, keepdims=True)
    i2 = first_lane(el2 == v2)
    e2 = jnp.exp(v2 - v1)
    den = 1.0 + e2
    w1 = p_g * (1.0 / den)
    w2 = p_g * (e2 / den)
    eid_ref[...] = jnp.where(lane == 0, i1 - MOE_GROUPS, jnp.where(lane == 1, i2 - MOE_GROUPS, 0))
    ew_ref[...] = jnp.where(lane == 0, w1, jnp.where(lane == 1, w2, 0.0))


def _router(x, g, mod, shift_chunk, scale_chunk, wr, br):
    m, d = x.shape
    tm = ROW_TILE
    blocks = 2 * _nbytes((tm, d), F32) + 2 * _nbytes((MOD_ROWS, d), F32) + _nbytes((d, V7X_LANES), F32) \
        + 2 * _nbytes((tm, V7X_LANES), F32)
    return pl.pallas_call(
        _router_kernel,
        grid=(m // tm,),
        in_specs=[pl.BlockSpec((tm, d), lambda i: (i, 0)),
                  pl.BlockSpec((1, d), lambda i: (0, 0)),
                  pl.BlockSpec((MOD_ROWS, d), lambda i: (0, shift_chunk)),
                  pl.BlockSpec((MOD_ROWS, d), lambda i: (0, scale_chunk)),
                  pl.BlockSpec((d, V7X_LANES), lambda i: (0, 0)),
                  pl.BlockSpec((1, V7X_LANES), lambda i: (0, 0))],
        out_specs=[pl.BlockSpec((tm, d // 2), lambda i: (i, 0)),
                   pl.BlockSpec((tm, V7X_LANES), lambda i: (i, 0)),
                   pl.BlockSpec((tm, V7X_LANES), lambda i: (i, 0))],
        out_shape=[jax.ShapeDtypeStruct((m, d // 2), jnp.uint32),
                   jax.ShapeDtypeStruct((m, V7X_LANES), jnp.int32),
                   jax.ShapeDtypeStruct((m, V7X_LANES), F32)],
        compiler_params=pltpu.CompilerParams(
            dimension_semantics=("arbitrary",),
            vmem_limit_bytes=_vmem_limit(blocks, 0)),
        name="router",
    )(x, g, mod, mod, wr, br)


def _row_copy(src, src_row, dst, dst_row, sem):
    return pltpu.make_async_copy(src.at[pl.ds(src_row, 1), :], dst.at[pl.ds(dst_row, 1), :], sem)


def _for_rows(n_rows, fn):
    n_groups = lax.shift_right_logical(n_rows, DMA_UNROLL.bit_length() - 1)

    def group(gi, carry):
        for u in range(DMA_UNROLL):
            fn(gi * DMA_UNROLL + u, u)
        return carry

    def single(r, carry):
        fn(r, 0)
        return carry

    lax.fori_loop(0, n_groups, group, 0)
    lax.fori_loop(n_groups * DMA_UNROLL, n_rows, single, 0)


def _experts_kernel(be_ref, nact_ref, nv_ref, cs_ref, stok_ref, sdst_ref,
                    h_hbm, wg_ref, wu_ref, wd_ref, y_hbm,
                    x_ref, acc_ref, wgb_ref, wub_ref, wdb_ref, sem_in, sem_out):
    del be_ref
    b = pl.program_id(0)
    f = pl.program_id(1)
    n_active = nact_ref[0]
    active = b < n_active
    slot = lax.rem(b, 2)
    sub = EXPERT_SUB
    last_f = EXPERT_SPLIT - 1

    def wait_rows(src, dst, n, sem):
        n_whole = pl.multiple_of(lax.shift_right_logical(n, DMA_UNROLL.bit_length() - 1) * DMA_UNROLL, DMA_UNROLL)

        @pl.when(n_whole > 0)
        def _():
            rows = pl.ds(0, n_whole)
            pltpu.make_async_copy(src.at[rows, :], dst.at[rows, :], sem).wait()

        def single(r, carry):
            _row_copy(src, r, dst, r, sem).wait()
            return carry

        lax.fori_loop(n_whole, n, single, 0)

    def gather(bb, slot_, wait):
        def one(r, u):
            _row_copy(h_hbm, stok_ref[cs_ref[bb] + r], x_ref.at[slot_], r,
                      sem_in.at[slot_]).start(priority=u % N_DMA_PRIORITIES)

        if wait:
            wait_rows(h_hbm, x_ref.at[slot_], nv_ref[bb], sem_in.at[slot_])
        else:
            _for_rows(nv_ref[bb], one)

    def scatter(bb, wait):
        par = lax.rem(bb, 2)

        def one(r, u):
            _row_copy(acc_ref.at[par], r, y_hbm, sdst_ref[cs_ref[bb] + r],
                      sem_out.at[par]).start(priority=u % N_DMA_PRIORITIES)

        if wait:
            wait_rows(acc_ref.at[par], y_hbm, nv_ref[bb], sem_out.at[par])
        else:
            _for_rows(nv_ref[bb], one)

    @pl.when(active & (f == 0))
    def _rows_in():
        @pl.when(b == 0)
        def _():
            x_ref[...] = jnp.zeros(x_ref.shape, x_ref.dtype)
            gather(0, 0, False)

        @pl.when(b + 1 < n_active)
        def _():
            gather(b + 1, 1 - slot, False)

        gather(b, slot, True)

    @pl.when(active)
    def _compute():
        half = wgb_ref.shape[0] // 2

        def sub_block(j, wg, wu):
            rows = pl.ds(j * sub, sub)
            xw = x_ref[slot, rows, :]
            xa = lax.bitcast_convert_type(xw << 16, F32).astype(BF16)
            xb = lax.bitcast_convert_type(xw & jnp.uint32(0xFFFF0000), F32).astype(BF16)
            gate = (jnp.dot(xa, wg[:half, :], preferred_element_type=F32)
                    + jnp.dot(xb, wg[half:, :], preferred_element_type=F32))
            up = (jnp.dot(xa, wu[:half, :], preferred_element_type=F32)
                  + jnp.dot(xb, wu[half:, :], preferred_element_type=F32))
            hid = (gate * jax.nn.sigmoid(gate) * up).astype(BF16)

            @pl.when(f == 0)
            def _():
                acc_ref[slot, rows, :] = jnp.dot(hid, wdb_ref[...], preferred_element_type=F32)

            @pl.when(f > 0)
            def _():
                acc_ref[slot, rows, :] += jnp.dot(hid, wdb_ref[...], preferred_element_type=F32)

        wg = wg_ref[0].astype(BF16)
        wu = wu_ref[0].astype(BF16)
        wgb_ref[...] = wg
        wub_ref[...] = wu
        wdb_ref[...] = wd_ref[0].astype(BF16)
        sub_block(0, wg, wu)
        for j in range(1, x_ref.shape[1] // sub):
            @pl.when(j * sub < nv_ref[b])
            def _():
                sub_block(j, wgb_ref[...], wub_ref[...])

    @pl.when(active & (f == last_f))
    def _rows_out():
        @pl.when(b > 0)
        def _():
            scatter(b - 1, True)

        scatter(b, False)

        @pl.when(b == n_active - 1)
        def _():
            scatter(b, True)


def _experts(h, w_gate, w_up, w_down, plan, n_out_rows):
    block_e, n_active, n_valid, c_start, s_tok, s_dst = plan
    d = w_gate.shape[1]
    de = w_gate.shape[-1]
    des = de // EXPERT_SPLIT
    n_blocks = block_e.shape[0]
    assert h.shape[1] * 2 == d and EXPERT_SPLIT >= 2 and EXPERT_ROWS % EXPERT_SUB == 0

    def w_idx(transpose):
        def index_map(b, f, be, nact, *_):
            fs = jnp.where(b < nact[0], f, EXPERT_SPLIT - 1)
            return (be[b], fs, 0) if transpose else (be[b], 0, fs)
        return index_map

    blocks = 3 * _nbytes((d, des), F32)
    scratch = (_nbytes((2, EXPERT_ROWS, d // 2), jnp.uint32) + _nbytes((2, EXPERT_ROWS, d), F32)
               + 3 * _nbytes((d, des), BF16))
    grid_spec = pltpu.PrefetchScalarGridSpec(
        num_scalar_prefetch=6,
        grid=(n_blocks, EXPERT_SPLIT),
        in_specs=[pl.BlockSpec(memory_space=pl.ANY),
                  pl.BlockSpec((1, d, des), w_idx(False)),
                  pl.BlockSpec((1, d, des), w_idx(False)),
                  pl.BlockSpec((1, des, d), w_idx(True))],
        out_specs=pl.BlockSpec(memory_space=pl.ANY),
        scratch_shapes=[pltpu.VMEM((2, EXPERT_ROWS, d // 2), jnp.uint32),
                        pltpu.VMEM((2, EXPERT_ROWS, d), F32),
                        pltpu.VMEM((d, des), BF16), pltpu.VMEM((d, des), BF16), pltpu.VMEM((des, d), BF16),
                        pltpu.SemaphoreType.DMA((2,)), pltpu.SemaphoreType.DMA((2,))],
    )
    return pl.pallas_call(
        _experts_kernel,
        grid_spec=grid_spec,
        out_shape=jax.ShapeDtypeStruct((n_out_rows, d), F32),
        compiler_params=pltpu.CompilerParams(
            dimension_semantics=("arbitrary", "arbitrary"),
            vmem_limit_bytes=_vmem_limit(blocks, scratch)),
        name="experts",
    )(block_e, n_active, n_valid, c_start, s_tok, s_dst, h, w_gate, w_up, w_down)


def _combine_kernel(*refs, top_k):
    y_refs = refs[:top_k]
    w_ref, x_ref, g_ref, fg_ref, o_ref = refs[top_k:]
    y = y_refs[0][0] * w_ref[:, 0:1]
    for k in range(1, top_k):
        y = y + y_refs[k][0] * w_ref[:, k:k + 1]
    x = x_ref[...] + g_ref[0:1, :] * y
    o_ref[...] = x * lax.rsqrt(jnp.mean(x * x, axis=-1, keepdims=True) + EPS) * fg_ref[...]


def _combine(y, ew, x, mod, gate_chunk, final_g, top_k):
    m, d = x.shape
    tm = ROW_TILE
    blocks = (top_k + 2) * _nbytes((tm, d), F32) + _nbytes((MOD_ROWS, d), F32) + _nbytes((tm, V7X_LANES), F32)
    in_specs = [pl.BlockSpec((1, tm, d), functools.partial(lambda k, i: (k, i, 0), k)) for k in range(top_k)]
    in_specs += [pl.BlockSpec((tm, V7X_LANES), lambda i: (i, 0)),
                 pl.BlockSpec((tm, d), lambda i: (i, 0)),
                 pl.BlockSpec((MOD_ROWS, d), lambda i: (0, gate_chunk)),
                 pl.BlockSpec((1, d), lambda i: (0, 0))]
    return pl.pallas_call(
        functools.partial(_combine_kernel, top_k=top_k),
        grid=(m // tm,),
        in_specs=in_specs,
        out_specs=pl.BlockSpec((tm, d), lambda i: (i, 0)),
        out_shape=jax.ShapeDtypeStruct((m, d), F32),
        compiler_params=pltpu.CompilerParams(
            dimension_semantics=("arbitrary",),
            vmem_limit_bytes=_vmem_limit(blocks, 0)),
        name="combine",
    )(*([y] * top_k), ew, x, mod, final_g)


def _rope_tables(n_tokens):
    n_rows = n_tokens // GRID_W
    n_freq = RET_DK // 4
    freqs = ROPE_BASE ** (-jnp.arange(n_freq, dtype=F32) / n_freq)
    sign = jnp.concatenate([-jnp.ones((n_freq,), F32), jnp.ones((n_freq,), F32)])

    def half_tables(n_pos):
        ang = jnp.arange(n_pos, dtype=F32)[:, None] * freqs
        return jnp.tile(jnp.cos(ang), (1, 2)), jnp.tile(jnp.sin(ang), (1, 2)) * sign

    cos_r, sin_r = half_tables(n_rows)
    cos_c, sin_c = half_tables(GRID_W)

    def expand(by_row, by_col):
        by_row = jnp.broadcast_to(by_row[:, None, :], (n_rows, GRID_W, 2 * n_freq))
        by_col = jnp.broadcast_to(by_col[None, :, :], (n_rows, GRID_W, 2 * n_freq))
        return jnp.concatenate([by_row, by_col], axis=-1).reshape(n_tokens, RET_DK)

    return expand(cos_r, cos_c), expand(sin_r, sin_c)


def _dispatch(eid, n_tok, top_k):
    m = n_tok * top_k
    n_blocks = -(-m // EXPERT_ROWS) + N_EXPERTS
    e_flat = eid[:, :top_k].reshape(-1)
    order = jnp.argsort(e_flat).astype(jnp.int32)
    s_tok = order // top_k
    s_dst = (order % top_k) * n_tok + s_tok
    counts = jnp.sum((e_flat[:, None] == jnp.arange(N_EXPERTS, dtype=jnp.int32)[None, :]).astype(jnp.int32), axis=0)
    starts = jnp.cumsum(counts) - counts
    e_blocks = (counts + EXPERT_ROWS - 1) // EXPERT_ROWS
    b_ends = jnp.cumsum(e_blocks)
    n_active = b_ends[-1].astype(jnp.int32)
    blk = jnp.arange(n_blocks, dtype=jnp.int32)
    block_e = jnp.sum((jnp.minimum(blk, n_active - 1)[:, None] >= b_ends[None, :]).astype(jnp.int32), axis=1)
    block_e = jnp.minimum(block_e, N_EXPERTS - 1)
    own = (block_e[:, None] == jnp.arange(N_EXPERTS, dtype=jnp.int32)[None, :]).astype(jnp.int32)

    def of_block(per_expert):
        return jnp.sum(own * per_expert[None, :], axis=1)

    within = blk - of_block(b_ends - e_blocks)
    c_start = of_block(starts) + within * EXPERT_ROWS
    n_valid = jnp.where(blk < n_active, jnp.clip(of_block(counts) - within * EXPERT_ROWS, 0, EXPERT_ROWS), 0)
    return block_e, n_active.reshape(1), n_valid, c_start, s_tok, s_dst


def kernel(x, c, ctx, c_ctx, w_ada, b_ada, norm1_g, norm2_g, w_in, ret_decay_f, ret_decay_b, ret_gn_g, sgu_ln_g, sgu_ln_b, sgu_w_s, sgu_b_s, w_out, w_router_group, b_router_group, w_router_expert, b_router_expert, w_gate, w_up, w_down, final_g):
    batch, n_tok, d = x.shape
    assert batch == 1 and w_ada.shape[0] == 1
    n_heads = ret_decay_f.shape[-1]
    n_groups = sgu_w_s.shape[1]
    ret_qk_w = n_heads * RET_DK
    ret_w = n_heads * RET_DV
    sgu_w = n_groups * SGU_GROUP_DIM
    k_off = ret_qk_w
    u_off = 2 * ret_qk_w + 2 * ret_w
    in_w = u_off + 2 * sgu_w
    top_k = 2
    assert w_in.shape == (1, d, in_w) and w_out.shape == (1, ret_w + sgu_w, d)

    cc = jnp.concatenate([c, c_ctx[None, :], jnp.zeros((MOD_ROWS - 2, d), F32)], axis=0)
    mod, h1 = _ada_norm(cc, w_ada[0], b_ada[0].reshape(1, N_MOD * d), x[0], norm1_g)

    lg_f = -jnp.exp(ret_decay_f[0])
    lg_b = -jnp.exp(ret_decay_b[0])

    hc = _norm_mod(ctx[0], norm1_g, mod, 1, 0, 1, ROW_TILE)
    kv_c = _matmul([hc], w_in[0], ret_qk_w + ret_w, k_off, ctx.shape[1], MM_TN)
    s_f, s_b = _ctx_state(kv_c, lg_f, lg_b, n_heads)

    cos, sin_signed = _rope_tables(n_tok)
    p = _in_proj(h1, w_in[0], cos, sin_signed, sgu_ln_g, sgu_ln_b, ret_qk_w, 2 * ret_w, sgu_w, IN_TM, IN_TN)
    ret_out, sgu_out = _mixer(p, lg_f, lg_b, s_f, s_b, ret_gn_g, sgu_w_s[0],
                              sgu_b_s[0].reshape(n_groups, SGU_CHUNK, 1), n_heads, u_off)
    x1 = _matmul([ret_out, sgu_out], w_out[0], d, 0, IN_TM, IN_TN, residual=(x[0], mod, 2))

    n_router = MOE_GROUPS + N_EXPERTS
    wr = jnp.concatenate([w_router_group[0], w_router_expert[0],
                          jnp.zeros((d, V7X_LANES - n_router), F32)], axis=1)
    br = jnp.concatenate([b_router_group, b_router_expert,
                          jnp.zeros((1, V7X_LANES - n_router), F32)], axis=1)
    h2, eid, ew = _router(x1, norm2_g, mod, 3, 4, wr, br)
    plan = _dispatch(eid, n_tok, top_k)
    y = _experts(h2, w_gate[0], w_up[0], w_down[0], plan, top_k * n_tok)
    out = _combine(y.reshape(top_k, n_tok, d), ew, x1, mod, 5, final_g.reshape(1, d), top_k)
    return out.reshape(batch, n_tok, d)
```

```python
import functools

import jax
import jax.numpy as jnp
from jax import lax
from jax.experimental import pallas as pl
from jax.experimental.pallas import tpu as pltpu

F32 = jnp.float32
BF16 = jnp.bfloat16

GRID_W = 64
RET_DK = 256
RET_DV = 256
ROPE_BASE = 10000.0
SGU_GROUP_DIM = 256
SGU_CHUNK = 128
MOE_GROUPS = 8
EXPERTS_PER_GROUP = 8
N_EXPERTS = MOE_GROUPS * EXPERTS_PER_GROUP
N_MOD = 6
EPS = 1e-6

V7X_LANES = 128
V7X_VMEM_BYTES = 64 * 1024 * 1024
MOD_ROWS = 8

ADA_TN = 512
MM_TN = 512
IN_TM = 512
IN_TN = 1024
ROW_TILE = 256
RET_TILE = 1024
RET_SUB = 256
EXPERT_ROWS = 512
EXPERT_SUBS = (320, 192)
EXPERT_SPLIT = 2
DMA_UNROLL = 8
N_DMA_PRIORITIES = 2


def _vmem_limit(block_bytes, scratch_bytes):
    want = 2 * block_bytes + scratch_bytes
    return int(min(V7X_VMEM_BYTES - 4 * 1024 * 1024, max(2 * want, 32 * 1024 * 1024)))


def _nbytes(shape, dtype):
    n = 1
    for s in shape:
        n *= s
    return n * jnp.dtype(dtype).itemsize


def _rms_mod(x, g, shift, scale):
    y = x * lax.rsqrt(jnp.mean(x * x, axis=-1, keepdims=True) + EPS) * g
    return y * (1.0 + scale) + shift


def _ada_norm_kernel(cc_ref, w_ref, b_ref, x_ref, g_ref, mod_ref, h_ref, lead_ref, *, n_lead):
    j = pl.program_id(0)
    a = cc_ref[...]
    s = (a * jax.nn.sigmoid(a)).astype(BF16)
    m = jnp.dot(s, w_ref[...].astype(BF16), preferred_element_type=F32) + b_ref[...]
    mod_ref[...] = m

    @pl.when(j < n_lead)
    def _():
        lead_ref[j] = m

    @pl.when(j >= n_lead)
    def _():
        half = n_lead // 2
        shift = jnp.concatenate([lead_ref[t][0:1, :] for t in range(half)], axis=1)
        scale = jnp.concatenate([lead_ref[t][0:1, :] for t in range(half, n_lead)], axis=1)
        h_ref[...] = _rms_mod(x_ref[...], g_ref[...], shift, scale).astype(h_ref.dtype)


def _ada_norm(cc, w, b, x, g):
    d, n = w.shape
    m_rows = x.shape[0]
    n_tiles = n // ADA_TN
    n_lead = 2 * d // ADA_TN
    assert m_rows % (n_tiles - n_lead) == 0
    tm = m_rows // (n_tiles - n_lead)
    assert tm % 16 == 0

    def row_idx(j):
        return (jnp.maximum(j - n_lead, 0), 0)

    blocks = (_nbytes((MOD_ROWS, d), F32) + _nbytes((d, ADA_TN), F32) + 2 * _nbytes((MOD_ROWS, ADA_TN), F32)
              + _nbytes((tm, d), F32) + _nbytes((tm, d), BF16))
    scratch = _nbytes((n_lead, MOD_ROWS, ADA_TN), F32) + _nbytes((d, ADA_TN), BF16)
    return pl.pallas_call(
        functools.partial(_ada_norm_kernel, n_lead=n_lead),
        grid=(n_tiles,),
        in_specs=[pl.BlockSpec((MOD_ROWS, d), lambda j: (0, 0)),
                  pl.BlockSpec((d, ADA_TN), lambda j: (0, j)),
                  pl.BlockSpec((1, ADA_TN), lambda j: (0, j)),
                  pl.BlockSpec((tm, d), row_idx),
                  pl.BlockSpec((1, d), lambda j: (0, 0))],
        out_specs=[pl.BlockSpec((MOD_ROWS, ADA_TN), lambda j: (0, j)),
                   pl.BlockSpec((tm, d), row_idx)],
        out_shape=[jax.ShapeDtypeStruct((MOD_ROWS, n), F32), jax.ShapeDtypeStruct((m_rows, d), BF16)],
        scratch_shapes=[pltpu.VMEM((n_lead, MOD_ROWS, ADA_TN), F32)],
        compiler_params=pltpu.CompilerParams(
            dimension_semantics=("arbitrary",),
            vmem_limit_bytes=_vmem_limit(blocks, scratch)),
        name="ada_norm",
    )(cc, w, b, x, g)


def _norm_mod_kernel(x_ref, g_ref, sh_ref, sc_ref, o_ref, *, row):
    h = _rms_mod(x_ref[...], g_ref[...], sh_ref[row:row + 1, :], sc_ref[row:row + 1, :])
    o_ref[...] = h.astype(o_ref.dtype)


def _norm_mod(x, g, mod, row, shift_chunk, scale_chunk, tm):
    m, d = x.shape
    blocks = _nbytes((tm, d), F32) * 2 + 3 * _nbytes((MOD_ROWS, d), F32)
    return pl.pallas_call(
        functools.partial(_norm_mod_kernel, row=row),
        grid=(m // tm,),
        in_specs=[pl.BlockSpec((tm, d), lambda i: (i, 0)),
                  pl.BlockSpec((1, d), lambda i: (0, 0)),
                  pl.BlockSpec((MOD_ROWS, d), lambda i: (0, shift_chunk)),
                  pl.BlockSpec((MOD_ROWS, d), lambda i: (0, scale_chunk))],
        out_specs=pl.BlockSpec((tm, d), lambda i: (i, 0)),
        out_shape=jax.ShapeDtypeStruct((m, d), BF16),
        compiler_params=pltpu.CompilerParams(
            dimension_semantics=("arbitrary",),
            vmem_limit_bytes=_vmem_limit(blocks, 0)),
        name="norm_mod",
    )(x, g, mod, mod)


def _matmul_acc(a_refs, w_ref, wbf_ref):
    @pl.when(pl.program_id(1) == 0)
    def _():
        wbf_ref[...] = w_ref[...].astype(BF16)

    acc = None
    k0 = 0
    for a_ref in a_refs:
        kk = a_ref.shape[1]
        part = jnp.dot(a_ref[...], wbf_ref[k0:k0 + kk, :], preferred_element_type=F32)
        acc = part if acc is None else acc + part
        k0 += kk
    return acc


def _matmul_kernel(*refs, n_a):
    w_ref, o_ref, wbf_ref = refs[n_a:]
    o_ref[...] = _matmul_acc(refs[:n_a], w_ref, wbf_ref).astype(o_ref.dtype)


def _matmul_res_kernel(*refs, n_a):
    w_ref, x_ref, g_ref, o_ref, wbf_ref = refs[n_a:]
    o_ref[...] = x_ref[...] + g_ref[0:1, :] * _matmul_acc(refs[:n_a], w_ref, wbf_ref)


def _matmul(a_list, w, n_cols, col_off, tm, tn, residual=None, out_dtype=F32):
    m = a_list[0].shape[0]
    k = sum(a.shape[1] for a in a_list)
    n_a = len(a_list)
    joff = col_off // tn
    grid = (n_cols // tn, m // tm)
    in_specs = [pl.BlockSpec((tm, a.shape[1]), lambda j, i: (i, 0)) for a in a_list]
    in_specs.append(pl.BlockSpec((k, tn), lambda j, i: (0, j + joff)))
    args = list(a_list) + [w]
    blocks = _nbytes((tm, k), BF16) + _nbytes((k, tn), F32) + _nbytes((tm, tn), F32)
    if residual is None:
        body = functools.partial(_matmul_kernel, n_a=n_a)
    else:
        x, mod, gate_chunk = residual
        goff = gate_chunk * (n_cols // tn)
        in_specs += [pl.BlockSpec((tm, tn), lambda j, i: (i, j)),
                     pl.BlockSpec((MOD_ROWS, tn), lambda j, i: (0, goff + j))]
        args += [x, mod]
        blocks += _nbytes((tm, tn), F32) + _nbytes((MOD_ROWS, tn), F32)
        body = functools.partial(_matmul_res_kernel, n_a=n_a)
    return pl.pallas_call(
        body,
        grid=grid,
        in_specs=in_specs,
        out_specs=pl.BlockSpec((tm, tn), lambda j, i: (i, j)),
        out_shape=jax.ShapeDtypeStruct((m, n_cols), out_dtype),
        scratch_shapes=[pltpu.VMEM((k, tn), BF16)],
        compiler_params=pltpu.CompilerParams(
            dimension_semantics=("arbitrary", "arbitrary"),
            vmem_limit_bytes=_vmem_limit(blocks, _nbytes((k, tn), BF16))),
        name="matmul_res" if residual is not None else "matmul",
    )(*args)


def _rope(t, cos, sin_signed):
    half = RET_DK // 2
    rot = jnp.concatenate([pltpu.roll(t[:, :half], half // 2, 1),
                           pltpu.roll(t[:, half:], half // 2, 1)], axis=1)
    return t * cos + rot * sin_signed


def _in_proj_kernel(a_ref, w_ref, cos_ref, sin_ref, lng_ref, lnb_ref, o_ref, wbf_ref, *, tile_ends):
    j = pl.program_id(0)
    q_end, k_end, plain_end, u_end = tile_ends

    def acc():
        return _matmul_acc([a_ref], w_ref, wbf_ref)

    @pl.when(j < k_end)
    def _rotated():
        t = acc() * jnp.where(j < q_end, 1.0, RET_DK ** -0.5)
        cos = cos_ref[...]
        sin = sin_ref[...]
        for hh in range(t.shape[1] // RET_DK):
            cols = slice(hh * RET_DK, (hh + 1) * RET_DK)
            o_ref[:, cols] = _rope(t[:, cols], cos, sin).astype(o_ref.dtype)

    @pl.when((j >= k_end) & (j < plain_end))
    def _plain():
        o_ref[...] = acc().astype(o_ref.dtype)

    @pl.when((j >= plain_end) & (j < u_end))
    def _gelu():
        o_ref[...] = jax.nn.gelu(acc()).astype(o_ref.dtype)

    @pl.when(j >= u_end)
    def _gelu_norm():
        t = jax.nn.gelu(acc())
        for gg in range(t.shape[1] // SGU_GROUP_DIM):
            cols = slice(gg * SGU_GROUP_DIM, (gg + 1) * SGU_GROUP_DIM)
            v32 = t[:, cols]
            mu = jnp.mean(v32, axis=-1, keepdims=True)
            var = jnp.mean(jnp.square(v32 - mu), axis=-1, keepdims=True)
            vn = (v32 - mu) * lax.rsqrt(var + EPS) * lng_ref[:, cols] + lnb_ref[:, cols]
            o_ref[:, cols] = vn.astype(o_ref.dtype)


def _in_proj(a, w, cos, sin_signed, ln_g, ln_b, qk_w, v_gate_w, sgu_w, tm, tn):
    m, k = a.shape
    n = w.shape[1]
    assert n == 2 * qk_w + v_gate_w + 2 * sgu_w
    assert qk_w % tn == 0 and v_gate_w % tn == 0 and sgu_w % tn == 0
    assert tn % RET_DK == 0 and tn % SGU_GROUP_DIM == 0
    q_end = qk_w // tn
    k_end = 2 * q_end
    plain_end = k_end + v_gate_w // tn
    u_end = plain_end + sgu_w // tn

    def table_idx(j, i):
        return (jnp.where(j < k_end, i, 0), 0)

    def ln_idx(j, i):
        return (0, jnp.maximum(j - u_end, 0))

    blocks = (_nbytes((tm, k), BF16) + _nbytes((k, tn), F32) + _nbytes((tm, tn), BF16)
              + 2 * _nbytes((tm, RET_DK), F32))
    return pl.pallas_call(
        functools.partial(_in_proj_kernel, tile_ends=(q_end, k_end, plain_end, u_end)),
        grid=(n // tn, m // tm),
        in_specs=[pl.BlockSpec((tm, k), lambda j, i: (i, 0)),
                  pl.BlockSpec((k, tn), lambda j, i: (0, j)),
                  pl.BlockSpec((tm, RET_DK), table_idx),
                  pl.BlockSpec((tm, RET_DK), table_idx),
                  pl.BlockSpec((1, tn), ln_idx),
                  pl.BlockSpec((1, tn), ln_idx)],
        out_specs=pl.BlockSpec((tm, tn), lambda j, i: (i, j)),
        out_shape=jax.ShapeDtypeStruct((m, n), BF16),
        scratch_shapes=[pltpu.VMEM((k, tn), BF16)],
        compiler_params=pltpu.CompilerParams(
            dimension_semantics=("arbitrary", "arbitrary"),
            vmem_limit_bytes=_vmem_limit(blocks, _nbytes((k, tn), BF16))),
        name="in_proj",
    )(a, w, cos, sin_signed, ln_g, ln_b)


def _ctx_state_kernel(lgf_ref, lgb_ref, k_ref, v_ref, sf_ref, sb_ref):
    h = pl.program_id(0)
    n = k_ref.shape[0]
    pos = lax.broadcasted_iota(jnp.int32, (n, 1), 0).astype(F32)
    k = k_ref[...] * (RET_DK ** -0.5)
    v = v_ref[...].astype(BF16)
    wf = jnp.exp((n - 1.0 - pos) * lgf_ref[h])
    wb = jnp.exp(pos * lgb_ref[h])
    tn_dims = (((0,), (0,)), ((), ()))
    sf_ref[0] = lax.dot_general((k * wf).astype(BF16), v, tn_dims, preferred_element_type=F32)
    sb_ref[0] = lax.dot_general((k * wb).astype(BF16), v, tn_dims, preferred_element_type=F32)


def _ctx_state(kv, lg_f, lg_b, n_heads):
    n = kv.shape[0]
    smem = pl.BlockSpec(memory_space=pltpu.SMEM)
    st = jax.ShapeDtypeStruct((n_heads, RET_DK, RET_DV), F32)
    return pl.pallas_call(
        _ctx_state_kernel,
        grid=(n_heads,),
        in_specs=[smem, smem,
                  pl.BlockSpec((n, RET_DK), lambda h: (0, h)),
                  pl.BlockSpec((n, RET_DV), lambda h: (0, n_heads + h))],
        out_specs=[pl.BlockSpec((1, RET_DK, RET_DV), lambda h: (h, 0, 0)),
                   pl.BlockSpec((1, RET_DK, RET_DV), lambda h: (h, 0, 0))],
        out_shape=[st, st],
        compiler_params=pltpu.CompilerParams(dimension_semantics=("arbitrary",)),
        name="ctx_state",
    )(lg_f, lg_b, kv, kv)


def _retention_kernel(lgf_ref, lgb_ref, q_ref, k_ref, v_ref, g_ref, s0f_ref, s0b_ref, gn_ref,
                      u_ref, vn_ref, ws_ref, bs_ref, o_ref, o2_ref, state_ref, yb_ref, *, n_steps):
    h = pl.program_id(0)
    p = pl.program_id(1)
    s = pl.program_id(2)
    c = RET_SUB
    n_sub = q_ref.shape[0] // c
    ii = lax.broadcasted_iota(jnp.int32, (c, c), 0)
    jj = lax.broadcasted_iota(jnp.int32, (c, c), 1)
    idx = lax.broadcasted_iota(jnp.int32, (c, 1), 0).astype(F32)
    nt_dims = (((1,), (1,)), ((), ()))
    tn_dims = (((0,), (0,)), ((), ()))

    def chunk(ci, decay, q_decay, k_decay, chunk_decay):
        rows = pl.ds(ci * c, c)
        qb = q_ref[rows, :]
        kb = k_ref[rows, :]
        v = v_ref[rows, :]
        st = state_ref[...]
        scores = lax.dot_general(qb, kb, nt_dims, preferred_element_type=F32) * decay
        out = (jnp.dot(scores.astype(BF16), v, preferred_element_type=F32)
               + jnp.dot(qb, st.astype(BF16), preferred_element_type=F32) * q_decay)
        state_ref[...] = st * chunk_decay + lax.dot_general(
            (kb.astype(F32) * k_decay).astype(BF16), v, tn_dims, preferred_element_type=F32)
        return out

    @pl.when(p == 0)
    def _backward():
        lg = lgb_ref[h]
        blk = n_steps - 1 - s

        @pl.when(s == 0)
        def _():
            state_ref[...] = s0b_ref[0]

        mask = jj > ii
        decay = jnp.where(mask, jnp.exp(jnp.where(mask, jj - ii, 0).astype(F32) * lg), 0.0)
        q_decay = jnp.exp((c - idx) * lg)
        k_decay = jnp.exp(idx * lg)
        chunk_decay = jnp.exp(jnp.full((1, RET_DV), c, F32) * lg)
        for ci in reversed(range(n_sub)):
            out = chunk(ci, decay, q_decay, k_decay, chunk_decay)
            start = pl.multiple_of(blk * (n_sub * c) + ci * c, c)
            yb_ref[pl.ds(start, c), :] = out

    @pl.when(p == 1)
    def _forward():
        lg = lgf_ref[h]

        @pl.when(s == 0)
        def _():
            state_ref[...] = s0f_ref[0]

        mask = ii >= jj
        decay = jnp.where(mask, jnp.exp(jnp.where(mask, ii - jj, 0).astype(F32) * lg), 0.0)
        q_decay = jnp.exp((idx + 1.0) * lg)
        k_decay = jnp.exp((c - 1.0 - idx) * lg)
        chunk_decay = jnp.exp(jnp.full((1, RET_DV), c, F32) * lg)
        for ci in range(n_sub):
            out = chunk(ci, decay, q_decay, k_decay, chunk_decay)
            start = pl.multiple_of(s * (n_sub * c) + ci * c, c)
            y = out + yb_ref[pl.ds(start, c), :]
            mu = jnp.mean(y, axis=-1, keepdims=True)
            var = jnp.mean(jnp.square(y - mu), axis=-1, keepdims=True)
            yn = (y - mu) * lax.rsqrt(var + EPS) * gn_ref[...]
            gate = g_ref[pl.ds(ci * c, c), :].astype(F32)
            o_ref[pl.ds(ci * c, c), :] = (gate * jax.nn.sigmoid(gate) * yn).astype(o_ref.dtype)

        ws = ws_ref[0].astype(BF16)
        bs = bs_ref[0]
        for ci in range(u_ref.shape[0] // SGU_CHUNK):
            rows = pl.ds(ci * SGU_CHUNK, SGU_CHUNK)
            mixed = jnp.dot(ws, vn_ref[rows, :], preferred_element_type=F32) + bs
            o2_ref[rows, :] = (u_ref[rows, :].astype(F32) * mixed).astype(o2_ref.dtype)


def _mixer(p, lg_f, lg_b, s0_f, s0_b, gn_g, w_s, b_s, n_heads, u_off):
    n_tok = p.shape[0]
    tl = RET_TILE
    n_steps = n_tok // tl
    ub = u_off // SGU_GROUP_DIM
    assert w_s.shape[0] == n_heads and RET_DV == SGU_GROUP_DIM and tl % RET_SUB == 0 and tl % SGU_CHUNK == 0
    smem = pl.BlockSpec(memory_space=pltpu.SMEM)

    def seq_blk(p_, s_):
        return jnp.where(p_ == 0, n_steps - 1 - s_, s_)

    def fwd_blk(col):
        return pl.BlockSpec((tl, RET_DV), lambda h, p_, s_: (s_ * p_, col + h))

    blocks = 8 * _nbytes((tl, RET_DK), BF16) + 2 * _nbytes((RET_DK, RET_DV), F32)
    scratch = _nbytes((RET_DK, RET_DV), F32) + _nbytes((n_tok, RET_DV), F32)
    out = jax.ShapeDtypeStruct((n_tok, n_heads * RET_DV), BF16)
    return pl.pallas_call(
        functools.partial(_retention_kernel, n_steps=n_steps),
        grid=(n_heads, 2, n_steps),
        in_specs=[smem, smem,
                  pl.BlockSpec((tl, RET_DK), lambda h, p_, s_: (seq_blk(p_, s_), h)),
                  pl.BlockSpec((tl, RET_DK), lambda h, p_, s_: (seq_blk(p_, s_), n_heads + h)),
                  pl.BlockSpec((tl, RET_DV), lambda h, p_, s_: (seq_blk(p_, s_), 2 * n_heads + h)),
                  fwd_blk(3 * n_heads),
                  pl.BlockSpec((1, RET_DK, RET_DV), lambda h, p_, s_: (h, 0, 0)),
                  pl.BlockSpec((1, RET_DK, RET_DV), lambda h, p_, s_: (h, 0, 0)),
                  pl.BlockSpec((1, RET_DV), lambda h, p_, s_: (0, h)),
                  fwd_blk(ub),
                  fwd_blk(ub + n_heads),
                  pl.BlockSpec((1, SGU_CHUNK, SGU_CHUNK), lambda h, p_, s_: (h, 0, 0)),
                  pl.BlockSpec((1, SGU_CHUNK, 1), lambda h, p_, s_: (h, 0, 0))],
        out_specs=[fwd_blk(0), fwd_blk(0)],
        out_shape=[out, out],
        scratch_shapes=[pltpu.VMEM((RET_DK, RET_DV), F32), pltpu.VMEM((n_tok, RET_DV), F32)],
        compiler_params=pltpu.CompilerParams(
            dimension_semantics=("arbitrary", "arbitrary", "arbitrary"),
            vmem_limit_bytes=_vmem_limit(blocks, scratch)),
        name="mixer",
    )(lg_f, lg_b, p, p, p, p, s0_f, s0_b, gn_g, p, p, w_s, b_s)


def _split_bf16(t):
    hi = t.astype(BF16)
    lo = (t - hi.astype(F32)).astype(BF16)
    return hi, lo


def _router_kernel(x_ref, g_ref, sh_ref, sc_ref, wr_ref, br_ref, h_ref, eid_ref, ew_ref):
    h = _rms_mod(x_ref[...], g_ref[...], sh_ref[0:1, :], sc_ref[0:1, :])
    half = h.shape[1] // 2
    bits = lax.bitcast_convert_type(h.astype(BF16).astype(F32), jnp.uint32)
    h_ref[...] = (bits[:, half:] & jnp.uint32(0xFFFF0000)) | (bits[:, :half] >> 16)
    h_hi, h_lo = _split_bf16(h)
    w_hi, w_lo = _split_bf16(wr_ref[...])
    logits = (jnp.dot(h_hi, w_hi, preferred_element_type=F32)
              + jnp.dot(h_lo, w_hi, preferred_element_type=F32)
              + jnp.dot(h_hi, w_lo, preferred_element_type=F32)) + br_ref[...]
    lane = lax.broadcasted_iota(jnp.int32, logits.shape, 1)
    lane_f = lane.astype(F32)
    neg = -jnp.inf

    def first_lane(hit):
        return jnp.min(jnp.where(hit, lane_f, float(V7X_LANES)), axis=-1, keepdims=True).astype(jnp.int32)

    gl = jnp.where(lane < MOE_GROUPS, logits, neg)
    g_max = jnp.max(gl, axis=-1, keepdims=True)
    g_sel = first_lane(gl == g_max)
    p_g = 1.0 / jnp.sum(jnp.exp(gl - g_max), axis=-1, keepdims=True)
    e_lo = MOE_GROUPS + g_sel * EXPERTS_PER_GROUP
    el = jnp.where((lane >= e_lo) & (lane < e_lo + EXPERTS_PER_GROUP), logits, neg)
    v1 = jnp.max(el, axis=-1, keepdims=True)
    i1 = first_lane(el == v1)
    el2 = jnp.where(lane == i1, neg, el)
    v2 = jnp.max(el2, axis=-1, keepdims=True)
    i2 = first_lane(el2 == v2)
    e2 = jnp.exp(v2 - v1)
    den = 1.0 + e2
    w1 = p_g * (1.0 / den)
    w2 = p_g * (e2 / den)
    eid_ref[...] = jnp.where(lane == 0, i1 - MOE_GROUPS, jnp.where(lane == 1, i2 - MOE_GROUPS, 0))
    ew_ref[...] = jnp.where(lane == 0, w1, jnp.where(lane == 1, w2, 0.0))


def _router(x, g, mod, shift_chunk, scale_chunk, wr, br):
    m, d = x.shape
    tm = ROW_TILE
    blocks = 2 * _nbytes((tm, d), F32) + 2 * _nbytes((MOD_ROWS, d), F32) + _nbytes((d, V7X_LANES), F32) \
        + 2 * _nbytes((tm, V7X_LANES), F32)
    return pl.pallas_call(
        _router_kernel,
        grid=(m // tm,),
        in_specs=[pl.BlockSpec((tm, d), lambda i: (i, 0)),
                  pl.BlockSpec((1, d), lambda i: (0, 0)),
                  pl.BlockSpec((MOD_ROWS, d), lambda i: (0, shift_chunk)),
                  pl.BlockSpec((MOD_ROWS, d), lambda i: (0, scale_chunk)),
                  pl.BlockSpec((d, V7X_LANES), lambda i: (0, 0)),
                  pl.BlockSpec((1, V7X_LANES), lambda i: (0, 0))],
        out_specs=[pl.BlockSpec((tm, d // 2), lambda i: (i, 0)),
                   pl.BlockSpec((tm, V7X_LANES), lambda i: (i, 0)),
                   pl.BlockSpec((tm, V7X_LANES), lambda i: (i, 0))],
        out_shape=[jax.ShapeDtypeStruct((m, d // 2), jnp.uint32),
                   jax.ShapeDtypeStruct((m, V7X_LANES), jnp.int32),
                   jax.ShapeDtypeStruct((m, V7X_LANES), F32)],
        compiler_params=pltpu.CompilerParams(
            dimension_semantics=("arbitrary",),
            vmem_limit_bytes=_vmem_limit(blocks, 0)),
        name="router",
    )(x, g, mod, mod, wr, br)


def _row_copy(src, src_row, dst, dst_row, sem):
    return pltpu.make_async_copy(src.at[pl.ds(src_row, 1), :], dst.at[pl.ds(dst_row, 1), :], sem)


def _for_rows(n_rows, fn):
    n_groups = lax.shift_right_logical(n_rows, DMA_UNROLL.bit_length() - 1)

    def group(gi, carry):
        for u in range(DMA_UNROLL):
            fn(gi * DMA_UNROLL + u, u)
        return carry

    def single(r, carry):
        fn(r, 0)
        return carry

    lax.fori_loop(0, n_groups, group, 0)
    lax.fori_loop(n_groups * DMA_UNROLL, n_rows, single, 0)


def _experts_kernel(be_ref, nact_ref, nv_ref, cs_ref, stok_ref, sdst_ref,
                    h_hbm, wg_ref, wu_ref, wd_ref, y_hbm,
                    x_ref, acc_ref, wgb_ref, wub_ref, wdb_ref, sem_in, sem_out):
    del be_ref
    b = pl.program_id(0)
    f = pl.program_id(1)
    n_active = nact_ref[0]
    active = b < n_active
    slot = lax.rem(b, 2)
    last_f = EXPERT_SPLIT - 1

    def wait_rows(src, dst, n, sem):
        n_whole = pl.multiple_of(lax.shift_right_logical(n, DMA_UNROLL.bit_length() - 1) * DMA_UNROLL, DMA_UNROLL)

        @pl.when(n_whole > 0)
        def _():
            rows = pl.ds(0, n_whole)
            pltpu.make_async_copy(src.at[rows, :], dst.at[rows, :], sem).wait()

        def single(r, carry):
            _row_copy(src, r, dst, r, sem).wait()
            return carry

        lax.fori_loop(n_whole, n, single, 0)

    def gather(bb, slot_, wait):
        def one(r, u):
            _row_copy(h_hbm, stok_ref[cs_ref[bb] + r], x_ref.at[slot_], r,
                      sem_in.at[slot_]).start(priority=u % N_DMA_PRIORITIES)

        if wait:
            wait_rows(h_hbm, x_ref.at[slot_], nv_ref[bb], sem_in.at[slot_])
        else:
            _for_rows(nv_ref[bb], one)

    def scatter(bb, wait):
        par = lax.rem(bb, 2)

        def one(r, u):
            _row_copy(acc_ref.at[par], r, y_hbm, sdst_ref[cs_ref[bb] + r],
                      sem_out.at[par]).start(priority=u % N_DMA_PRIORITIES)

        if wait:
            wait_rows(acc_ref.at[par], y_hbm, nv_ref[bb], sem_out.at[par])
        else:
            _for_rows(nv_ref[bb], one)

    @pl.when(active & (f == 0))
    def _rows_in():
        @pl.when(b == 0)
        def _():
            x_ref[...] = jnp.zeros(x_ref.shape, x_ref.dtype)
            gather(0, 0, False)

        @pl.when(b + 1 < n_active)
        def _():
            gather(b + 1, 1 - slot, False)

        gather(b, slot, True)

    @pl.when(active)
    def _compute():
        half = wgb_ref.shape[0] // 2

        def sub_block(start, size, wg, wu):
            rows = pl.ds(start, size)
            xw = x_ref[slot, rows, :]
            xa = lax.bitcast_convert_type(xw << 16, F32).astype(BF16)
            xb = lax.bitcast_convert_type(xw & jnp.uint32(0xFFFF0000), F32).astype(BF16)
            gate = (jnp.dot(xa, wg[:half, :], preferred_element_type=F32)
                    + jnp.dot(xb, wg[half:, :], preferred_element_type=F32))
            up = (jnp.dot(xa, wu[:half, :], preferred_element_type=F32)
                  + jnp.dot(xb, wu[half:, :], preferred_element_type=F32))
            hid = (gate * jax.nn.sigmoid(gate) * up).astype(BF16)

            @pl.when(f == 0)
            def _():
                acc_ref[slot, rows, :] = jnp.dot(hid, wdb_ref[...], preferred_element_type=F32)

            @pl.when(f > 0)
            def _():
                acc_ref[slot, rows, :] += jnp.dot(hid, wdb_ref[...], preferred_element_type=F32)

        wg = wg_ref[0].astype(BF16)
        wu = wu_ref[0].astype(BF16)
        wgb_ref[...] = wg
        wub_ref[...] = wu
        wdb_ref[...] = wd_ref[0].astype(BF16)
        starts = [sum(EXPERT_SUBS[:j]) for j in range(len(EXPERT_SUBS))]
        sub_block(starts[0], EXPERT_SUBS[0], wg, wu)
        for start, size in zip(starts[1:], EXPERT_SUBS[1:]):
            @pl.when(start < nv_ref[b])
            def _():
                sub_block(start, size, wgb_ref[...], wub_ref[...])

    @pl.when(active & (f == last_f))
    def _rows_out():
        @pl.when(b > 0)
        def _():
            scatter(b - 1, True)

        scatter(b, False)

        @pl.when(b == n_active - 1)
        def _():
            scatter(b, True)


def _experts(h, w_gate, w_up, w_down, plan, n_out_rows):
    block_e, n_active, n_valid, c_start, s_tok, s_dst = plan
    d = w_gate.shape[1]
    de = w_gate.shape[-1]
    des = de // EXPERT_SPLIT
    n_blocks = block_e.shape[0]
    assert h.shape[1] * 2 == d and EXPERT_SPLIT >= 2 and sum(EXPERT_SUBS) == EXPERT_ROWS

    def w_idx(transpose):
        def index_map(b, f, be, nact, *_):
            fs = jnp.where(b < nact[0], f, EXPERT_SPLIT - 1)
            return (be[b], fs, 0) if transpose else (be[b], 0, fs)
        return index_map

    blocks = 3 * _nbytes((d, des), F32)
    scratch = (_nbytes((2, EXPERT_ROWS, d // 2), jnp.uint32) + _nbytes((2, EXPERT_ROWS, d), F32)
               + 3 * _nbytes((d, des), BF16))
    grid_spec = pltpu.PrefetchScalarGridSpec(
        num_scalar_prefetch=6,
        grid=(n_blocks, EXPERT_SPLIT),
        in_specs=[pl.BlockSpec(memory_space=pl.ANY),
                  pl.BlockSpec((1, d, des), w_idx(False)),
                  pl.BlockSpec((1, d, des), w_idx(False)),
                  pl.BlockSpec((1, des, d), w_idx(True))],
        out_specs=pl.BlockSpec(memory_space=pl.ANY),
        scratch_shapes=[pltpu.VMEM((2, EXPERT_ROWS, d // 2), jnp.uint32),
                        pltpu.VMEM((2, EXPERT_ROWS, d), F32),
                        pltpu.VMEM((d, des), BF16), pltpu.VMEM((d, des), BF16), pltpu.VMEM((des, d), BF16),
                        pltpu.SemaphoreType.DMA((2,)), pltpu.SemaphoreType.DMA((2,))],
    )
    return pl.pallas_call(
        _experts_kernel,
        grid_spec=grid_spec,
        out_shape=jax.ShapeDtypeStruct((n_out_rows, d), F32),
        compiler_params=pltpu.CompilerParams(
            dimension_semantics=("arbitrary", "arbitrary"),
            vmem_limit_bytes=_vmem_limit(blocks, scratch)),
        name="experts",
    )(block_e, n_active, n_valid, c_start, s_tok, s_dst, h, w_gate, w_up, w_down)


def _combine_kernel(*refs, top_k):
    y_refs = refs[:top_k]
    w_ref, x_ref, g_ref, fg_ref, o_ref = refs[top_k:]
    y = y_refs[0][0] * w_ref[:, 0:1]
    for k in range(1, top_k):
        y = y + y_refs[k][0] * w_ref[:, k:k + 1]
    x = x_ref[...] + g_ref[0:1, :] * y
    o_ref[...] = x * lax.rsqrt(jnp.mean(x * x, axis=-1, keepdims=True) + EPS) * fg_ref[...]


def _combine(y, ew, x, mod, gate_chunk, final_g, top_k):
    m, d = x.shape
    tm = ROW_TILE
    blocks = (top_k + 2) * _nbytes((tm, d), F32) + _nbytes((MOD_ROWS, d), F32) + _nbytes((tm, V7X_LANES), F32)
    in_specs = [pl.BlockSpec((1, tm, d), functools.partial(lambda k, i: (k, i, 0), k)) for k in range(top_k)]
    in_specs += [pl.BlockSpec((tm, V7X_LANES), lambda i: (i, 0)),
                 pl.BlockSpec((tm, d), lambda i: (i, 0)),
                 pl.BlockSpec((MOD_ROWS, d), lambda i: (0, gate_chunk)),
                 pl.BlockSpec((1, d), lambda i: (0, 0))]
    return pl.pallas_call(
        functools.partial(_combine_kernel, top_k=top_k),
        grid=(m // tm,),
        in_specs=in_specs,
        out_specs=pl.BlockSpec((tm, d), lambda i: (i, 0)),
        out_shape=jax.ShapeDtypeStruct((m, d), F32),
        compiler_params=pltpu.CompilerParams(
            dimension_semantics=("arbitrary",),
            vmem_limit_bytes=_vmem_limit(blocks, 0)),
        name="combine",
    )(*([y] * top_k), ew, x, mod, final_g)


def _rope_tables(n_tokens):
    n_rows = n_tokens // GRID_W
    n_freq = RET_DK // 4
    freqs = ROPE_BASE ** (-jnp.arange(n_freq, dtype=F32) / n_freq)
    sign = jnp.concatenate([-jnp.ones((n_freq,), F32), jnp.ones((n_freq,), F32)])

    def half_tables(n_pos):
        ang = jnp.arange(n_pos, dtype=F32)[:, None] * freqs
        return jnp.tile(jnp.cos(ang), (1, 2)), jnp.tile(jnp.sin(ang), (1, 2)) * sign

    cos_r, sin_r = half_tables(n_rows)
    cos_c, sin_c = half_tables(GRID_W)

    def expand(by_row, by_col):
        by_row = jnp.broadcast_to(by_row[:, None, :], (n_rows, GRID_W, 2 * n_freq))
        by_col = jnp.broadcast_to(by_col[None, :, :], (n_rows, GRID_W, 2 * n_freq))
        return jnp.concatenate([by_row, by_col], axis=-1).reshape(n_tokens, RET_DK)

    return expand(cos_r, cos_c), expand(sin_r, sin_c)


def _dispatch(eid, n_tok, top_k):
    m = n_tok * top_k
    n_blocks = -(-m // EXPERT_ROWS) + N_EXPERTS
    e_flat = eid[:, :top_k].reshape(-1)
    order = jnp.argsort(e_flat).astype(jnp.int32)
    s_tok = order // top_k
    s_dst = (order % top_k) * n_tok + s_tok
    counts = jnp.sum((e_flat[:, None] == jnp.arange(N_EXPERTS, dtype=jnp.int32)[None, :]).astype(jnp.int32), axis=0)
    starts = jnp.cumsum(counts) - counts
    e_blocks = (counts + EXPERT_ROWS - 1) // EXPERT_ROWS
    b_ends = jnp.cumsum(e_blocks)
    n_active = b_ends[-1].astype(jnp.int32)
    blk = jnp.arange(n_blocks, dtype=jnp.int32)
    block_e = jnp.sum((jnp.minimum(blk, n_active - 1)[:, None] >= b_ends[None, :]).astype(jnp.int32), axis=1)
    block_e = jnp.minimum(block_e, N_EXPERTS - 1)
    own = (block_e[:, None] == jnp.arange(N_EXPERTS, dtype=jnp.int32)[None, :]).astype(jnp.int32)

    def of_block(per_expert):
        return jnp.sum(own * per_expert[None, :], axis=1)

    within = blk - of_block(b_ends - e_blocks)
    c_start = of_block(starts) + within * EXPERT_ROWS
    n_valid = jnp.where(blk < n_active, jnp.clip(of_block(counts) - within * EXPERT_ROWS, 0, EXPERT_ROWS), 0)
    return block_e, n_active.reshape(1), n_valid, c_start, s_tok, s_dst


def kernel(x, c, ctx, c_ctx, w_ada, b_ada, norm1_g, norm2_g, w_in, ret_decay_f, ret_decay_b, ret_gn_g, sgu_ln_g, sgu_ln_b, sgu_w_s, sgu_b_s, w_out, w_router_group, b_router_group, w_router_expert, b_router_expert, w_gate, w_up, w_down, final_g):
    batch, n_tok, d = x.shape
    assert batch == 1 and w_ada.shape[0] == 1
    n_heads = ret_decay_f.shape[-1]
    n_groups = sgu_w_s.shape[1]
    ret_qk_w = n_heads * RET_DK
    ret_w = n_heads * RET_DV
    sgu_w = n_groups * SGU_GROUP_DIM
    k_off = ret_qk_w
    u_off = 2 * ret_qk_w + 2 * ret_w
    in_w = u_off + 2 * sgu_w
    top_k = 2
    assert w_in.shape == (1, d, in_w) and w_out.shape == (1, ret_w + sgu_w, d)

    cc = jnp.concatenate([c, c_ctx[None, :], jnp.zeros((MOD_ROWS - 2, d), F32)], axis=0)
    mod, h1 = _ada_norm(cc, w_ada[0], b_ada[0].reshape(1, N_MOD * d), x[0], norm1_g)

    lg_f = -jnp.exp(ret_decay_f[0])
    lg_b = -jnp.exp(ret_decay_b[0])

    hc = _norm_mod(ctx[0], norm1_g, mod, 1, 0, 1, ROW_TILE)
    kv_c = _matmul([hc], w_in[0], ret_qk_w + ret_w, k_off, ctx.shape[1], MM_TN)
    s_f, s_b = _ctx_state(kv_c, lg_f, lg_b, n_heads)

    cos, sin_signed = _rope_tables(n_tok)
    p = _in_proj(h1, w_in[0], cos, sin_signed, sgu_ln_g, sgu_ln_b, ret_qk_w, 2 * ret_w, sgu_w, IN_TM, IN_TN)
    ret_out, sgu_out = _mixer(p, lg_f, lg_b, s_f, s_b, ret_gn_g, sgu_w_s[0],
                              sgu_b_s[0].reshape(n_groups, SGU_CHUNK, 1), n_heads, u_off)
    x1 = _matmul([ret_out, sgu_out], w_out[0], d, 0, IN_TM, IN_TN, residual=(x[0], mod, 2))

    n_router = MOE_GROUPS + N_EXPERTS
    wr = jnp.concatenate([w_router_group[0], w_router_expert[0],
                          jnp.zeros((d, V7X_LANES - n_router), F32)], axis=1)
    br = jnp.concatenate([b_router_group, b_router_expert,
                          jnp.zeros((1, V7X_LANES - n_router), F32)], axis=1)
    h2, eid, ew = _router(x1, norm2_g, mod, 3, 4, wr, br)
    plan = _dispatch(eid, n_tok, top_k)
    y = _experts(h2, w_gate[0], w_up[0], w_down[0], plan, top_k * n_tok)
    out = _combine(y.reshape(top_k, n_tok, d), ew, x1, mod, 5, final_g.reshape(1, d), top_k)
    return out.reshape(batch, n_tok, d)
```

```python
import functools

import jax
import jax.numpy as jnp
from jax import lax
from jax.experimental import pallas as pl
from jax.experimental.pallas import tpu as pltpu

F32 = jnp.float32
BF16 = jnp.bfloat16

GRID_W = 64
RET_DK = 256
RET_DV = 256
ROPE_BASE = 10000.0
SGU_GROUP_DIM = 256
SGU_CHUNK = 128
MOE_GROUPS = 8
EXPERTS_PER_GROUP = 8
N_EXPERTS = MOE_GROUPS * EXPERTS_PER_GROUP
N_MOD = 6
EPS = 1e-6

V7X_LANES = 128
V7X_VMEM_BYTES = 64 * 1024 * 1024
MOD_ROWS = 8

ADA_TN = 512
MM_TN = 512
IN_TM = 512
IN_TN = 1024
ROW_TILE = 256
RET_TILE = 1024
RET_SUB = 256
EXPERT_ROWS = 512
EXPERT_SUBS = (320, 192)
DMA_UNROLL = 8
N_DMA_PRIORITIES = 2


def _vmem_limit(block_bytes, scratch_bytes):
    want = 2 * block_bytes + scratch_bytes
    return int(min(V7X_VMEM_BYTES - 4 * 1024 * 1024, max(2 * want, 32 * 1024 * 1024)))


def _nbytes(shape, dtype):
    n = 1
    for s in shape:
        n *= s
    return n * jnp.dtype(dtype).itemsize


def _rms_mod(x, g, shift, scale):
    y = x * lax.rsqrt(jnp.mean(x * x, axis=-1, keepdims=True) + EPS) * g
    return y * (1.0 + scale) + shift


def _ada_norm_kernel(cc_ref, w_ref, b_ref, x_ref, g_ref, mod_ref, h_ref, lead_ref, *, n_lead):
    j = pl.program_id(0)
    a = cc_ref[...]
    s = (a * jax.nn.sigmoid(a)).astype(BF16)
    m = jnp.dot(s, w_ref[...].astype(BF16), preferred_element_type=F32) + b_ref[...]
    mod_ref[...] = m

    @pl.when(j < n_lead)
    def _():
        lead_ref[j] = m

    @pl.when(j >= n_lead)
    def _():
        half = n_lead // 2
        shift = jnp.concatenate([lead_ref[t][0:1, :] for t in range(half)], axis=1)
        scale = jnp.concatenate([lead_ref[t][0:1, :] for t in range(half, n_lead)], axis=1)
        h_ref[...] = _rms_mod(x_ref[...], g_ref[...], shift, scale).astype(h_ref.dtype)


def _ada_norm(cc, w, b, x, g):
    d, n = w.shape
    m_rows = x.shape[0]
    n_tiles = n // ADA_TN
    n_lead = 2 * d // ADA_TN
    assert m_rows % (n_tiles - n_lead) == 0
    tm = m_rows // (n_tiles - n_lead)
    assert tm % 16 == 0

    def row_idx(j):
        return (jnp.maximum(j - n_lead, 0), 0)

    blocks = (_nbytes((MOD_ROWS, d), F32) + _nbytes((d, ADA_TN), F32) + 2 * _nbytes((MOD_ROWS, ADA_TN), F32)
              + _nbytes((tm, d), F32) + _nbytes((tm, d), BF16))
    scratch = _nbytes((n_lead, MOD_ROWS, ADA_TN), F32) + _nbytes((d, ADA_TN), BF16)
    return pl.pallas_call(
        functools.partial(_ada_norm_kernel, n_lead=n_lead),
        grid=(n_tiles,),
        in_specs=[pl.BlockSpec((MOD_ROWS, d), lambda j: (0, 0)),
                  pl.BlockSpec((d, ADA_TN), lambda j: (0, j)),
                  pl.BlockSpec((1, ADA_TN), lambda j: (0, j)),
                  pl.BlockSpec((tm, d), row_idx),
                  pl.BlockSpec((1, d), lambda j: (0, 0))],
        out_specs=[pl.BlockSpec((MOD_ROWS, ADA_TN), lambda j: (0, j)),
                   pl.BlockSpec((tm, d), row_idx)],
        out_shape=[jax.ShapeDtypeStruct((MOD_ROWS, n), F32), jax.ShapeDtypeStruct((m_rows, d), BF16)],
        scratch_shapes=[pltpu.VMEM((n_lead, MOD_ROWS, ADA_TN), F32)],
        compiler_params=pltpu.CompilerParams(
            dimension_semantics=("arbitrary",),
            vmem_limit_bytes=_vmem_limit(blocks, scratch)),
        name="ada_norm",
    )(cc, w, b, x, g)


def _norm_mod_kernel(x_ref, g_ref, sh_ref, sc_ref, o_ref, *, row):
    h = _rms_mod(x_ref[...], g_ref[...], sh_ref[row:row + 1, :], sc_ref[row:row + 1, :])
    o_ref[...] = h.astype(o_ref.dtype)


def _norm_mod(x, g, mod, row, shift_chunk, scale_chunk, tm):
    m, d = x.shape
    blocks = _nbytes((tm, d), F32) * 2 + 3 * _nbytes((MOD_ROWS, d), F32)
    return pl.pallas_call(
        functools.partial(_norm_mod_kernel, row=row),
        grid=(m // tm,),
        in_specs=[pl.BlockSpec((tm, d), lambda i: (i, 0)),
                  pl.BlockSpec((1, d), lambda i: (0, 0)),
                  pl.BlockSpec((MOD_ROWS, d), lambda i: (0, shift_chunk)),
                  pl.BlockSpec((MOD_ROWS, d), lambda i: (0, scale_chunk))],
        out_specs=pl.BlockSpec((tm, d), lambda i: (i, 0)),
        out_shape=jax.ShapeDtypeStruct((m, d), BF16),
        compiler_params=pltpu.CompilerParams(
            dimension_semantics=("arbitrary",),
            vmem_limit_bytes=_vmem_limit(blocks, 0)),
        name="norm_mod",
    )(x, g, mod, mod)


def _matmul_acc(a_refs, w_ref, wbf_ref):
    @pl.when(pl.program_id(1) == 0)
    def _():
        wbf_ref[...] = w_ref[...].astype(BF16)

    acc = None
    k0 = 0
    for a_ref in a_refs:
        kk = a_ref.shape[1]
        part = jnp.dot(a_ref[...], wbf_ref[k0:k0 + kk, :], preferred_element_type=F32)
        acc = part if acc is None else acc + part
        k0 += kk
    return acc


def _matmul_kernel(*refs, n_a):
    w_ref, o_ref, wbf_ref = refs[n_a:]
    o_ref[...] = _matmul_acc(refs[:n_a], w_ref, wbf_ref).astype(o_ref.dtype)


def _matmul_res_kernel(*refs, n_a):
    w_ref, x_ref, g_ref, o_ref, wbf_ref = refs[n_a:]
    o_ref[...] = x_ref[...] + g_ref[0:1, :] * _matmul_acc(refs[:n_a], w_ref, wbf_ref)


def _matmul(a_list, w, n_cols, col_off, tm, tn, residual=None, out_dtype=F32):
    m = a_list[0].shape[0]
    k = sum(a.shape[1] for a in a_list)
    n_a = len(a_list)
    joff = col_off // tn
    grid = (n_cols // tn, m // tm)
    in_specs = [pl.BlockSpec((tm, a.shape[1]), lambda j, i: (i, 0)) for a in a_list]
    in_specs.append(pl.BlockSpec((k, tn), lambda j, i: (0, j + joff)))
    args = list(a_list) + [w]
    blocks = _nbytes((tm, k), BF16) + _nbytes((k, tn), F32) + _nbytes((tm, tn), F32)
    if residual is None:
        body = functools.partial(_matmul_kernel, n_a=n_a)
    else:
        x, mod, gate_chunk = residual
        goff = gate_chunk * (n_cols // tn)
        in_specs += [pl.BlockSpec((tm, tn), lambda j, i: (i, j)),
                     pl.BlockSpec((MOD_ROWS, tn), lambda j, i: (0, goff + j))]
        args += [x, mod]
        blocks += _nbytes((tm, tn), F32) + _nbytes((MOD_ROWS, tn), F32)
        body = functools.partial(_matmul_res_kernel, n_a=n_a)
    return pl.pallas_call(
        body,
        grid=grid,
        in_specs=in_specs,
        out_specs=pl.BlockSpec((tm, tn), lambda j, i: (i, j)),
        out_shape=jax.ShapeDtypeStruct((m, n_cols), out_dtype),
        scratch_shapes=[pltpu.VMEM((k, tn), BF16)],
        compiler_params=pltpu.CompilerParams(
            dimension_semantics=("arbitrary", "arbitrary"),
            vmem_limit_bytes=_vmem_limit(blocks, _nbytes((k, tn), BF16))),
        name="matmul_res" if residual is not None else "matmul",
    )(*args)


def _rope(t, cos, sin_signed):
    half = RET_DK // 2
    rot = jnp.concatenate([pltpu.roll(t[:, :half], half // 2, 1),
                           pltpu.roll(t[:, half:], half // 2, 1)], axis=1)
    return t * cos + rot * sin_signed


def _in_proj_kernel(a_ref, w_ref, cos_ref, sin_ref, lng_ref, lnb_ref, o_ref, wbf_ref, *, tile_ends):
    j = pl.program_id(0)
    q_end, k_end, plain_end, u_end = tile_ends

    def acc():
        return _matmul_acc([a_ref], w_ref, wbf_ref)

    @pl.when(j < k_end)
    def _rotated():
        t = acc() * jnp.where(j < q_end, 1.0, RET_DK ** -0.5)
        cos = cos_ref[...]
        sin = sin_ref[...]
        for hh in range(t.shape[1] // RET_DK):
            cols = slice(hh * RET_DK, (hh + 1) * RET_DK)
            o_ref[:, cols] = _rope(t[:, cols], cos, sin).astype(o_ref.dtype)

    @pl.when((j >= k_end) & (j < plain_end))
    def _plain():
        o_ref[...] = acc().astype(o_ref.dtype)

    @pl.when((j >= plain_end) & (j < u_end))
    def _gelu():
        o_ref[...] = jax.nn.gelu(acc()).astype(o_ref.dtype)

    @pl.when(j >= u_end)
    def _gelu_norm():
        t = jax.nn.gelu(acc())
        for gg in range(t.shape[1] // SGU_GROUP_DIM):
            cols = slice(gg * SGU_GROUP_DIM, (gg + 1) * SGU_GROUP_DIM)
            v32 = t[:, cols]
            mu = jnp.mean(v32, axis=-1, keepdims=True)
            var = jnp.mean(jnp.square(v32 - mu), axis=-1, keepdims=True)
            vn = (v32 - mu) * lax.rsqrt(var + EPS) * lng_ref[:, cols] + lnb_ref[:, cols]
            o_ref[:, cols] = vn.astype(o_ref.dtype)


def _in_proj(a, w, cos, sin_signed, ln_g, ln_b, qk_w, v_gate_w, sgu_w, tm, tn):
    m, k = a.shape
    n = w.shape[1]
    assert n == 2 * qk_w + v_gate_w + 2 * sgu_w
    assert qk_w % tn == 0 and v_gate_w % tn == 0 and sgu_w % tn == 0
    assert tn % RET_DK == 0 and tn % SGU_GROUP_DIM == 0
    q_end = qk_w // tn
    k_end = 2 * q_end
    plain_end = k_end + v_gate_w // tn
    u_end = plain_end + sgu_w // tn

    def table_idx(j, i):
        return (jnp.where(j < k_end, i, 0), 0)

    def ln_idx(j, i):
        return (0, jnp.maximum(j - u_end, 0))

    blocks = (_nbytes((tm, k), BF16) + _nbytes((k, tn), F32) + _nbytes((tm, tn), BF16)
              + 2 * _nbytes((tm, RET_DK), F32))
    return pl.pallas_call(
        functools.partial(_in_proj_kernel, tile_ends=(q_end, k_end, plain_end, u_end)),
        grid=(n // tn, m // tm),
        in_specs=[pl.BlockSpec((tm, k), lambda j, i: (i, 0)),
                  pl.BlockSpec((k, tn), lambda j, i: (0, j)),
                  pl.BlockSpec((tm, RET_DK), table_idx),
                  pl.BlockSpec((tm, RET_DK), table_idx),
                  pl.BlockSpec((1, tn), ln_idx),
                  pl.BlockSpec((1, tn), ln_idx)],
        out_specs=pl.BlockSpec((tm, tn), lambda j, i: (i, j)),
        out_shape=jax.ShapeDtypeStruct((m, n), BF16),
        scratch_shapes=[pltpu.VMEM((k, tn), BF16)],
        compiler_params=pltpu.CompilerParams(
            dimension_semantics=("arbitrary", "arbitrary"),
            vmem_limit_bytes=_vmem_limit(blocks, _nbytes((k, tn), BF16))),
        name="in_proj",
    )(a, w, cos, sin_signed, ln_g, ln_b)


def _ctx_state_kernel(lgf_ref, lgb_ref, k_ref, v_ref, sf_ref, sb_ref):
    h = pl.program_id(0)
    n = k_ref.shape[0]
    pos = lax.broadcasted_iota(jnp.int32, (n, 1), 0).astype(F32)
    k = k_ref[...] * (RET_DK ** -0.5)
    v = v_ref[...].astype(BF16)
    wf = jnp.exp((n - 1.0 - pos) * lgf_ref[h])
    wb = jnp.exp(pos * lgb_ref[h])
    tn_dims = (((0,), (0,)), ((), ()))
    sf_ref[0] = lax.dot_general((k * wf).astype(BF16), v, tn_dims, preferred_element_type=F32)
    sb_ref[0] = lax.dot_general((k * wb).astype(BF16), v, tn_dims, preferred_element_type=F32)


def _ctx_state(kv, lg_f, lg_b, n_heads):
    n = kv.shape[0]
    smem = pl.BlockSpec(memory_space=pltpu.SMEM)
    st = jax.ShapeDtypeStruct((n_heads, RET_DK, RET_DV), F32)
    return pl.pallas_call(
        _ctx_state_kernel,
        grid=(n_heads,),
        in_specs=[smem, smem,
                  pl.BlockSpec((n, RET_DK), lambda h: (0, h)),
                  pl.BlockSpec((n, RET_DV), lambda h: (0, n_heads + h))],
        out_specs=[pl.BlockSpec((1, RET_DK, RET_DV), lambda h: (h, 0, 0)),
                   pl.BlockSpec((1, RET_DK, RET_DV), lambda h: (h, 0, 0))],
        out_shape=[st, st],
        compiler_params=pltpu.CompilerParams(dimension_semantics=("arbitrary",)),
        name="ctx_state",
    )(lg_f, lg_b, kv, kv)


def _retention_kernel(lgf_ref, lgb_ref, q_ref, k_ref, v_ref, g_ref, s0f_ref, s0b_ref, gn_ref,
                      u_ref, vn_ref, ws_ref, bs_ref, o_ref, o2_ref, state_ref, yb_ref, *, n_steps):
    h = pl.program_id(0)
    p = pl.program_id(1)
    s = pl.program_id(2)
    c = RET_SUB
    n_sub = q_ref.shape[0] // c
    ii = lax.broadcasted_iota(jnp.int32, (c, c), 0)
    jj = lax.broadcasted_iota(jnp.int32, (c, c), 1)
    idx = lax.broadcasted_iota(jnp.int32, (c, 1), 0).astype(F32)
    nt_dims = (((1,), (1,)), ((), ()))
    tn_dims = (((0,), (0,)), ((), ()))

    def chunk(ci, decay, q_decay, k_decay, chunk_decay):
        rows = pl.ds(ci * c, c)
        qb = q_ref[rows, :]
        kb = k_ref[rows, :]
        v = v_ref[rows, :]
        st = state_ref[...]
        scores = lax.dot_general(qb, kb, nt_dims, preferred_element_type=F32) * decay
        out = (jnp.dot(scores.astype(BF16), v, preferred_element_type=F32)
               + jnp.dot(qb, st.astype(BF16), preferred_element_type=F32) * q_decay)
        state_ref[...] = st * chunk_decay + lax.dot_general(
            (kb.astype(F32) * k_decay).astype(BF16), v, tn_dims, preferred_element_type=F32)
        return out

    @pl.when(p == 0)
    def _backward():
        lg = lgb_ref[h]
        blk = n_steps - 1 - s

        @pl.when(s == 0)
        def _():
            state_ref[...] = s0b_ref[0]

        mask = jj > ii
        decay = jnp.where(mask, jnp.exp(jnp.where(mask, jj - ii, 0).astype(F32) * lg), 0.0)
        q_decay = jnp.exp((c - idx) * lg)
        k_decay = jnp.exp(idx * lg)
        chunk_decay = jnp.exp(jnp.full((1, RET_DV), c, F32) * lg)
        for ci in reversed(range(n_sub)):
            out = chunk(ci, decay, q_decay, k_decay, chunk_decay)
            start = pl.multiple_of(blk * (n_sub * c) + ci * c, c)
            yb_ref[pl.ds(start, c), :] = out

    @pl.when(p == 1)
    def _forward():
        lg = lgf_ref[h]

        @pl.when(s == 0)
        def _():
            state_ref[...] = s0f_ref[0]

        mask = ii >= jj
        decay = jnp.where(mask, jnp.exp(jnp.where(mask, ii - jj, 0).astype(F32) * lg), 0.0)
        q_decay = jnp.exp((idx + 1.0) * lg)
        k_decay = jnp.exp((c - 1.0 - idx) * lg)
        chunk_decay = jnp.exp(jnp.full((1, RET_DV), c, F32) * lg)
        for ci in range(n_sub):
            out = chunk(ci, decay, q_decay, k_decay, chunk_decay)
            start = pl.multiple_of(s * (n_sub * c) + ci * c, c)
            y = out + yb_ref[pl.ds(start, c), :]
            mu = jnp.mean(y, axis=-1, keepdims=True)
            var = jnp.mean(jnp.square(y - mu), axis=-1, keepdims=True)
            yn = (y - mu) * lax.rsqrt(var + EPS) * gn_ref[...]
            gate = g_ref[pl.ds(ci * c, c), :].astype(F32)
            o_ref[pl.ds(ci * c, c), :] = (gate * jax.nn.sigmoid(gate) * yn).astype(o_ref.dtype)

        ws = ws_ref[0].astype(BF16)
        bs = bs_ref[0]
        for ci in range(u_ref.shape[0] // SGU_CHUNK):
            rows = pl.ds(ci * SGU_CHUNK, SGU_CHUNK)
            mixed = jnp.dot(ws, vn_ref[rows, :], preferred_element_type=F32) + bs
            o2_ref[rows, :] = (u_ref[rows, :].astype(F32) * mixed).astype(o2_ref.dtype)


def _mixer(p, lg_f, lg_b, s0_f, s0_b, gn_g, w_s, b_s, n_heads, u_off):
    n_tok = p.shape[0]
    tl = RET_TILE
    n_steps = n_tok // tl
    ub = u_off // SGU_GROUP_DIM
    assert w_s.shape[0] == n_heads and RET_DV == SGU_GROUP_DIM and tl % RET_SUB == 0 and tl % SGU_CHUNK == 0
    smem = pl.BlockSpec(memory_space=pltpu.SMEM)

    def seq_blk(p_, s_):
        return jnp.where(p_ == 0, n_steps - 1 - s_, s_)

    def fwd_blk(col):
        return pl.BlockSpec((tl, RET_DV), lambda h, p_, s_: (s_ * p_, col + h))

    blocks = 8 * _nbytes((tl, RET_DK), BF16) + 2 * _nbytes((RET_DK, RET_DV), F32)
    scratch = _nbytes((RET_DK, RET_DV), F32) + _nbytes((n_tok, RET_DV), F32)
    out = jax.ShapeDtypeStruct((n_tok, n_heads * RET_DV), BF16)
    return pl.pallas_call(
        functools.partial(_retention_kernel, n_steps=n_steps),
        grid=(n_heads, 2, n_steps),
        in_specs=[smem, smem,
                  pl.BlockSpec((tl, RET_DK), lambda h, p_, s_: (seq_blk(p_, s_), h)),
                  pl.BlockSpec((tl, RET_DK), lambda h, p_, s_: (seq_blk(p_, s_), n_heads + h)),
                  pl.BlockSpec((tl, RET_DV), lambda h, p_, s_: (seq_blk(p_, s_), 2 * n_heads + h)),
                  fwd_blk(3 * n_heads),
                  pl.BlockSpec((1, RET_DK, RET_DV), lambda h, p_, s_: (h, 0, 0)),
                  pl.BlockSpec((1, RET_DK, RET_DV), lambda h, p_, s_: (h, 0, 0)),
                  pl.BlockSpec((1, RET_DV), lambda h, p_, s_: (0, h)),
                  fwd_blk(ub),
                  fwd_blk(ub + n_heads),
                  pl.BlockSpec((1, SGU_CHUNK, SGU_CHUNK), lambda h, p_, s_: (h, 0, 0)),
                  pl.BlockSpec((1, SGU_CHUNK, 1), lambda h, p_, s_: (h, 0, 0))],
        out_specs=[fwd_blk(0), fwd_blk(0)],
        out_shape=[out, out],
        scratch_shapes=[pltpu.VMEM((RET_DK, RET_DV), F32), pltpu.VMEM((n_tok, RET_DV), F32)],
        compiler_params=pltpu.CompilerParams(
            dimension_semantics=("arbitrary", "arbitrary", "arbitrary"),
            vmem_limit_bytes=_vmem_limit(blocks, scratch)),
        name="mixer",
    )(lg_f, lg_b, p, p, p, p, s0_f, s0_b, gn_g, p, p, w_s, b_s)


def _split_bf16(t):
    hi = t.astype(BF16)
    lo = (t - hi.astype(F32)).astype(BF16)
    return hi, lo


def _router_kernel(x_ref, g_ref, sh_ref, sc_ref, wr_ref, br_ref, h_ref, eid_ref, ew_ref):
    h = _rms_mod(x_ref[...], g_ref[...], sh_ref[0:1, :], sc_ref[0:1, :])
    half = h.shape[1] // 2
    bits = lax.bitcast_convert_type(h.astype(BF16).astype(F32), jnp.uint32)
    h_ref[...] = (bits[:, half:] & jnp.uint32(0xFFFF0000)) | (bits[:, :half] >> 16)
    h_hi, h_lo = _split_bf16(h)
    w_hi, w_lo = _split_bf16(wr_ref[...])
    logits = (jnp.dot(h_hi, w_hi, preferred_element_type=F32)
              + jnp.dot(h_lo, w_hi, preferred_element_type=F32)
              + jnp.dot(h_hi, w_lo, preferred_element_type=F32)) + br_ref[...]
    lane = lax.broadcasted_iota(jnp.int32, logits.shape, 1)
    lane_f = lane.astype(F32)
    neg = -jnp.inf

    def first_lane(hit):
        return jnp.min(jnp.where(hit, lane_f, float(V7X_LANES)), axis=-1, keepdims=True).astype(jnp.int32)

    gl = jnp.where(lane < MOE_GROUPS, logits, neg)
    g_max = jnp.max(gl, axis=-1, keepdims=True)
    g_sel = first_lane(gl == g_max)
    p_g = 1.0 / jnp.sum(jnp.exp(gl - g_max), axis=-1, keepdims=True)
    e_lo = MOE_GROUPS + g_sel * EXPERTS_PER_GROUP
    el = jnp.where((lane >= e_lo) & (lane < e_lo + EXPERTS_PER_GROUP), logits, neg)
    v1 = jnp.max(el, axis=-1, keepdims=True)
    i1 = first_lane(el == v1)
    el2 = jnp.where(lane == i1, neg, el)
    v2 = jnp.max(el2, axis=-1, keepdims=True)
    i2 = first_lane(el2 == v2)
    e2 = jnp.exp(v2 - v1)
    den = 1.0 + e2
    w1 = p_g * (1.0 / den)
    w2 = p_g * (e2 / den)
    eid_ref[...] = jnp.where(lane == 0, i1 - MOE_GROUPS, jnp.where(lane == 1, i2 - MOE_GROUPS, 0))
    ew_ref[...] = jnp.where(lane == 0, w1, jnp.where(lane == 1, w2, 0.0))


def _router(x, g, mod, shift_chunk, scale_chunk, wr, br):
    m, d = x.shape
    tm = ROW_TILE
    blocks = 2 * _nbytes((tm, d), F32) + 2 * _nbytes((MOD_ROWS, d), F32) + _nbytes((d, V7X_LANES), F32) \
        + 2 * _nbytes((tm, V7X_LANES), F32)
    return pl.pallas_call(
        _router_kernel,
        grid=(m // tm,),
        in_specs=[pl.BlockSpec((tm, d), lambda i: (i, 0)),
                  pl.BlockSpec((1, d), lambda i: (0, 0)),
                  pl.BlockSpec((MOD_ROWS, d), lambda i: (0, shift_chunk)),
                  pl.BlockSpec((MOD_ROWS, d), lambda i: (0, scale_chunk)),
                  pl.BlockSpec((d, V7X_LANES), lambda i: (0, 0)),
                  pl.BlockSpec((1, V7X_LANES), lambda i: (0, 0))],
        out_specs=[pl.BlockSpec((tm, d // 2), lambda i: (i, 0)),
                   pl.BlockSpec((tm, V7X_LANES), lambda i: (i, 0)),
                   pl.BlockSpec((tm, V7X_LANES), lambda i: (i, 0))],
        out_shape=[jax.ShapeDtypeStruct((m, d // 2), jnp.uint32),
                   jax.ShapeDtypeStruct((m, V7X_LANES), jnp.int32),
                   jax.ShapeDtypeStruct((m, V7X_LANES), F32)],
        compiler_params=pltpu.CompilerParams(
            dimension_semantics=("arbitrary",),
            vmem_limit_bytes=_vmem_limit(blocks, 0)),
        name="router",
    )(x, g, mod, mod, wr, br)


def _row_copy(src, src_row, dst, dst_row, sem):
    return pltpu.make_async_copy(src.at[pl.ds(src_row, 1), :], dst.at[pl.ds(dst_row, 1), :], sem)


def _for_rows(n_rows, fn):
    n_groups = lax.shift_right_logical(n_rows, DMA_UNROLL.bit_length() - 1)

    def group(gi, carry):
        for u in range(DMA_UNROLL):
            fn(gi * DMA_UNROLL + u, u)
        return carry

    def single(r, carry):
        fn(r, 0)
        return carry

    lax.fori_loop(0, n_groups, group, 0)
    lax.fori_loop(n_groups * DMA_UNROLL, n_rows, single, 0)


def _experts_kernel(be_ref, nact_ref, nv_ref, cs_ref, stok_ref, sdst_ref,
                    h_hbm, wg_hbm, wu_hbm, wd_hbm, y_hbm,
                    wg_ref, wu_ref, wd_ref, x_ref, acc_ref, gate_ref, hid_ref, sem_w, sem_in, sem_out):
    n_active = nact_ref[0]
    weights = ((wg_hbm, wg_ref), (wu_hbm, wu_ref), (wd_hbm, wd_ref))

    def weight_copy(k, bb):
        src, dst = weights[k]
        return pltpu.make_async_copy(src.at[be_ref[bb]], dst, sem_w.at[k])

    def wait_rows(src, dst, n, sem):
        n_whole = pl.multiple_of(lax.shift_right_logical(n, DMA_UNROLL.bit_length() - 1) * DMA_UNROLL, DMA_UNROLL)

        @pl.when(n_whole > 0)
        def _():
            rows = pl.ds(0, n_whole)
            pltpu.make_async_copy(src.at[rows, :], dst.at[rows, :], sem).wait()

        def single(r, carry):
            _row_copy(src, r, dst, r, sem).wait()
            return carry

        lax.fori_loop(n_whole, n, single, 0)

    def gather(bb, slot_, wait):
        def one(r, u):
            _row_copy(h_hbm, stok_ref[cs_ref[bb] + r], x_ref.at[slot_], r,
                      sem_in.at[slot_]).start(priority=u % N_DMA_PRIORITIES)

        if wait:
            wait_rows(h_hbm, x_ref.at[slot_], nv_ref[bb], sem_in.at[slot_])
        else:
            _for_rows(nv_ref[bb], one)

    def scatter(bb, wait):
        par = lax.rem(bb, 2)

        def one(r, u):
            _row_copy(acc_ref.at[par], r, y_hbm, sdst_ref[cs_ref[bb] + r],
                      sem_out.at[par]).start(priority=u % N_DMA_PRIORITIES)

        if wait:
            wait_rows(acc_ref.at[par], y_hbm, nv_ref[bb], sem_out.at[par])
        else:
            _for_rows(nv_ref[bb], one)

    half = x_ref.shape[-1]
    sub_starts = [sum(EXPERT_SUBS[:j]) for j in range(len(EXPERT_SUBS))]

    def for_sub_blocks(bb, fn):
        fn(pl.ds(sub_starts[0], EXPERT_SUBS[0]))
        for start, size in zip(sub_starts[1:], EXPERT_SUBS[1:]):
            @pl.when(start < nv_ref[bb])
            def _():
                fn(pl.ds(start, size))

    def x_times(slot, rows, w_ref):
        xw = x_ref[slot, rows, :]
        xa = lax.bitcast_convert_type(xw << 16, F32).astype(BF16)
        xb = lax.bitcast_convert_type(xw & jnp.uint32(0xFFFF0000), F32).astype(BF16)
        return (jnp.dot(xa, w_ref[:half, :].astype(BF16), preferred_element_type=F32)
                + jnp.dot(xb, w_ref[half:, :].astype(BF16), preferred_element_type=F32))

    def block(b, carry):
        slot = lax.rem(b, 2)
        has_next = b + 1 < n_active

        def refill(k):
            @pl.when(has_next)
            def _():
                weight_copy(k, b + 1).start()

        @pl.when(has_next)
        def _():
            gather(b + 1, 1 - slot, False)

        gather(b, slot, True)

        weight_copy(0, b).wait()

        def gate_phase(rows):
            gate_ref[rows, :] = x_times(slot, rows, wg_ref)

        for_sub_blocks(b, gate_phase)
        refill(0)

        weight_copy(1, b).wait()

        def up_phase(rows):
            gate = gate_ref[rows, :]
            hid_ref[rows, :] = (gate * jax.nn.sigmoid(gate) * x_times(slot, rows, wu_ref)).astype(BF16)

        for_sub_blocks(b, up_phase)
        refill(1)

        weight_copy(2, b).wait()

        def down_phase(rows):
            acc_ref[slot, rows, :] = jnp.dot(hid_ref[rows, :], wd_ref[...].astype(BF16),
                                             preferred_element_type=F32)

        for_sub_blocks(b, down_phase)
        refill(2)

        @pl.when(b > 0)
        def _():
            scatter(b - 1, True)

        scatter(b, False)
        return carry

    x_ref[...] = jnp.zeros(x_ref.shape, x_ref.dtype)
    gather(0, 0, False)
    for k in range(len(weights)):
        weight_copy(k, 0).start()
    lax.fori_loop(0, n_active, block, 0)
    scatter(n_active - 1, True)


def _experts(h, w_gate, w_up, w_down, plan, n_out_rows):
    block_e, n_active, n_valid, c_start, s_tok, s_dst = plan
    d = w_gate.shape[1]
    de = w_gate.shape[-1]
    assert h.shape[1] * 2 == d and sum(EXPERT_SUBS) == EXPERT_ROWS

    any_space = pl.BlockSpec(memory_space=pl.ANY)
    scratch = (3 * _nbytes((d, de), F32) + _nbytes((2, EXPERT_ROWS, d // 2), jnp.uint32)
               + _nbytes((2, EXPERT_ROWS, d), F32) + _nbytes((EXPERT_ROWS, de), F32) + _nbytes((EXPERT_ROWS, de), BF16))
    grid_spec = pltpu.PrefetchScalarGridSpec(
        num_scalar_prefetch=6,
        grid=(1,),
        in_specs=[any_space, any_space, any_space, any_space],
        out_specs=any_space,
        scratch_shapes=[pltpu.VMEM((d, de), F32), pltpu.VMEM((d, de), F32), pltpu.VMEM((de, d), F32),
                        pltpu.VMEM((2, EXPERT_ROWS, d // 2), jnp.uint32),
                        pltpu.VMEM((2, EXPERT_ROWS, d), F32),
                        pltpu.VMEM((EXPERT_ROWS, de), F32), pltpu.VMEM((EXPERT_ROWS, de), BF16),
                        pltpu.SemaphoreType.DMA((3,)), pltpu.SemaphoreType.DMA((2,)),
                        pltpu.SemaphoreType.DMA((2,))],
    )
    return pl.pallas_call(
        _experts_kernel,
        grid_spec=grid_spec,
        out_shape=jax.ShapeDtypeStruct((n_out_rows, d), F32),
        compiler_params=pltpu.CompilerParams(
            dimension_semantics=("arbitrary",),
            vmem_limit_bytes=_vmem_limit(0, scratch)),
        name="experts",
    )(block_e, n_active, n_valid, c_start, s_tok, s_dst, h, w_gate, w_up, w_down)


def _combine_kernel(*refs, top_k):
    y_refs = refs[:top_k]
    w_ref, x_ref, g_ref, fg_ref, o_ref = refs[top_k:]
    y = y_refs[0][0] * w_ref[:, 0:1]
    for k in range(1, top_k):
        y = y + y_refs[k][0] * w_ref[:, k:k + 1]
    x = x_ref[...] + g_ref[0:1, :] * y
    o_ref[...] = x * lax.rsqrt(jnp.mean(x * x, axis=-1, keepdims=True) + EPS) * fg_ref[...]


def _combine(y, ew, x, mod, gate_chunk, final_g, top_k):
    m, d = x.shape
    tm = ROW_TILE
    blocks = (top_k + 2) * _nbytes((tm, d), F32) + _nbytes((MOD_ROWS, d), F32) + _nbytes((tm, V7X_LANES), F32)
    in_specs = [pl.BlockSpec((1, tm, d), functools.partial(lambda k, i: (k, i, 0), k)) for k in range(top_k)]
    in_specs += [pl.BlockSpec((tm, V7X_LANES), lambda i: (i, 0)),
                 pl.BlockSpec((tm, d), lambda i: (i, 0)),
                 pl.BlockSpec((MOD_ROWS, d), lambda i: (0, gate_chunk)),
                 pl.BlockSpec((1, d), lambda i: (0, 0))]
    return pl.pallas_call(
        functools.partial(_combine_kernel, top_k=top_k),
        grid=(m // tm,),
        in_specs=in_specs,
        out_specs=pl.BlockSpec((tm, d), lambda i: (i, 0)),
        out_shape=jax.ShapeDtypeStruct((m, d), F32),
        compiler_params=pltpu.CompilerParams(
            dimension_semantics=("arbitrary",),
            vmem_limit_bytes=_vmem_limit(blocks, 0)),
        name="combine",
    )(*([y] * top_k), ew, x, mod, final_g)


def _rope_tables(n_tokens):
    n_rows = n_tokens // GRID_W
    n_freq = RET_DK // 4
    freqs = ROPE_BASE ** (-jnp.arange(n_freq, dtype=F32) / n_freq)
    sign = jnp.concatenate([-jnp.ones((n_freq,), F32), jnp.ones((n_freq,), F32)])

    def half_tables(n_pos):
        ang = jnp.arange(n_pos, dtype=F32)[:, None] * freqs
        return jnp.tile(jnp.cos(ang), (1, 2)), jnp.tile(jnp.sin(ang), (1, 2)) * sign

    cos_r, sin_r = half_tables(n_rows)
    cos_c, sin_c = half_tables(GRID_W)

    def expand(by_row, by_col):
        by_row = jnp.broadcast_to(by_row[:, None, :], (n_rows, GRID_W, 2 * n_freq))
        by_col = jnp.broadcast_to(by_col[None, :, :], (n_rows, GRID_W, 2 * n_freq))
        return jnp.concatenate([by_row, by_col], axis=-1).reshape(n_tokens, RET_DK)

    return expand(cos_r, cos_c), expand(sin_r, sin_c)


def _dispatch(eid, n_tok, top_k):
    m = n_tok * top_k
    n_blocks = -(-m // EXPERT_ROWS) + N_EXPERTS
    e_flat = eid[:, :top_k].reshape(-1)
    order = jnp.argsort(e_flat).astype(jnp.int32)
    s_tok = order // top_k
    s_dst = (order % top_k) * n_tok + s_tok
    counts = jnp.sum((e_flat[:, None] == jnp.arange(N_EXPERTS, dtype=jnp.int32)[None, :]).astype(jnp.int32), axis=0)
    starts = jnp.cumsum(counts) - counts
    e_blocks = (counts + EXPERT_ROWS - 1) // EXPERT_ROWS
    b_ends = jnp.cumsum(e_blocks)
    n_active = b_ends[-1].astype(jnp.int32)
    blk = jnp.arange(n_blocks, dtype=jnp.int32)
    block_e = jnp.sum((jnp.minimum(blk, n_active - 1)[:, None] >= b_ends[None, :]).astype(jnp.int32), axis=1)
    block_e = jnp.minimum(block_e, N_EXPERTS - 1)
    own = (block_e[:, None] == jnp.arange(N_EXPERTS, dtype=jnp.int32)[None, :]).astype(jnp.int32)

    def of_block(per_expert):
        return jnp.sum(own * per_expert[None, :], axis=1)

    within = blk - of_block(b_ends - e_blocks)
    c_start = of_block(starts) + within * EXPERT_ROWS
    n_valid = jnp.where(blk < n_active, jnp.clip(of_block(counts) - within * EXPERT_ROWS, 0, EXPERT_ROWS), 0)
    return block_e, n_active.reshape(1), n_valid, c_start, s_tok, s_dst


def kernel(x, c, ctx, c_ctx, w_ada, b_ada, norm1_g, norm2_g, w_in, ret_decay_f, ret_decay_b, ret_gn_g, sgu_ln_g, sgu_ln_b, sgu_w_s, sgu_b_s, w_out, w_router_group, b_router_group, w_router_expert, b_router_expert, w_gate, w_up, w_down, final_g):
    batch, n_tok, d = x.shape
    assert batch == 1 and w_ada.shape[0] == 1
    n_heads = ret_decay_f.shape[-1]
    n_groups = sgu_w_s.shape[1]
    ret_qk_w = n_heads * RET_DK
    ret_w = n_heads * RET_DV
    sgu_w = n_groups * SGU_GROUP_DIM
    k_off = ret_qk_w
    u_off = 2 * ret_qk_w + 2 * ret_w
    in_w = u_off + 2 * sgu_w
    top_k = 2
    assert w_in.shape == (1, d, in_w) and w_out.shape == (1, ret_w + sgu_w, d)

    cc = jnp.concatenate([c, c_ctx[None, :], jnp.zeros((MOD_ROWS - 2, d), F32)], axis=0)
    mod, h1 = _ada_norm(cc, w_ada[0], b_ada[0].reshape(1, N_MOD * d), x[0], norm1_g)

    lg_f = -jnp.exp(ret_decay_f[0])
    lg_b = -jnp.exp(ret_decay_b[0])

    hc = _norm_mod(ctx[0], norm1_g, mod, 1, 0, 1, ROW_TILE)
    kv_c = _matmul([hc], w_in[0], ret_qk_w + ret_w, k_off, ctx.shape[1], MM_TN)
    s_f, s_b = _ctx_state(kv_c, lg_f, lg_b, n_heads)

    cos, sin_signed = _rope_tables(n_tok)
    p = _in_proj(h1, w_in[0], cos, sin_signed, sgu_ln_g, sgu_ln_b, ret_qk_w, 2 * ret_w, sgu_w, IN_TM, IN_TN)
    ret_out, sgu_out = _mixer(p, lg_f, lg_b, s_f, s_b, ret_gn_g, sgu_w_s[0],
                              sgu_b_s[0].reshape(n_groups, SGU_CHUNK, 1), n_heads, u_off)
    x1 = _matmul([ret_out, sgu_out], w_out[0], d, 0, IN_TM, IN_TN, residual=(x[0], mod, 2))

    n_router = MOE_GROUPS + N_EXPERTS
    wr = jnp.concatenate([w_router_group[0], w_router_expert[0],
                          jnp.zeros((d, V7X_LANES - n_router), F32)], axis=1)
    br = jnp.concatenate([b_router_group, b_router_expert,
                          jnp.zeros((1, V7X_LANES - n_router), F32)], axis=1)
    h2, eid, ew = _router(x1, norm2_g, mod, 3, 4, wr, br)
    plan = _dispatch(eid, n_tok, top_k)
    y = _experts(h2, w_gate[0], w_up[0], w_down[0], plan, top_k * n_tok)
    out = _combine(y.reshape(top_k, n_tok, d), ew, x1, mod, 5, final_g.reshape(1, d), top_k)
    return out.reshape(batch, n_tok, d)
```

```python
import functools

import jax
import jax.numpy as jnp
from jax import lax
from jax.experimental import pallas as pl
from jax.experimental.pallas import tpu as pltpu

F32 = jnp.float32
BF16 = jnp.bfloat16

GRID_W = 64
RET_DK = 256
RET_DV = 256
ROPE_BASE = 10000.0
SGU_GROUP_DIM = 256
SGU_CHUNK = 128
MOE_GROUPS = 8
EXPERTS_PER_GROUP = 8
N_EXPERTS = MOE_GROUPS * EXPERTS_PER_GROUP
N_MOD = 6
EPS = 1e-6

V7X_LANES = 128
V7X_VMEM_BYTES = 64 * 1024 * 1024
MOD_ROWS = 8

ADA_TN = 512
MM_TN = 512
IN_TM = 512
IN_TN = 1024
ROW_TILE = 512
COMBINE_TILE = 256
RET_TILE = 2048
RET_SUB = 256
EXPERT_ROWS = 512
EXPERT_SUBS = (320, 192)
DMA_UNROLL = 8
N_DMA_PRIORITIES = 2


def _vmem_limit(block_bytes, scratch_bytes):
    want = 2 * block_bytes + scratch_bytes
    return int(min(V7X_VMEM_BYTES - 4 * 1024 * 1024, max(2 * want, 32 * 1024 * 1024)))


def _nbytes(shape, dtype):
    n = 1
    for s in shape:
        n *= s
    return n * jnp.dtype(dtype).itemsize


def _rms_mod(x, g, shift, scale):
    y = x * lax.rsqrt(jnp.mean(x * x, axis=-1, keepdims=True) + EPS) * g
    return y * (1.0 + scale) + shift


def _ada_norm_kernel(cc_ref, w_ref, b_ref, x_ref, g_ref, mod_ref, h_ref, lead_ref, *, n_lead):
    j = pl.program_id(0)
    a = cc_ref[...]
    s = (a * jax.nn.sigmoid(a)).astype(BF16)
    m = jnp.dot(s, w_ref[...].astype(BF16), preferred_element_type=F32) + b_ref[...]
    mod_ref[...] = m

    @pl.when(j < n_lead)
    def _():
        lead_ref[j] = m

    @pl.when(j >= n_lead)
    def _():
        half = n_lead // 2
        shift = jnp.concatenate([lead_ref[t][0:1, :] for t in range(half)], axis=1)
        scale = jnp.concatenate([lead_ref[t][0:1, :] for t in range(half, n_lead)], axis=1)
        h_ref[...] = _rms_mod(x_ref[...], g_ref[...], shift, scale).astype(h_ref.dtype)


def _ada_norm(cc, w, b, x, g):
    d, n = w.shape
    m_rows = x.shape[0]
    n_tiles = n // ADA_TN
    n_lead = 2 * d // ADA_TN
    assert m_rows % (n_tiles - n_lead) == 0
    tm = m_rows // (n_tiles - n_lead)
    assert tm % 16 == 0

    def row_idx(j):
        return (jnp.maximum(j - n_lead, 0), 0)

    blocks = (_nbytes((MOD_ROWS, d), F32) + _nbytes((d, ADA_TN), F32) + 2 * _nbytes((MOD_ROWS, ADA_TN), F32)
              + _nbytes((tm, d), F32) + _nbytes((tm, d), BF16))
    scratch = _nbytes((n_lead, MOD_ROWS, ADA_TN), F32) + _nbytes((d, ADA_TN), BF16)
    return pl.pallas_call(
        functools.partial(_ada_norm_kernel, n_lead=n_lead),
        grid=(n_tiles,),
        in_specs=[pl.BlockSpec((MOD_ROWS, d), lambda j: (0, 0)),
                  pl.BlockSpec((d, ADA_TN), lambda j: (0, j)),
                  pl.BlockSpec((1, ADA_TN), lambda j: (0, j)),
                  pl.BlockSpec((tm, d), row_idx),
                  pl.BlockSpec((1, d), lambda j: (0, 0))],
        out_specs=[pl.BlockSpec((MOD_ROWS, ADA_TN), lambda j: (0, j)),
                   pl.BlockSpec((tm, d), row_idx)],
        out_shape=[jax.ShapeDtypeStruct((MOD_ROWS, n), F32), jax.ShapeDtypeStruct((m_rows, d), BF16)],
        scratch_shapes=[pltpu.VMEM((n_lead, MOD_ROWS, ADA_TN), F32)],
        compiler_params=pltpu.CompilerParams(
            dimension_semantics=("arbitrary",),
            vmem_limit_bytes=_vmem_limit(blocks, scratch)),
        name="ada_norm",
    )(cc, w, b, x, g)


def _norm_mod_kernel(x_ref, g_ref, sh_ref, sc_ref, o_ref, *, row):
    h = _rms_mod(x_ref[...], g_ref[...], sh_ref[row:row + 1, :], sc_ref[row:row + 1, :])
    o_ref[...] = h.astype(o_ref.dtype)


def _norm_mod(x, g, mod, row, shift_chunk, scale_chunk, tm):
    m, d = x.shape
    blocks = _nbytes((tm, d), F32) * 2 + 3 * _nbytes((MOD_ROWS, d), F32)
    return pl.pallas_call(
        functools.partial(_norm_mod_kernel, row=row),
        grid=(m // tm,),
        in_specs=[pl.BlockSpec((tm, d), lambda i: (i, 0)),
                  pl.BlockSpec((1, d), lambda i: (0, 0)),
                  pl.BlockSpec((MOD_ROWS, d), lambda i: (0, shift_chunk)),
                  pl.BlockSpec((MOD_ROWS, d), lambda i: (0, scale_chunk))],
        out_specs=pl.BlockSpec((tm, d), lambda i: (i, 0)),
        out_shape=jax.ShapeDtypeStruct((m, d), BF16),
        compiler_params=pltpu.CompilerParams(
            dimension_semantics=("arbitrary",),
            vmem_limit_bytes=_vmem_limit(blocks, 0)),
        name="norm_mod",
    )(x, g, mod, mod)


def _matmul_acc(a_refs, w_ref, wbf_ref):
    @pl.when(pl.program_id(1) == 0)
    def _():
        wbf_ref[...] = w_ref[...].astype(BF16)

    acc = None
    k0 = 0
    for a_ref in a_refs:
        kk = a_ref.shape[1]
        part = jnp.dot(a_ref[...], wbf_ref[k0:k0 + kk, :], preferred_element_type=F32)
        acc = part if acc is None else acc + part
        k0 += kk
    return acc


def _matmul_kernel(*refs, n_a):
    w_ref, o_ref, wbf_ref = refs[n_a:]
    o_ref[...] = _matmul_acc(refs[:n_a], w_ref, wbf_ref).astype(o_ref.dtype)


def _matmul_res_kernel(*refs, n_a):
    w_ref, x_ref, g_ref, o_ref, wbf_ref = refs[n_a:]
    o_ref[...] = x_ref[...] + g_ref[0:1, :] * _matmul_acc(refs[:n_a], w_ref, wbf_ref)


def _matmul(a_list, w, n_cols, col_off, tm, tn, residual=None, out_dtype=F32):
    m = a_list[0].shape[0]
    k = sum(a.shape[1] for a in a_list)
    n_a = len(a_list)
    joff = col_off // tn
    grid = (n_cols // tn, m // tm)
    in_specs = [pl.BlockSpec((tm, a.shape[1]), lambda j, i: (i, 0)) for a in a_list]
    in_specs.append(pl.BlockSpec((k, tn), lambda j, i: (0, j + joff)))
    args = list(a_list) + [w]
    blocks = _nbytes((tm, k), BF16) + _nbytes((k, tn), F32) + _nbytes((tm, tn), F32)
    if residual is None:
        body = functools.partial(_matmul_kernel, n_a=n_a)
    else:
        x, mod, gate_chunk = residual
        goff = gate_chunk * (n_cols // tn)
        in_specs += [pl.BlockSpec((tm, tn), lambda j, i: (i, j)),
                     pl.BlockSpec((MOD_ROWS, tn), lambda j, i: (0, goff + j))]
        args += [x, mod]
        blocks += _nbytes((tm, tn), F32) + _nbytes((MOD_ROWS, tn), F32)
        body = functools.partial(_matmul_res_kernel, n_a=n_a)
    return pl.pallas_call(
        body,
        grid=grid,
        in_specs=in_specs,
        out_specs=pl.BlockSpec((tm, tn), lambda j, i: (i, j)),
        out_shape=jax.ShapeDtypeStruct((m, n_cols), out_dtype),
        scratch_shapes=[pltpu.VMEM((k, tn), BF16)],
        compiler_params=pltpu.CompilerParams(
            dimension_semantics=("arbitrary", "arbitrary"),
            vmem_limit_bytes=_vmem_limit(blocks, _nbytes((k, tn), BF16))),
        name="matmul_res" if residual is not None else "matmul",
    )(*args)


def _rope(t, cos, sin_signed):
    half = RET_DK // 2
    rot = jnp.concatenate([pltpu.roll(t[:, :half], half // 2, 1),
                           pltpu.roll(t[:, half:], half // 2, 1)], axis=1)
    return t * cos + rot * sin_signed


def _in_proj_kernel(a_ref, w_ref, cos_ref, sin_ref, lng_ref, lnb_ref, o_ref, wbf_ref, *, tile_ends):
    j = pl.program_id(0)
    q_end, k_end, plain_end, u_end = tile_ends

    def acc():
        return _matmul_acc([a_ref], w_ref, wbf_ref)

    @pl.when(j < k_end)
    def _rotated():
        t = acc() * jnp.where(j < q_end, 1.0, RET_DK ** -0.5)
        cos = cos_ref[...]
        sin = sin_ref[...]
        for hh in range(t.shape[1] // RET_DK):
            cols = slice(hh * RET_DK, (hh + 1) * RET_DK)
            o_ref[:, cols] = _rope(t[:, cols], cos, sin).astype(o_ref.dtype)

    @pl.when((j >= k_end) & (j < plain_end))
    def _plain():
        o_ref[...] = acc().astype(o_ref.dtype)

    @pl.when((j >= plain_end) & (j < u_end))
    def _gelu():
        o_ref[...] = jax.nn.gelu(acc()).astype(o_ref.dtype)

    @pl.when(j >= u_end)
    def _gelu_norm():
        t = jax.nn.gelu(acc())
        for gg in range(t.shape[1] // SGU_GROUP_DIM):
            cols = slice(gg * SGU_GROUP_DIM, (gg + 1) * SGU_GROUP_DIM)
            v32 = t[:, cols]
            mu = jnp.mean(v32, axis=-1, keepdims=True)
            var = jnp.mean(jnp.square(v32 - mu), axis=-1, keepdims=True)
            vn = (v32 - mu) * lax.rsqrt(var + EPS) * lng_ref[:, cols] + lnb_ref[:, cols]
            o_ref[:, cols] = vn.astype(o_ref.dtype)


def _in_proj(a, w, cos, sin_signed, ln_g, ln_b, qk_w, v_gate_w, sgu_w, tm, tn):
    m, k = a.shape
    n = w.shape[1]
    assert n == 2 * qk_w + v_gate_w + 2 * sgu_w
    assert qk_w % tn == 0 and v_gate_w % tn == 0 and sgu_w % tn == 0
    assert tn % RET_DK == 0 and tn % SGU_GROUP_DIM == 0
    q_end = qk_w // tn
    k_end = 2 * q_end
    plain_end = k_end + v_gate_w // tn
    u_end = plain_end + sgu_w // tn

    def table_idx(j, i):
        return (jnp.where(j < k_end, i, 0), 0)

    def ln_idx(j, i):
        return (0, jnp.maximum(j - u_end, 0))

    blocks = (_nbytes((tm, k), BF16) + _nbytes((k, tn), F32) + _nbytes((tm, tn), BF16)
              + 2 * _nbytes((tm, RET_DK), F32))
    return pl.pallas_call(
        functools.partial(_in_proj_kernel, tile_ends=(q_end, k_end, plain_end, u_end)),
        grid=(n // tn, m // tm),
        in_specs=[pl.BlockSpec((tm, k), lambda j, i: (i, 0)),
                  pl.BlockSpec((k, tn), lambda j, i: (0, j)),
                  pl.BlockSpec((tm, RET_DK), table_idx),
                  pl.BlockSpec((tm, RET_DK), table_idx),
                  pl.BlockSpec((1, tn), ln_idx),
                  pl.BlockSpec((1, tn), ln_idx)],
        out_specs=pl.BlockSpec((tm, tn), lambda j, i: (i, j)),
        out_shape=jax.ShapeDtypeStruct((m, n), BF16),
        scratch_shapes=[pltpu.VMEM((k, tn), BF16)],
        compiler_params=pltpu.CompilerParams(
            dimension_semantics=("arbitrary", "arbitrary"),
            vmem_limit_bytes=_vmem_limit(blocks, _nbytes((k, tn), BF16))),
        name="in_proj",
    )(a, w, cos, sin_signed, ln_g, ln_b)


def _ctx_state_kernel(lgf_ref, lgb_ref, k_ref, v_ref, sf_ref, sb_ref):
    h = pl.program_id(0)
    n = k_ref.shape[0]
    pos = lax.broadcasted_iota(jnp.int32, (n, 1), 0).astype(F32)
    k = k_ref[...] * (RET_DK ** -0.5)
    v = v_ref[...].astype(BF16)
    wf = jnp.exp((n - 1.0 - pos) * lgf_ref[h])
    wb = jnp.exp(pos * lgb_ref[h])
    tn_dims = (((0,), (0,)), ((), ()))
    sf_ref[0] = lax.dot_general((k * wf).astype(BF16), v, tn_dims, preferred_element_type=F32)
    sb_ref[0] = lax.dot_general((k * wb).astype(BF16), v, tn_dims, preferred_element_type=F32)


def _ctx_state(kv, lg_f, lg_b, n_heads):
    n = kv.shape[0]
    smem = pl.BlockSpec(memory_space=pltpu.SMEM)
    st = jax.ShapeDtypeStruct((n_heads, RET_DK, RET_DV), F32)
    return pl.pallas_call(
        _ctx_state_kernel,
        grid=(n_heads,),
        in_specs=[smem, smem,
                  pl.BlockSpec((n, RET_DK), lambda h: (0, h)),
                  pl.BlockSpec((n, RET_DV), lambda h: (0, n_heads + h))],
        out_specs=[pl.BlockSpec((1, RET_DK, RET_DV), lambda h: (h, 0, 0)),
                   pl.BlockSpec((1, RET_DK, RET_DV), lambda h: (h, 0, 0))],
        out_shape=[st, st],
        compiler_params=pltpu.CompilerParams(dimension_semantics=("arbitrary",)),
        name="ctx_state",
    )(lg_f, lg_b, kv, kv)


def _retention_kernel(lgf_ref, lgb_ref, q_ref, k_ref, v_ref, g_ref, s0f_ref, s0b_ref, gn_ref,
                      u_ref, vn_ref, ws_ref, bs_ref, o_ref, o2_ref, state_ref, yb_ref, *, n_steps):
    h = pl.program_id(0)
    p = pl.program_id(1)
    s = pl.program_id(2)
    c = RET_SUB
    n_sub = q_ref.shape[0] // c
    ii = lax.broadcasted_iota(jnp.int32, (c, c), 0)
    jj = lax.broadcasted_iota(jnp.int32, (c, c), 1)
    idx = lax.broadcasted_iota(jnp.int32, (c, 1), 0).astype(F32)
    nt_dims = (((1,), (1,)), ((), ()))
    tn_dims = (((0,), (0,)), ((), ()))

    def chunk(ci, decay, q_decay, k_decay, chunk_decay):
        rows = pl.ds(ci * c, c)
        qb = q_ref[rows, :]
        kb = k_ref[rows, :]
        v = v_ref[rows, :]
        st = state_ref[...]
        scores = lax.dot_general(qb, kb, nt_dims, preferred_element_type=F32) * decay
        out = (jnp.dot(scores.astype(BF16), v, preferred_element_type=F32)
               + jnp.dot(qb, st.astype(BF16), preferred_element_type=F32) * q_decay)
        state_ref[...] = st * chunk_decay + lax.dot_general(
            (kb.astype(F32) * k_decay).astype(BF16), v, tn_dims, preferred_element_type=F32)
        return out

    @pl.when(p == 0)
    def _backward():
        lg = lgb_ref[h]
        blk = n_steps - 1 - s

        @pl.when(s == 0)
        def _():
            state_ref[...] = s0b_ref[0]

        mask = jj > ii
        decay = jnp.where(mask, jnp.exp(jnp.where(mask, jj - ii, 0).astype(F32) * lg), 0.0)
        q_decay = jnp.exp((c - idx) * lg)
        k_decay = jnp.exp(idx * lg)
        chunk_decay = jnp.exp(jnp.full((1, RET_DV), c, F32) * lg)
        for ci in reversed(range(n_sub)):
            out = chunk(ci, decay, q_decay, k_decay, chunk_decay)
            start = pl.multiple_of(blk * (n_sub * c) + ci * c, c)
            yb_ref[pl.ds(start, c), :] = out

    @pl.when(p == 1)
    def _forward():
        lg = lgf_ref[h]

        @pl.when(s == 0)
        def _():
            state_ref[...] = s0f_ref[0]

        mask = ii >= jj
        decay = jnp.where(mask, jnp.exp(jnp.where(mask, ii - jj, 0).astype(F32) * lg), 0.0)
        q_decay = jnp.exp((idx + 1.0) * lg)
        k_decay = jnp.exp((c - 1.0 - idx) * lg)
        chunk_decay = jnp.exp(jnp.full((1, RET_DV), c, F32) * lg)
        for ci in range(n_sub):
            out = chunk(ci, decay, q_decay, k_decay, chunk_decay)
            start = pl.multiple_of(s * (n_sub * c) + ci * c, c)
            y = out + yb_ref[pl.ds(start, c), :]
            mu = jnp.mean(y, axis=-1, keepdims=True)
            var = jnp.mean(jnp.square(y - mu), axis=-1, keepdims=True)
            yn = (y - mu) * lax.rsqrt(var + EPS) * gn_ref[...]
            gate = g_ref[pl.ds(ci * c, c), :].astype(F32)
            o_ref[pl.ds(ci * c, c), :] = (gate * jax.nn.sigmoid(gate) * yn).astype(o_ref.dtype)

        ws = ws_ref[0].astype(BF16)
        bs = bs_ref[0]
        for ci in range(u_ref.shape[0] // SGU_CHUNK):
            rows = pl.ds(ci * SGU_CHUNK, SGU_CHUNK)
            mixed = jnp.dot(ws, vn_ref[rows, :], preferred_element_type=F32) + bs
            o2_ref[rows, :] = (u_ref[rows, :].astype(F32) * mixed).astype(o2_ref.dtype)


def _mixer(p, lg_f, lg_b, s0_f, s0_b, gn_g, w_s, b_s, n_heads, u_off):
    n_tok = p.shape[0]
    tl = RET_TILE
    n_steps = n_tok // tl
    ub = u_off // SGU_GROUP_DIM
    assert w_s.shape[0] == n_heads and RET_DV == SGU_GROUP_DIM and tl % RET_SUB == 0 and tl % SGU_CHUNK == 0
    smem = pl.BlockSpec(memory_space=pltpu.SMEM)

    def seq_blk(p_, s_):
        return jnp.where(p_ == 0, n_steps - 1 - s_, s_)

    def fwd_blk(col):
        return pl.BlockSpec((tl, RET_DV), lambda h, p_, s_: (s_ * p_, col + h))

    blocks = 8 * _nbytes((tl, RET_DK), BF16) + 2 * _nbytes((RET_DK, RET_DV), F32)
    scratch = _nbytes((RET_DK, RET_DV), F32) + _nbytes((n_tok, RET_DV), F32)
    out = jax.ShapeDtypeStruct((n_tok, n_heads * RET_DV), BF16)
    return pl.pallas_call(
        functools.partial(_retention_kernel, n_steps=n_steps),
        grid=(n_heads, 2, n_steps),
        in_specs=[smem, smem,
                  pl.BlockSpec((tl, RET_DK), lambda h, p_, s_: (seq_blk(p_, s_), h)),
                  pl.BlockSpec((tl, RET_DK), lambda h, p_, s_: (seq_blk(p_, s_), n_heads + h)),
                  pl.BlockSpec((tl, RET_DV), lambda h, p_, s_: (seq_blk(p_, s_), 2 * n_heads + h)),
                  fwd_blk(3 * n_heads),
                  pl.BlockSpec((1, RET_DK, RET_DV), lambda h, p_, s_: (h, 0, 0)),
                  pl.BlockSpec((1, RET_DK, RET_DV), lambda h, p_, s_: (h, 0, 0)),
                  pl.BlockSpec((1, RET_DV), lambda h, p_, s_: (0, h)),
                  fwd_blk(ub),
                  fwd_blk(ub + n_heads),
                  pl.BlockSpec((1, SGU_CHUNK, SGU_CHUNK), lambda h, p_, s_: (h, 0, 0)),
                  pl.BlockSpec((1, SGU_CHUNK, 1), lambda h, p_, s_: (h, 0, 0))],
        out_specs=[fwd_blk(0), fwd_blk(0)],
        out_shape=[out, out],
        scratch_shapes=[pltpu.VMEM((RET_DK, RET_DV), F32), pltpu.VMEM((n_tok, RET_DV), F32)],
        compiler_params=pltpu.CompilerParams(
            dimension_semantics=("arbitrary", "arbitrary", "arbitrary"),
            vmem_limit_bytes=_vmem_limit(blocks, scratch)),
        name="mixer",
    )(lg_f, lg_b, p, p, p, p, s0_f, s0_b, gn_g, p, p, w_s, b_s)


def _split_bf16(t):
    hi = t.astype(BF16)
    lo = (t - hi.astype(F32)).astype(BF16)
    return hi, lo


def _router_kernel(x_ref, g_ref, sh_ref, sc_ref, wr_ref, br_ref, h_ref, eid_ref, ew_ref):
    h = _rms_mod(x_ref[...], g_ref[...], sh_ref[0:1, :], sc_ref[0:1, :])
    half = h.shape[1] // 2
    bits = lax.bitcast_convert_type(h.astype(BF16).astype(F32), jnp.uint32)
    h_ref[...] = (bits[:, half:] & jnp.uint32(0xFFFF0000)) | (bits[:, :half] >> 16)
    h_hi, h_lo = _split_bf16(h)
    w_hi, w_lo = _split_bf16(wr_ref[...])
    logits = (jnp.dot(h_hi, w_hi, preferred_element_type=F32)
              + jnp.dot(h_lo, w_hi, preferred_element_type=F32)
              + jnp.dot(h_hi, w_lo, preferred_element_type=F32)) + br_ref[...]
    lane = lax.broadcasted_iota(jnp.int32, logits.shape, 1)
    lane_f = lane.astype(F32)
    neg = -jnp.inf

    def first_lane(hit):
        return jnp.min(jnp.where(hit, lane_f, float(V7X_LANES)), axis=-1, keepdims=True).astype(jnp.int32)

    gl = jnp.where(lane < MOE_GROUPS, logits, neg)
    g_max = jnp.max(gl, axis=-1, keepdims=True)
    g_sel = first_lane(gl == g_max)
    p_g = 1.0 / jnp.sum(jnp.exp(gl - g_max), axis=-1, keepdims=True)
    e_lo = MOE_GROUPS + g_sel * EXPERTS_PER_GROUP
    el = jnp.where((lane >= e_lo) & (lane < e_lo + EXPERTS_PER_GROUP), logits, neg)
    v1 = jnp.max(el, axis=-1, keepdims=True)
    i1 = first_lane(el == v1)
    el2 = jnp.where(lane == i1, neg, el)
    v2 = jnp.max(el2, axis=-1, keepdims=True)
    i2 = first_lane(el2 == v2)
    e2 = jnp.exp(v2 - v1)
    den = 1.0 + e2
    w1 = p_g * (1.0 / den)
    w2 = p_g * (e2 / den)
    eid_ref[...] = jnp.where(lane == 0, i1 - MOE_GROUPS, jnp.where(lane == 1, i2 - MOE_GROUPS, 0))
    ew_ref[...] = jnp.where(lane == 0, w1, jnp.where(lane == 1, w2, 0.0))


def _router(x, g, mod, shift_chunk, scale_chunk, wr, br):
    m, d = x.shape
    tm = ROW_TILE
    blocks = 2 * _nbytes((tm, d), F32) + 2 * _nbytes((MOD_ROWS, d), F32) + _nbytes((d, V7X_LANES), F32) \
        + 2 * _nbytes((tm, V7X_LANES), F32)
    return pl.pallas_call(
        _router_kernel,
        grid=(m // tm,),
        in_specs=[pl.BlockSpec((tm, d), lambda i: (i, 0)),
                  pl.BlockSpec((1, d), lambda i: (0, 0)),
                  pl.BlockSpec((MOD_ROWS, d), lambda i: (0, shift_chunk)),
                  pl.BlockSpec((MOD_ROWS, d), lambda i: (0, scale_chunk)),
                  pl.BlockSpec((d, V7X_LANES), lambda i: (0, 0)),
                  pl.BlockSpec((1, V7X_LANES), lambda i: (0, 0))],
        out_specs=[pl.BlockSpec((tm, d // 2), lambda i: (i, 0)),
                   pl.BlockSpec((tm, V7X_LANES), lambda i: (i, 0)),
                   pl.BlockSpec((tm, V7X_LANES), lambda i: (i, 0))],
        out_shape=[jax.ShapeDtypeStruct((m, d // 2), jnp.uint32),
                   jax.ShapeDtypeStruct((m, V7X_LANES), jnp.int32),
                   jax.ShapeDtypeStruct((m, V7X_LANES), F32)],
        compiler_params=pltpu.CompilerParams(
            dimension_semantics=("arbitrary",),
            vmem_limit_bytes=_vmem_limit(blocks, 0)),
        name="router",
    )(x, g, mod, mod, wr, br)


def _row_copy(src, src_row, dst, dst_row, sem):
    return pltpu.make_async_copy(src.at[pl.ds(src_row, 1), :], dst.at[pl.ds(dst_row, 1), :], sem)


def _for_rows(n_rows, fn):
    n_groups = lax.shift_right_logical(n_rows, DMA_UNROLL.bit_length() - 1)

    def group(gi, carry):
        for u in range(DMA_UNROLL):
            fn(gi * DMA_UNROLL + u, u)
        return carry

    def single(r, carry):
        fn(r, 0)
        return carry

    lax.fori_loop(0, n_groups, group, 0)
    lax.fori_loop(n_groups * DMA_UNROLL, n_rows, single, 0)


def _experts_kernel(be_ref, nact_ref, nv_ref, cs_ref, stok_ref, sdst_ref,
                    h_hbm, wg_hbm, wu_hbm, wd_hbm, y_hbm,
                    wg_ref, wu_ref, wd_ref, x_ref, acc_ref, gate_ref, hid_ref, sem_w, sem_in, sem_out):
    n_active = nact_ref[0]
    weights = ((wg_hbm, wg_ref), (wu_hbm, wu_ref), (wd_hbm, wd_ref))

    def weight_copy(k, bb):
        src, dst = weights[k]
        return pltpu.make_async_copy(src.at[be_ref[bb]], dst, sem_w.at[k])

    def wait_rows(src, dst, n, sem):
        n_whole = pl.multiple_of(lax.shift_right_logical(n, DMA_UNROLL.bit_length() - 1) * DMA_UNROLL, DMA_UNROLL)

        @pl.when(n_whole > 0)
        def _():
            rows = pl.ds(0, n_whole)
            pltpu.make_async_copy(src.at[rows, :], dst.at[rows, :], sem).wait()

        def single(r, carry):
            _row_copy(src, r, dst, r, sem).wait()
            return carry

        lax.fori_loop(n_whole, n, single, 0)

    def gather(bb, slot_, wait):
        def one(r, u):
            _row_copy(h_hbm, stok_ref[cs_ref[bb] + r], x_ref.at[slot_], r,
                      sem_in.at[slot_]).start(priority=u % N_DMA_PRIORITIES)

        if wait:
            wait_rows(h_hbm, x_ref.at[slot_], nv_ref[bb], sem_in.at[slot_])
        else:
            _for_rows(nv_ref[bb], one)

    def scatter(bb, wait):
        par = lax.rem(bb, 2)

        def one(r, u):
            _row_copy(acc_ref.at[par], r, y_hbm, sdst_ref[cs_ref[bb] + r],
                      sem_out.at[par]).start(priority=u % N_DMA_PRIORITIES)

        if wait:
            wait_rows(acc_ref.at[par], y_hbm, nv_ref[bb], sem_out.at[par])
        else:
            _for_rows(nv_ref[bb], one)

    half = x_ref.shape[-1]
    sub_starts = [sum(EXPERT_SUBS[:j]) for j in range(len(EXPERT_SUBS))]

    def for_sub_blocks(bb, fn):
        fn(pl.ds(sub_starts[0], EXPERT_SUBS[0]))
        for start, size in zip(sub_starts[1:], EXPERT_SUBS[1:]):
            @pl.when(start < nv_ref[bb])
            def _():
                fn(pl.ds(start, size))

    def x_times(slot, rows, w_ref):
        xw = x_ref[slot, rows, :]
        xa = lax.bitcast_convert_type(xw << 16, F32).astype(BF16)
        xb = lax.bitcast_convert_type(xw & jnp.uint32(0xFFFF0000), F32).astype(BF16)
        return (jnp.dot(xa, w_ref[:half, :].astype(BF16), preferred_element_type=F32)
                + jnp.dot(xb, w_ref[half:, :].astype(BF16), preferred_element_type=F32))

    def block(b, carry):
        slot = lax.rem(b, 2)
        has_next = b + 1 < n_active

        def refill(k):
            @pl.when(has_next)
            def _():
                weight_copy(k, b + 1).start()

        @pl.when(has_next)
        def _():
            gather(b + 1, 1 - slot, False)

        gather(b, slot, True)

        weight_copy(0, b).wait()

        def gate_phase(rows):
            gate_ref[rows, :] = x_times(slot, rows, wg_ref)

        for_sub_blocks(b, gate_phase)
        refill(0)

        weight_copy(1, b).wait()

        def up_phase(rows):
            gate = gate_ref[rows, :]
            hid_ref[rows, :] = (gate * jax.nn.sigmoid(gate) * x_times(slot, rows, wu_ref)).astype(BF16)

        for_sub_blocks(b, up_phase)
        refill(1)

        weight_copy(2, b).wait()

        def down_phase(rows):
            acc_ref[slot, rows, :] = jnp.dot(hid_ref[rows, :], wd_ref[...].astype(BF16),
                                             preferred_element_type=F32)

        for_sub_blocks(b, down_phase)
        refill(2)

        @pl.when(b > 0)
        def _():
            scatter(b - 1, True)

        scatter(b, False)
        return carry

    x_ref[...] = jnp.zeros(x_ref.shape, x_ref.dtype)
    gather(0, 0, False)
    for k in range(len(weights)):
        weight_copy(k, 0).start()
    lax.fori_loop(0, n_active, block, 0)
    scatter(n_active - 1, True)


def _experts(h, w_gate, w_up, w_down, plan, n_out_rows):
    block_e, n_active, n_valid, c_start, s_tok, s_dst = plan
    d = w_gate.shape[1]
    de = w_gate.shape[-1]
    assert h.shape[1] * 2 == d and sum(EXPERT_SUBS) == EXPERT_ROWS

    any_space = pl.BlockSpec(memory_space=pl.ANY)
    scratch = (3 * _nbytes((d, de), F32) + _nbytes((2, EXPERT_ROWS, d // 2), jnp.uint32)
               + _nbytes((2, EXPERT_ROWS, d), F32) + _nbytes((EXPERT_ROWS, de), F32) + _nbytes((EXPERT_ROWS, de), BF16))
    grid_spec = pltpu.PrefetchScalarGridSpec(
        num_scalar_prefetch=6,
        grid=(1,),
        in_specs=[any_space, any_space, any_space, any_space],
        out_specs=any_space,
        scratch_shapes=[pltpu.VMEM((d, de), F32), pltpu.VMEM((d, de), F32), pltpu.VMEM((de, d), F32),
                        pltpu.VMEM((2, EXPERT_ROWS, d // 2), jnp.uint32),
                        pltpu.VMEM((2, EXPERT_ROWS, d), F32),
                        pltpu.VMEM((EXPERT_ROWS, de), F32), pltpu.VMEM((EXPERT_ROWS, de), BF16),
                        pltpu.SemaphoreType.DMA((3,)), pltpu.SemaphoreType.DMA((2,)),
                        pltpu.SemaphoreType.DMA((2,))],
    )
    return pl.pallas_call(
        _experts_kernel,
        grid_spec=grid_spec,
        out_shape=jax.ShapeDtypeStruct((n_out_rows, d), F32),
        compiler_params=pltpu.CompilerParams(
            dimension_semantics=("arbitrary",),
            vmem_limit_bytes=_vmem_limit(0, scratch)),
        name="experts",
    )(block_e, n_active, n_valid, c_start, s_tok, s_dst, h, w_gate, w_up, w_down)


def _combine_kernel(*refs, top_k):
    y_refs = refs[:top_k]
    w_ref, x_ref, g_ref, fg_ref, o_ref = refs[top_k:]
    y = y_refs[0][0] * w_ref[:, 0:1]
    for k in range(1, top_k):
        y = y + y_refs[k][0] * w_ref[:, k:k + 1]
    x = x_ref[...] + g_ref[0:1, :] * y
    o_ref[...] = x * lax.rsqrt(jnp.mean(x * x, axis=-1, keepdims=True) + EPS) * fg_ref[...]


def _combine(y, ew, x, mod, gate_chunk, final_g, top_k):
    m, d = x.shape
    tm = COMBINE_TILE
    blocks = (top_k + 2) * _nbytes((tm, d), F32) + _nbytes((MOD_ROWS, d), F32) + _nbytes((tm, V7X_LANES), F32)
    in_specs = [pl.BlockSpec((1, tm, d), functools.partial(lambda k, i: (k, i, 0), k)) for k in range(top_k)]
    in_specs += [pl.BlockSpec((tm, V7X_LANES), lambda i: (i, 0)),
                 pl.BlockSpec((tm, d), lambda i: (i, 0)),
                 pl.BlockSpec((MOD_ROWS, d), lambda i: (0, gate_chunk)),
                 pl.BlockSpec((1, d), lambda i: (0, 0))]
    return pl.pallas_call(
        functools.partial(_combine_kernel, top_k=top_k),
        grid=(m // tm,),
        in_specs=in_specs,
        out_specs=pl.BlockSpec((tm, d), lambda i: (i, 0)),
        out_shape=jax.ShapeDtypeStruct((m, d), F32),
        compiler_params=pltpu.CompilerParams(
            dimension_semantics=("arbitrary",),
            vmem_limit_bytes=_vmem_limit(blocks, 0)),
        name="combine",
    )(*([y] * top_k), ew, x, mod, final_g)


def _rope_tables(n_tokens):
    n_rows = n_tokens // GRID_W
    n_freq = RET_DK // 4
    freqs = ROPE_BASE ** (-jnp.arange(n_freq, dtype=F32) / n_freq)
    sign = jnp.concatenate([-jnp.ones((n_freq,), F32), jnp.ones((n_freq,), F32)])

    def half_tables(n_pos):
        ang = jnp.arange(n_pos, dtype=F32)[:, None] * freqs
        return jnp.tile(jnp.cos(ang), (1, 2)), jnp.tile(jnp.sin(ang), (1, 2)) * sign

    cos_r, sin_r = half_tables(n_rows)
    cos_c, sin_c = half_tables(GRID_W)

    def expand(by_row, by_col):
        by_row = jnp.broadcast_to(by_row[:, None, :], (n_rows, GRID_W, 2 * n_freq))
        by_col = jnp.broadcast_to(by_col[None, :, :], (n_rows, GRID_W, 2 * n_freq))
        return jnp.concatenate([by_row, by_col], axis=-1).reshape(n_tokens, RET_DK)

    return expand(cos_r, cos_c), expand(sin_r, sin_c)


def _dispatch(eid, n_tok, top_k):
    m = n_tok * top_k
    n_blocks = -(-m // EXPERT_ROWS) + N_EXPERTS
    e_flat = eid[:, :top_k].reshape(-1)
    order = jnp.argsort(e_flat).astype(jnp.int32)
    s_tok = order // top_k
    s_dst = (order % top_k) * n_tok + s_tok
    counts = jnp.sum((e_flat[:, None] == jnp.arange(N_EXPERTS, dtype=jnp.int32)[None, :]).astype(jnp.int32), axis=0)
    starts = jnp.cumsum(counts) - counts
    e_blocks = (counts + EXPERT_ROWS - 1) // EXPERT_ROWS
    b_ends = jnp.cumsum(e_blocks)
    n_active = b_ends[-1].astype(jnp.int32)
    blk = jnp.arange(n_blocks, dtype=jnp.int32)
    block_e = jnp.sum((jnp.minimum(blk, n_active - 1)[:, None] >= b_ends[None, :]).astype(jnp.int32), axis=1)
    block_e = jnp.minimum(block_e, N_EXPERTS - 1)
    own = (block_e[:, None] == jnp.arange(N_EXPERTS, dtype=jnp.int32)[None, :]).astype(jnp.int32)

    def of_block(per_expert):
        return jnp.sum(own * per_expert[None, :], axis=1)

    within = blk - of_block(b_ends - e_blocks)
    c_start = of_block(starts) + within * EXPERT_ROWS
    n_valid = jnp.where(blk < n_active, jnp.clip(of_block(counts) - within * EXPERT_ROWS, 0, EXPERT_ROWS), 0)
    return block_e, n_active.reshape(1), n_valid, c_start, s_tok, s_dst


def kernel(x, c, ctx, c_ctx, w_ada, b_ada, norm1_g, norm2_g, w_in, ret_decay_f, ret_decay_b, ret_gn_g, sgu_ln_g, sgu_ln_b, sgu_w_s, sgu_b_s, w_out, w_router_group, b_router_group, w_router_expert, b_router_expert, w_gate, w_up, w_down, final_g):
    batch, n_tok, d = x.shape
    assert batch == 1 and w_ada.shape[0] == 1
    n_heads = ret_decay_f.shape[-1]
    n_groups = sgu_w_s.shape[1]
    ret_qk_w = n_heads * RET_DK
    ret_w = n_heads * RET_DV
    sgu_w = n_groups * SGU_GROUP_DIM
    k_off = ret_qk_w
    u_off = 2 * ret_qk_w + 2 * ret_w
    in_w = u_off + 2 * sgu_w
    top_k = 2
    assert w_in.shape == (1, d, in_w) and w_out.shape == (1, ret_w + sgu_w, d)

    cc = jnp.concatenate([c, c_ctx[None, :], jnp.zeros((MOD_ROWS - 2, d), F32)], axis=0)
    mod, h1 = _ada_norm(cc, w_ada[0], b_ada[0].reshape(1, N_MOD * d), x[0], norm1_g)

    lg_f = -jnp.exp(ret_decay_f[0])
    lg_b = -jnp.exp(ret_decay_b[0])

    hc = _norm_mod(ctx[0], norm1_g, mod, 1, 0, 1, ctx.shape[1])
    kv_c = _matmul([hc], w_in[0], ret_qk_w + ret_w, k_off, ctx.shape[1], MM_TN)
    s_f, s_b = _ctx_state(kv_c, lg_f, lg_b, n_heads)

    cos, sin_signed = _rope_tables(n_tok)
    p = _in_proj(h1, w_in[0], cos, sin_signed, sgu_ln_g, sgu_ln_b, ret_qk_w, 2 * ret_w, sgu_w, IN_TM, IN_TN)
    ret_out, sgu_out = _mixer(p, lg_f, lg_b, s_f, s_b, ret_gn_g, sgu_w_s[0],
                              sgu_b_s[0].reshape(n_groups, SGU_CHUNK, 1), n_heads, u_off)
    x1 = _matmul([ret_out, sgu_out], w_out[0], d, 0, IN_TM, IN_TN, residual=(x[0], mod, 2))

    n_router = MOE_GROUPS + N_EXPERTS
    wr = jnp.concatenate([w_router_group[0], w_router_expert[0],
                          jnp.zeros((d, V7X_LANES - n_router), F32)], axis=1)
    br = jnp.concatenate([b_router_group, b_router_expert,
                          jnp.zeros((1, V7X_LANES - n_router), F32)], axis=1)
    h2, eid, ew = _router(x1, norm2_g, mod, 3, 4, wr, br)
    plan = _dispatch(eid, n_tok, top_k)
    y = _experts(h2, w_gate[0], w_up[0], w_down[0], plan, top_k * n_tok)
    out = _combine(y.reshape(top_k, n_tok, d), ew, x1, mod, 5, final_g.reshape(1, d), top_k)
    return out.reshape(batch, n_tok, d)
```

```python
import functools

import jax
import jax.numpy as jnp
from jax import lax
from jax.experimental import pallas as pl
from jax.experimental.pallas import tpu as pltpu

F32 = jnp.float32
BF16 = jnp.bfloat16

GRID_W = 64
RET_DK = 256
RET_DV = 256
ROPE_BASE = 10000.0
SGU_GROUP_DIM = 256
SGU_CHUNK = 128
MOE_GROUPS = 8
EXPERTS_PER_GROUP = 8
N_EXPERTS = MOE_GROUPS * EXPERTS_PER_GROUP
N_MOD = 6
EPS = 1e-6

V7X_LANES = 128
V7X_VMEM_BYTES = 64 * 1024 * 1024
MOD_ROWS = 8

ADA_TN = 512
MM_TN = 512
IN_TM = 512
IN_TN = 1024
ROW_TILE = 512
COMBINE_TILE = 256
RET_TILE = 4096
RET_SUB = 256
EXPERT_ROWS = 512
EXPERT_SUBS = (320, 192)
DMA_UNROLL = 8
ROW_DMA_PRIORITY = 0
WEIGHT_DMA_PRIORITY = 1


def _vmem_limit(block_bytes, scratch_bytes):
    want = 2 * block_bytes + scratch_bytes
    return int(min(V7X_VMEM_BYTES - 4 * 1024 * 1024, max(2 * want, 32 * 1024 * 1024)))


def _nbytes(shape, dtype):
    n = 1
    for s in shape:
        n *= s
    return n * jnp.dtype(dtype).itemsize


def _rms_mod(x, g, shift, scale):
    y = x * lax.rsqrt(jnp.mean(x * x, axis=-1, keepdims=True) + EPS) * g
    return y * (1.0 + scale) + shift


def _ada_norm_kernel(cc_ref, w_ref, b_ref, x_ref, g_ref, mod_ref, h_ref, lead_ref, *, n_lead):
    j = pl.program_id(0)
    a = cc_ref[...]
    s = (a * jax.nn.sigmoid(a)).astype(BF16)
    m = jnp.dot(s, w_ref[...].astype(BF16), preferred_element_type=F32) + b_ref[...]
    mod_ref[...] = m

    @pl.when(j < n_lead)
    def _():
        lead_ref[j] = m

    @pl.when(j >= n_lead)
    def _():
        half = n_lead // 2
        shift = jnp.concatenate([lead_ref[t][0:1, :] for t in range(half)], axis=1)
        scale = jnp.concatenate([lead_ref[t][0:1, :] for t in range(half, n_lead)], axis=1)
        h_ref[...] = _rms_mod(x_ref[...], g_ref[...], shift, scale).astype(h_ref.dtype)


def _ada_norm(cc, w, b, x, g):
    d, n = w.shape
    m_rows = x.shape[0]
    n_tiles = n // ADA_TN
    n_lead = 2 * d // ADA_TN
    assert m_rows % (n_tiles - n_lead) == 0
    tm = m_rows // (n_tiles - n_lead)
    assert tm % 16 == 0

    def row_idx(j):
        return (jnp.maximum(j - n_lead, 0), 0)

    blocks = (_nbytes((MOD_ROWS, d), F32) + _nbytes((d, ADA_TN), F32) + 2 * _nbytes((MOD_ROWS, ADA_TN), F32)
              + _nbytes((tm, d), F32) + _nbytes((tm, d), BF16))
    scratch = _nbytes((n_lead, MOD_ROWS, ADA_TN), F32) + _nbytes((d, ADA_TN), BF16)
    return pl.pallas_call(
        functools.partial(_ada_norm_kernel, n_lead=n_lead),
        grid=(n_tiles,),
        in_specs=[pl.BlockSpec((MOD_ROWS, d), lambda j: (0, 0)),
                  pl.BlockSpec((d, ADA_TN), lambda j: (0, j)),
                  pl.BlockSpec((1, ADA_TN), lambda j: (0, j)),
                  pl.BlockSpec((tm, d), row_idx),
                  pl.BlockSpec((1, d), lambda j: (0, 0))],
        out_specs=[pl.BlockSpec((MOD_ROWS, ADA_TN), lambda j: (0, j)),
                   pl.BlockSpec((tm, d), row_idx)],
        out_shape=[jax.ShapeDtypeStruct((MOD_ROWS, n), F32), jax.ShapeDtypeStruct((m_rows, d), BF16)],
        scratch_shapes=[pltpu.VMEM((n_lead, MOD_ROWS, ADA_TN), F32)],
        compiler_params=pltpu.CompilerParams(
            dimension_semantics=("arbitrary",),
            vmem_limit_bytes=_vmem_limit(blocks, scratch)),
        name="ada_norm",
    )(cc, w, b, x, g)


def _norm_mod_kernel(x_ref, g_ref, sh_ref, sc_ref, o_ref, *, row):
    h = _rms_mod(x_ref[...], g_ref[...], sh_ref[row:row + 1, :], sc_ref[row:row + 1, :])
    o_ref[...] = h.astype(o_ref.dtype)


def _norm_mod(x, g, mod, row, shift_chunk, scale_chunk, tm):
    m, d = x.shape
    blocks = _nbytes((tm, d), F32) * 2 + 3 * _nbytes((MOD_ROWS, d), F32)
    return pl.pallas_call(
        functools.partial(_norm_mod_kernel, row=row),
        grid=(m // tm,),
        in_specs=[pl.BlockSpec((tm, d), lambda i: (i, 0)),
                  pl.BlockSpec((1, d), lambda i: (0, 0)),
                  pl.BlockSpec((MOD_ROWS, d), lambda i: (0, shift_chunk)),
                  pl.BlockSpec((MOD_ROWS, d), lambda i: (0, scale_chunk))],
        out_specs=pl.BlockSpec((tm, d), lambda i: (i, 0)),
        out_shape=jax.ShapeDtypeStruct((m, d), BF16),
        compiler_params=pltpu.CompilerParams(
            dimension_semantics=("arbitrary",),
            vmem_limit_bytes=_vmem_limit(blocks, 0)),
        name="norm_mod",
    )(x, g, mod, mod)


def _matmul_acc(a_refs, w_ref, wbf_ref):
    @pl.when(pl.program_id(1) == 0)
    def _():
        wbf_ref[...] = w_ref[...].astype(BF16)

    acc = None
    k0 = 0
    for a_ref in a_refs:
        kk = a_ref.shape[1]
        part = jnp.dot(a_ref[...], wbf_ref[k0:k0 + kk, :], preferred_element_type=F32)
        acc = part if acc is None else acc + part
        k0 += kk
    return acc


def _matmul_kernel(*refs, n_a):
    w_ref, o_ref, wbf_ref = refs[n_a:]
    o_ref[...] = _matmul_acc(refs[:n_a], w_ref, wbf_ref).astype(o_ref.dtype)


def _matmul_res_kernel(*refs, n_a):
    w_ref, x_ref, g_ref, o_ref, wbf_ref = refs[n_a:]
    o_ref[...] = x_ref[...] + g_ref[0:1, :] * _matmul_acc(refs[:n_a], w_ref, wbf_ref)


def _matmul(a_list, w, n_cols, col_off, tm, tn, residual=None, out_dtype=F32):
    m = a_list[0].shape[0]
    k = sum(a.shape[1] for a in a_list)
    n_a = len(a_list)
    joff = col_off // tn
    grid = (n_cols // tn, m // tm)
    in_specs = [pl.BlockSpec((tm, a.shape[1]), lambda j, i: (i, 0)) for a in a_list]
    in_specs.append(pl.BlockSpec((k, tn), lambda j, i: (0, j + joff)))
    args = list(a_list) + [w]
    blocks = _nbytes((tm, k), BF16) + _nbytes((k, tn), F32) + _nbytes((tm, tn), F32)
    if residual is None:
        body = functools.partial(_matmul_kernel, n_a=n_a)
    else:
        x, mod, gate_chunk = residual
        goff = gate_chunk * (n_cols // tn)
        in_specs += [pl.BlockSpec((tm, tn), lambda j, i: (i, j)),
                     pl.BlockSpec((MOD_ROWS, tn), lambda j, i: (0, goff + j))]
        args += [x, mod]
        blocks += _nbytes((tm, tn), F32) + _nbytes((MOD_ROWS, tn), F32)
        body = functools.partial(_matmul_res_kernel, n_a=n_a)
    return pl.pallas_call(
        body,
        grid=grid,
        in_specs=in_specs,
        out_specs=pl.BlockSpec((tm, tn), lambda j, i: (i, j)),
        out_shape=jax.ShapeDtypeStruct((m, n_cols), out_dtype),
        scratch_shapes=[pltpu.VMEM((k, tn), BF16)],
        compiler_params=pltpu.CompilerParams(
            dimension_semantics=("arbitrary", "arbitrary"),
            vmem_limit_bytes=_vmem_limit(blocks, _nbytes((k, tn), BF16))),
        name="matmul_res" if residual is not None else "matmul",
    )(*args)


def _rope(t, cos, sin_signed):
    half = RET_DK // 2
    rot = jnp.concatenate([pltpu.roll(t[:, :half], half // 2, 1),
                           pltpu.roll(t[:, half:], half // 2, 1)], axis=1)
    return t * cos + rot * sin_signed


def _in_proj_kernel(a_ref, w_ref, cos_ref, sin_ref, lng_ref, lnb_ref, o_ref, wbf_ref, *, tile_ends):
    j = pl.program_id(0)
    q_end, k_end, plain_end, u_end = tile_ends

    def acc():
        return _matmul_acc([a_ref], w_ref, wbf_ref)

    @pl.when(j < k_end)
    def _rotated():
        t = acc() * jnp.where(j < q_end, 1.0, RET_DK ** -0.5)
        cos = cos_ref[...]
        sin = sin_ref[...]
        for hh in range(t.shape[1] // RET_DK):
            cols = slice(hh * RET_DK, (hh + 1) * RET_DK)
            o_ref[:, cols] = _rope(t[:, cols], cos, sin).astype(o_ref.dtype)

    @pl.when((j >= k_end) & (j < plain_end))
    def _plain():
        o_ref[...] = acc().astype(o_ref.dtype)

    @pl.when((j >= plain_end) & (j < u_end))
    def _gelu():
        o_ref[...] = jax.nn.gelu(acc()).astype(o_ref.dtype)

    @pl.when(j >= u_end)
    def _gelu_norm():
        t = jax.nn.gelu(acc())
        for gg in range(t.shape[1] // SGU_GROUP_DIM):
            cols = slice(gg * SGU_GROUP_DIM, (gg + 1) * SGU_GROUP_DIM)
            v32 = t[:, cols]
            mu = jnp.mean(v32, axis=-1, keepdims=True)
            var = jnp.mean(jnp.square(v32 - mu), axis=-1, keepdims=True)
            vn = (v32 - mu) * lax.rsqrt(var + EPS) * lng_ref[:, cols] + lnb_ref[:, cols]
            o_ref[:, cols] = vn.astype(o_ref.dtype)


def _in_proj(a, w, cos, sin_signed, ln_g, ln_b, qk_w, v_gate_w, sgu_w, tm, tn):
    m, k = a.shape
    n = w.shape[1]
    assert n == 2 * qk_w + v_gate_w + 2 * sgu_w
    assert qk_w % tn == 0 and v_gate_w % tn == 0 and sgu_w % tn == 0
    assert tn % RET_DK == 0 and tn % SGU_GROUP_DIM == 0
    q_end = qk_w // tn
    k_end = 2 * q_end
    plain_end = k_end + v_gate_w // tn
    u_end = plain_end + sgu_w // tn

    def table_idx(j, i):
        return (jnp.where(j < k_end, i, 0), 0)

    def ln_idx(j, i):
        return (0, jnp.maximum(j - u_end, 0))

    blocks = (_nbytes((tm, k), BF16) + _nbytes((k, tn), F32) + _nbytes((tm, tn), BF16)
              + 2 * _nbytes((tm, RET_DK), F32))
    return pl.pallas_call(
        functools.partial(_in_proj_kernel, tile_ends=(q_end, k_end, plain_end, u_end)),
        grid=(n // tn, m // tm),
        in_specs=[pl.BlockSpec((tm, k), lambda j, i: (i, 0)),
                  pl.BlockSpec((k, tn), lambda j, i: (0, j)),
                  pl.BlockSpec((tm, RET_DK), table_idx),
                  pl.BlockSpec((tm, RET_DK), table_idx),
                  pl.BlockSpec((1, tn), ln_idx),
                  pl.BlockSpec((1, tn), ln_idx)],
        out_specs=pl.BlockSpec((tm, tn), lambda j, i: (i, j)),
        out_shape=jax.ShapeDtypeStruct((m, n), BF16),
        scratch_shapes=[pltpu.VMEM((k, tn), BF16)],
        compiler_params=pltpu.CompilerParams(
            dimension_semantics=("arbitrary", "arbitrary"),
            vmem_limit_bytes=_vmem_limit(blocks, _nbytes((k, tn), BF16))),
        name="in_proj",
    )(a, w, cos, sin_signed, ln_g, ln_b)


def _ctx_state_kernel(lgf_ref, lgb_ref, k_ref, v_ref, sf_ref, sb_ref):
    h = pl.program_id(0)
    n = k_ref.shape[0]
    pos = lax.broadcasted_iota(jnp.int32, (n, 1), 0).astype(F32)
    k = k_ref[...] * (RET_DK ** -0.5)
    v = v_ref[...].astype(BF16)
    wf = jnp.exp((n - 1.0 - pos) * lgf_ref[h])
    wb = jnp.exp(pos * lgb_ref[h])
    tn_dims = (((0,), (0,)), ((), ()))
    sf_ref[0] = lax.dot_general((k * wf).astype(BF16), v, tn_dims, preferred_element_type=F32)
    sb_ref[0] = lax.dot_general((k * wb).astype(BF16), v, tn_dims, preferred_element_type=F32)


def _ctx_state(kv, lg_f, lg_b, n_heads):
    n = kv.shape[0]
    smem = pl.BlockSpec(memory_space=pltpu.SMEM)
    st = jax.ShapeDtypeStruct((n_heads, RET_DK, RET_DV), F32)
    return pl.pallas_call(
        _ctx_state_kernel,
        grid=(n_heads,),
        in_specs=[smem, smem,
                  pl.BlockSpec((n, RET_DK), lambda h: (0, h)),
                  pl.BlockSpec((n, RET_DV), lambda h: (0, n_heads + h))],
        out_specs=[pl.BlockSpec((1, RET_DK, RET_DV), lambda h: (h, 0, 0)),
                   pl.BlockSpec((1, RET_DK, RET_DV), lambda h: (h, 0, 0))],
        out_shape=[st, st],
        compiler_params=pltpu.CompilerParams(dimension_semantics=("arbitrary",)),
        name="ctx_state",
    )(lg_f, lg_b, kv, kv)


def _retention_kernel(lgf_ref, lgb_ref, q_ref, k_ref, v_ref, g_ref, s0f_ref, s0b_ref, gn_ref,
                      u_ref, vn_ref, ws_ref, bs_ref, o_ref, o2_ref, state_ref, yb_ref, *, n_steps):
    h = pl.program_id(0)
    p = pl.program_id(1)
    s = pl.program_id(2)
    c = RET_SUB
    n_sub = q_ref.shape[0] // c
    ii = lax.broadcasted_iota(jnp.int32, (c, c), 0)
    jj = lax.broadcasted_iota(jnp.int32, (c, c), 1)
    idx = lax.broadcasted_iota(jnp.int32, (c, 1), 0).astype(F32)
    nt_dims = (((1,), (1,)), ((), ()))
    tn_dims = (((0,), (0,)), ((), ()))

    def chunk(ci, decay, q_decay, k_decay, chunk_decay):
        rows = pl.ds(ci * c, c)
        qb = q_ref[rows, :]
        kb = k_ref[rows, :]
        v = v_ref[rows, :]
        st = state_ref[...]
        scores = lax.dot_general(qb, kb, nt_dims, preferred_element_type=F32) * decay
        out = (jnp.dot(scores.astype(BF16), v, preferred_element_type=F32)
               + jnp.dot(qb, st.astype(BF16), preferred_element_type=F32) * q_decay)
        state_ref[...] = st * chunk_decay + lax.dot_general(
            (kb.astype(F32) * k_decay).astype(BF16), v, tn_dims, preferred_element_type=F32)
        return out

    @pl.when(p == 0)
    def _backward():
        lg = lgb_ref[h]
        blk = n_steps - 1 - s

        @pl.when(s == 0)
        def _():
            state_ref[...] = s0b_ref[0]

        mask = jj > ii
        decay = jnp.where(mask, jnp.exp(jnp.where(mask, jj - ii, 0).astype(F32) * lg), 0.0)
        q_decay = jnp.exp((c - idx) * lg)
        k_decay = jnp.exp(idx * lg)
        chunk_decay = jnp.exp(jnp.full((1, RET_DV), c, F32) * lg)
        for ci in reversed(range(n_sub)):
            out = chunk(ci, decay, q_decay, k_decay, chunk_decay)
            start = pl.multiple_of(blk * (n_sub * c) + ci * c, c)
            yb_ref[pl.ds(start, c), :] = out

    @pl.when(p == 1)
    def _forward():
        lg = lgf_ref[h]

        @pl.when(s == 0)
        def _():
            state_ref[...] = s0f_ref[0]

        mask = ii >= jj
        decay = jnp.where(mask, jnp.exp(jnp.where(mask, ii - jj, 0).astype(F32) * lg), 0.0)
        q_decay = jnp.exp((idx + 1.0) * lg)
        k_decay = jnp.exp((c - 1.0 - idx) * lg)
        chunk_decay = jnp.exp(jnp.full((1, RET_DV), c, F32) * lg)
        for ci in range(n_sub):
            out = chunk(ci, decay, q_decay, k_decay, chunk_decay)
            start = pl.multiple_of(s * (n_sub * c) + ci * c, c)
            y = out + yb_ref[pl.ds(start, c), :]
            mu = jnp.mean(y, axis=-1, keepdims=True)
            var = jnp.mean(jnp.square(y - mu), axis=-1, keepdims=True)
            yn = (y - mu) * lax.rsqrt(var + EPS) * gn_ref[...]
            gate = g_ref[pl.ds(ci * c, c), :].astype(F32)
            o_ref[pl.ds(ci * c, c), :] = (gate * jax.nn.sigmoid(gate) * yn).astype(o_ref.dtype)

        ws = ws_ref[0].astype(BF16)
        bs = bs_ref[0]
        for ci in range(u_ref.shape[0] // SGU_CHUNK):
            rows = pl.ds(ci * SGU_CHUNK, SGU_CHUNK)
            mixed = jnp.dot(ws, vn_ref[rows, :], preferred_element_type=F32) + bs
            o2_ref[rows, :] = (u_ref[rows, :].astype(F32) * mixed).astype(o2_ref.dtype)


def _mixer(p, lg_f, lg_b, s0_f, s0_b, gn_g, w_s, b_s, n_heads, u_off):
    n_tok = p.shape[0]
    tl = RET_TILE
    n_steps = n_tok // tl
    ub = u_off // SGU_GROUP_DIM
    assert w_s.shape[0] == n_heads and RET_DV == SGU_GROUP_DIM and tl % RET_SUB == 0 and tl % SGU_CHUNK == 0
    smem = pl.BlockSpec(memory_space=pltpu.SMEM)

    def seq_blk(p_, s_):
        return jnp.where(p_ == 0, n_steps - 1 - s_, s_)

    def fwd_blk(col):
        return pl.BlockSpec((tl, RET_DV), lambda h, p_, s_: (s_ * p_, col + h))

    blocks = 8 * _nbytes((tl, RET_DK), BF16) + 2 * _nbytes((RET_DK, RET_DV), F32)
    scratch = _nbytes((RET_DK, RET_DV), F32) + _nbytes((n_tok, RET_DV), F32)
    out = jax.ShapeDtypeStruct((n_tok, n_heads * RET_DV), BF16)
    return pl.pallas_call(
        functools.partial(_retention_kernel, n_steps=n_steps),
        grid=(n_heads, 2, n_steps),
        in_specs=[smem, smem,
                  pl.BlockSpec((tl, RET_DK), lambda h, p_, s_: (seq_blk(p_, s_), h)),
                  pl.BlockSpec((tl, RET_DK), lambda h, p_, s_: (seq_blk(p_, s_), n_heads + h)),
                  pl.BlockSpec((tl, RET_DV), lambda h, p_, s_: (seq_blk(p_, s_), 2 * n_heads + h)),
                  fwd_blk(3 * n_heads),
                  pl.BlockSpec((1, RET_DK, RET_DV), lambda h, p_, s_: (h, 0, 0)),
                  pl.BlockSpec((1, RET_DK, RET_DV), lambda h, p_, s_: (h, 0, 0)),
                  pl.BlockSpec((1, RET_DV), lambda h, p_, s_: (0, h)),
                  fwd_blk(ub),
                  fwd_blk(ub + n_heads),
                  pl.BlockSpec((1, SGU_CHUNK, SGU_CHUNK), lambda h, p_, s_: (h, 0, 0)),
                  pl.BlockSpec((1, SGU_CHUNK, 1), lambda h, p_, s_: (h, 0, 0))],
        out_specs=[fwd_blk(0), fwd_blk(0)],
        out_shape=[out, out],
        scratch_shapes=[pltpu.VMEM((RET_DK, RET_DV), F32), pltpu.VMEM((n_tok, RET_DV), F32)],
        compiler_params=pltpu.CompilerParams(
            dimension_semantics=("arbitrary", "arbitrary", "arbitrary"),
            vmem_limit_bytes=_vmem_limit(blocks, scratch)),
        name="mixer",
    )(lg_f, lg_b, p, p, p, p, s0_f, s0_b, gn_g, p, p, w_s, b_s)


def _split_bf16(t):
    hi = t.astype(BF16)
    lo = (t - hi.astype(F32)).astype(BF16)
    return hi, lo


def _router_kernel(x_ref, g_ref, sh_ref, sc_ref, wr_ref, br_ref, h_ref, eid_ref, ew_ref):
    h = _rms_mod(x_ref[...], g_ref[...], sh_ref[0:1, :], sc_ref[0:1, :])
    half = h.shape[1] // 2
    bits = lax.bitcast_convert_type(h.astype(BF16).astype(F32), jnp.uint32)
    h_ref[...] = (bits[:, half:] & jnp.uint32(0xFFFF0000)) | (bits[:, :half] >> 16)
    h_hi, h_lo = _split_bf16(h)
    w_hi, w_lo = _split_bf16(wr_ref[...])
    logits = (jnp.dot(h_hi, w_hi, preferred_element_type=F32)
              + jnp.dot(h_lo, w_hi, preferred_element_type=F32)
              + jnp.dot(h_hi, w_lo, preferred_element_type=F32)) + br_ref[...]
    lane = lax.broadcasted_iota(jnp.int32, logits.shape, 1)
    lane_f = lane.astype(F32)
    neg = -jnp.inf

    def first_lane(hit):
        return jnp.min(jnp.where(hit, lane_f, float(V7X_LANES)), axis=-1, keepdims=True).astype(jnp.int32)

    gl = jnp.where(lane < MOE_GROUPS, logits, neg)
    g_max = jnp.max(gl, axis=-1, keepdims=True)
    g_sel = first_lane(gl == g_max)
    p_g = 1.0 / jnp.sum(jnp.exp(gl - g_max), axis=-1, keepdims=True)
    e_lo = MOE_GROUPS + g_sel * EXPERTS_PER_GROUP
    el = jnp.where((lane >= e_lo) & (lane < e_lo + EXPERTS_PER_GROUP), logits, neg)
    v1 = jnp.max(el, axis=-1, keepdims=True)
    i1 = first_lane(el == v1)
    el2 = jnp.where(lane == i1, neg, el)
    v2 = jnp.max(el2, axis=-1, keepdims=True)
    i2 = first_lane(el2 == v2)
    e2 = jnp.exp(v2 - v1)
    den = 1.0 + e2
    w1 = p_g * (1.0 / den)
    w2 = p_g * (e2 / den)
    eid_ref[...] = jnp.where(lane == 0, i1 - MOE_GROUPS, jnp.where(lane == 1, i2 - MOE_GROUPS, 0))
    ew_ref[...] = jnp.where(lane == 0, w1, jnp.where(lane == 1, w2, 0.0))


def _router(x, g, mod, shift_chunk, scale_chunk, wr, br):
    m, d = x.shape
    tm = ROW_TILE
    blocks = 2 * _nbytes((tm, d), F32) + 2 * _nbytes((MOD_ROWS, d), F32) + _nbytes((d, V7X_LANES), F32) \
        + 2 * _nbytes((tm, V7X_LANES), F32)
    return pl.pallas_call(
        _router_kernel,
        grid=(m // tm,),
        in_specs=[pl.BlockSpec((tm, d), lambda i: (i, 0)),
                  pl.BlockSpec((1, d), lambda i: (0, 0)),
                  pl.BlockSpec((MOD_ROWS, d), lambda i: (0, shift_chunk)),
                  pl.BlockSpec((MOD_ROWS, d), lambda i: (0, scale_chunk)),
                  pl.BlockSpec((d, V7X_LANES), lambda i: (0, 0)),
                  pl.BlockSpec((1, V7X_LANES), lambda i: (0, 0))],
        out_specs=[pl.BlockSpec((tm, d // 2), lambda i: (i, 0)),
                   pl.BlockSpec((tm, V7X_LANES), lambda i: (i, 0)),
                   pl.BlockSpec((tm, V7X_LANES), lambda i: (i, 0))],
        out_shape=[jax.ShapeDtypeStruct((m, d // 2), jnp.uint32),
                   jax.ShapeDtypeStruct((m, V7X_LANES), jnp.int32),
                   jax.ShapeDtypeStruct((m, V7X_LANES), F32)],
        compiler_params=pltpu.CompilerParams(
            dimension_semantics=("arbitrary",),
            vmem_limit_bytes=_vmem_limit(blocks, 0)),
        name="router",
    )(x, g, mod, mod, wr, br)


def _row_copy(src, src_row, dst, dst_row, sem):
    return pltpu.make_async_copy(src.at[pl.ds(src_row, 1), :], dst.at[pl.ds(dst_row, 1), :], sem)


def _for_rows(n_rows, fn):
    n_groups = lax.shift_right_logical(n_rows, DMA_UNROLL.bit_length() - 1)

    def group(gi, carry):
        for u in range(DMA_UNROLL):
            fn(gi * DMA_UNROLL + u, u)
        return carry

    def single(r, carry):
        fn(r, 0)
        return carry

    lax.fori_loop(0, n_groups, group, 0)
    lax.fori_loop(n_groups * DMA_UNROLL, n_rows, single, 0)


def _experts_kernel(be_ref, nact_ref, nv_ref, cs_ref, stok_ref, sdst_ref,
                    h_hbm, wg_hbm, wu_hbm, wd_hbm, y_hbm,
                    wg_ref, wu_ref, wd_ref, x_ref, acc_ref, gate_ref, hid_ref, sem_w, sem_in, sem_out):
    n_active = nact_ref[0]
    weights = ((wg_hbm, wg_ref), (wu_hbm, wu_ref), (wd_hbm, wd_ref))

    def weight_copy(k, bb):
        src, dst = weights[k]
        return pltpu.make_async_copy(src.at[be_ref[bb]], dst, sem_w.at[k])

    def wait_rows(src, dst, n, sem):
        n_whole = pl.multiple_of(lax.shift_right_logical(n, DMA_UNROLL.bit_length() - 1) * DMA_UNROLL, DMA_UNROLL)

        @pl.when(n_whole > 0)
        def _():
            rows = pl.ds(0, n_whole)
            pltpu.make_async_copy(src.at[rows, :], dst.at[rows, :], sem).wait()

        def single(r, carry):
            _row_copy(src, r, dst, r, sem).wait()
            return carry

        lax.fori_loop(n_whole, n, single, 0)

    def gather(bb, slot_, wait):
        def one(r, u):
            _row_copy(h_hbm, stok_ref[cs_ref[bb] + r], x_ref.at[slot_], r,
                      sem_in.at[slot_]).start(priority=ROW_DMA_PRIORITY)

        if wait:
            wait_rows(h_hbm, x_ref.at[slot_], nv_ref[bb], sem_in.at[slot_])
        else:
            _for_rows(nv_ref[bb], one)

    def scatter(bb, wait):
        par = lax.rem(bb, 2)

        def one(r, u):
            _row_copy(acc_ref.at[par], r, y_hbm, sdst_ref[cs_ref[bb] + r],
                      sem_out.at[par]).start(priority=ROW_DMA_PRIORITY)

        if wait:
            wait_rows(acc_ref.at[par], y_hbm, nv_ref[bb], sem_out.at[par])
        else:
            _for_rows(nv_ref[bb], one)

    half = x_ref.shape[-1]
    sub_starts = [sum(EXPERT_SUBS[:j]) for j in range(len(EXPERT_SUBS))]

    def for_sub_blocks(bb, fn):
        fn(pl.ds(sub_starts[0], EXPERT_SUBS[0]))
        for start, size in zip(sub_starts[1:], EXPERT_SUBS[1:]):
            @pl.when(start < nv_ref[bb])
            def _():
                fn(pl.ds(start, size))

    def x_times(slot, rows, w_ref):
        xw = x_ref[slot, rows, :]
        xa = lax.bitcast_convert_type(xw << 16, F32).astype(BF16)
        xb = lax.bitcast_convert_type(xw & jnp.uint32(0xFFFF0000), F32).astype(BF16)
        return (jnp.dot(xa, w_ref[:half, :].astype(BF16), preferred_element_type=F32)
                + jnp.dot(xb, w_ref[half:, :].astype(BF16), preferred_element_type=F32))

    def block(b, carry):
        slot = lax.rem(b, 2)
        has_next = b + 1 < n_active

        def refill(k):
            @pl.when(has_next)
            def _():
                weight_copy(k, b + 1).start(priority=WEIGHT_DMA_PRIORITY)

        @pl.when(has_next)
        def _():
            gather(b + 1, 1 - slot, False)

        gather(b, slot, True)

        weight_copy(0, b).wait()

        def gate_phase(rows):
            gate_ref[rows, :] = x_times(slot, rows, wg_ref)

        for_sub_blocks(b, gate_phase)
        refill(0)

        weight_copy(1, b).wait()

        def up_phase(rows):
            gate = gate_ref[rows, :]
            hid_ref[rows, :] = (gate * jax.nn.sigmoid(gate) * x_times(slot, rows, wu_ref)).astype(BF16)

        for_sub_blocks(b, up_phase)
        refill(1)

        weight_copy(2, b).wait()

        def down_phase(rows):
            acc_ref[slot, rows, :] = jnp.dot(hid_ref[rows, :], wd_ref[...].astype(BF16),
                                             preferred_element_type=F32)

        for_sub_blocks(b, down_phase)
        refill(2)

        @pl.when(b > 0)
        def _():
            scatter(b - 1, True)

        scatter(b, False)
        return carry

    x_ref[...] = jnp.zeros(x_ref.shape, x_ref.dtype)
    gather(0, 0, False)
    for k in range(len(weights)):
        weight_copy(k, 0).start(priority=WEIGHT_DMA_PRIORITY)
    lax.fori_loop(0, n_active, block, 0)
    scatter(n_active - 1, True)


def _experts(h, w_gate, w_up, w_down, plan, n_out_rows):
    block_e, n_active, n_valid, c_start, s_tok, s_dst = plan
    d = w_gate.shape[1]
    de = w_gate.shape[-1]
    assert h.shape[1] * 2 == d and sum(EXPERT_SUBS) == EXPERT_ROWS

    any_space = pl.BlockSpec(memory_space=pl.ANY)
    scratch = (3 * _nbytes((d, de), F32) + _nbytes((2, EXPERT_ROWS, d // 2), jnp.uint32)
               + _nbytes((2, EXPERT_ROWS, d), F32) + _nbytes((EXPERT_ROWS, de), F32) + _nbytes((EXPERT_ROWS, de), BF16))
    grid_spec = pltpu.PrefetchScalarGridSpec(
        num_scalar_prefetch=6,
        grid=(1,),
        in_specs=[any_space, any_space, any_space, any_space],
        out_specs=any_space,
        scratch_shapes=[pltpu.VMEM((d, de), F32), pltpu.VMEM((d, de), F32), pltpu.VMEM((de, d), F32),
                        pltpu.VMEM((2, EXPERT_ROWS, d // 2), jnp.uint32),
                        pltpu.VMEM((2, EXPERT_ROWS, d), F32),
                        pltpu.VMEM((EXPERT_ROWS, de), F32), pltpu.VMEM((EXPERT_ROWS, de), BF16),
                        pltpu.SemaphoreType.DMA((3,)), pltpu.SemaphoreType.DMA((2,)),
                        pltpu.SemaphoreType.DMA((2,))],
    )
    return pl.pallas_call(
        _experts_kernel,
        grid_spec=grid_spec,
        out_shape=jax.ShapeDtypeStruct((n_out_rows, d), F32),
        compiler_params=pltpu.CompilerParams(
            dimension_semantics=("arbitrary",),
            vmem_limit_bytes=_vmem_limit(0, scratch)),
        name="experts",
    )(block_e, n_active, n_valid, c_start, s_tok, s_dst, h, w_gate, w_up, w_down)


def _combine_kernel(*refs, top_k):
    y_refs = refs[:top_k]
    w_ref, x_ref, g_ref, fg_ref, o_ref = refs[top_k:]
    y = y_refs[0][0] * w_ref[:, 0:1]
    for k in range(1, top_k):
        y = y + y_refs[k][0] * w_ref[:, k:k + 1]
    x = x_ref[...] + g_ref[0:1, :] * y
    o_ref[...] = x * lax.rsqrt(jnp.mean(x * x, axis=-1, keepdims=True) + EPS) * fg_ref[...]


def _combine(y, ew, x, mod, gate_chunk, final_g, top_k):
    m, d = x.shape
    tm = COMBINE_TILE
    blocks = (top_k + 2) * _nbytes((tm, d), F32) + _nbytes((MOD_ROWS, d), F32) + _nbytes((tm, V7X_LANES), F32)
    in_specs = [pl.BlockSpec((1, tm, d), functools.partial(lambda k, i: (k, i, 0), k)) for k in range(top_k)]
    in_specs += [pl.BlockSpec((tm, V7X_LANES), lambda i: (i, 0)),
                 pl.BlockSpec((tm, d), lambda i: (i, 0)),
                 pl.BlockSpec((MOD_ROWS, d), lambda i: (0, gate_chunk)),
                 pl.BlockSpec((1, d), lambda i: (0, 0))]
    return pl.pallas_call(
        functools.partial(_combine_kernel, top_k=top_k),
        grid=(m // tm,),
        in_specs=in_specs,
        out_specs=pl.BlockSpec((tm, d), lambda i: (i, 0)),
        out_shape=jax.ShapeDtypeStruct((m, d), F32),
        compiler_params=pltpu.CompilerParams(
            dimension_semantics=("arbitrary",),
            vmem_limit_bytes=_vmem_limit(blocks, 0)),
        name="combine",
    )(*([y] * top_k), ew, x, mod, final_g)


def _rope_tables(n_tokens):
    n_rows = n_tokens // GRID_W
    n_freq = RET_DK // 4
    freqs = ROPE_BASE ** (-jnp.arange(n_freq, dtype=F32) / n_freq)
    sign = jnp.concatenate([-jnp.ones((n_freq,), F32), jnp.ones((n_freq,), F32)])

    def half_tables(n_pos):
        ang = jnp.arange(n_pos, dtype=F32)[:, None] * freqs
        return jnp.tile(jnp.cos(ang), (1, 2)), jnp.tile(jnp.sin(ang), (1, 2)) * sign

    cos_r, sin_r = half_tables(n_rows)
    cos_c, sin_c = half_tables(GRID_W)

    def expand(by_row, by_col):
        by_row = jnp.broadcast_to(by_row[:, None, :], (n_rows, GRID_W, 2 * n_freq))
        by_col = jnp.broadcast_to(by_col[None, :, :], (n_rows, GRID_W, 2 * n_freq))
        return jnp.concatenate([by_row, by_col], axis=-1).reshape(n_tokens, RET_DK)

    return expand(cos_r, cos_c), expand(sin_r, sin_c)


def _dispatch(eid, n_tok, top_k):
    m = n_tok * top_k
    n_blocks = -(-m // EXPERT_ROWS) + N_EXPERTS
    e_flat = eid[:, :top_k].reshape(-1)
    order = jnp.argsort(e_flat).astype(jnp.int32)
    s_tok = order // top_k
    s_dst = (order % top_k) * n_tok + s_tok
    counts = jnp.sum((e_flat[:, None] == jnp.arange(N_EXPERTS, dtype=jnp.int32)[None, :]).astype(jnp.int32), axis=0)
    starts = jnp.cumsum(counts) - counts
    e_blocks = (counts + EXPERT_ROWS - 1) // EXPERT_ROWS
    b_ends = jnp.cumsum(e_blocks)
    n_active = b_ends[-1].astype(jnp.int32)
    blk = jnp.arange(n_blocks, dtype=jnp.int32)
    block_e = jnp.sum((jnp.minimum(blk, n_active - 1)[:, None] >= b_ends[None, :]).astype(jnp.int32), axis=1)
    block_e = jnp.minimum(block_e, N_EXPERTS - 1)
    own = (block_e[:, None] == jnp.arange(N_EXPERTS, dtype=jnp.int32)[None, :]).astype(jnp.int32)

    def of_block(per_expert):
        return jnp.sum(own * per_expert[None, :], axis=1)

    within = blk - of_block(b_ends - e_blocks)
    c_start = of_block(starts) + within * EXPERT_ROWS
    n_valid = jnp.where(blk < n_active, jnp.clip(of_block(counts) - within * EXPERT_ROWS, 0, EXPERT_ROWS), 0)
    return block_e, n_active.reshape(1), n_valid, c_start, s_tok, s_dst


def kernel(x, c, ctx, c_ctx, w_ada, b_ada, norm1_g, norm2_g, w_in, ret_decay_f, ret_decay_b, ret_gn_g, sgu_ln_g, sgu_ln_b, sgu_w_s, sgu_b_s, w_out, w_router_group, b_router_group, w_router_expert, b_router_expert, w_gate, w_up, w_down, final_g):
    batch, n_tok, d = x.shape
    assert batch == 1 and w_ada.shape[0] == 1
    n_heads = ret_decay_f.shape[-1]
    n_groups = sgu_w_s.shape[1]
    ret_qk_w = n_heads * RET_DK
    ret_w = n_heads * RET_DV
    sgu_w = n_groups * SGU_GROUP_DIM
    k_off = ret_qk_w
    u_off = 2 * ret_qk_w + 2 * ret_w
    in_w = u_off + 2 * sgu_w
    top_k = 2
    assert w_in.shape == (1, d, in_w) and w_out.shape == (1, ret_w + sgu_w, d)

    cc = jnp.concatenate([c, c_ctx[None, :], jnp.zeros((MOD_ROWS - 2, d), F32)], axis=0)
    mod, h1 = _ada_norm(cc, w_ada[0], b_ada[0].reshape(1, N_MOD * d), x[0], norm1_g)

    lg_f = -jnp.exp(ret_decay_f[0])
    lg_b = -jnp.exp(ret_decay_b[0])

    hc = _norm_mod(ctx[0], norm1_g, mod, 1, 0, 1, ctx.shape[1])
    kv_c = _matmul([hc], w_in[0], ret_qk_w + ret_w, k_off, ctx.shape[1], MM_TN)
    s_f, s_b = _ctx_state(kv_c, lg_f, lg_b, n_heads)

    cos, sin_signed = _rope_tables(n_tok)
    p = _in_proj(h1, w_in[0], cos, sin_signed, sgu_ln_g, sgu_ln_b, ret_qk_w, 2 * ret_w, sgu_w, IN_TM, IN_TN)
    ret_out, sgu_out = _mixer(p, lg_f, lg_b, s_f, s_b, ret_gn_g, sgu_w_s[0],
                              sgu_b_s[0].reshape(n_groups, SGU_CHUNK, 1), n_heads, u_off)
    x1 = _matmul([ret_out, sgu_out], w_out[0], d, 0, IN_TM, IN_TN, residual=(x[0], mod, 2))

    n_router = MOE_GROUPS + N_EXPERTS
    wr = jnp.concatenate([w_router_group[0], w_router_expert[0],
                          jnp.zeros((d, V7X_LANES - n_router), F32)], axis=1)
    br = jnp.concatenate([b_router_group, b_router_expert,
                          jnp.zeros((1, V7X_LANES - n_router), F32)], axis=1)
    h2, eid, ew = _router(x1, norm2_g, mod, 3, 4, wr, br)
    plan = _dispatch(eid, n_tok, top_k)
    y = _experts(h2, w_gate[0], w_up[0], w_down[0], plan, top_k * n_tok)
    out = _combine(y.reshape(top_k, n_tok, d), ew, x1, mod, 5, final_g.reshape(1, d), top_k)
    return out.reshape(batch, n_tok, d)
```

```python
import functools

import jax
import jax.numpy as jnp
from jax import lax
from jax.experimental import pallas as pl
from jax.experimental.pallas import tpu as pltpu

F32 = jnp.float32
BF16 = jnp.bfloat16

GRID_W = 64
RET_DK = 256
RET_DV = 256
ROPE_BASE = 10000.0
SGU_GROUP_DIM = 256
SGU_CHUNK = 128
MOE_GROUPS = 8
EXPERTS_PER_GROUP = 8
N_EXPERTS = MOE_GROUPS * EXPERTS_PER_GROUP
N_MOD = 6
EPS = 1e-6

V7X_LANES = 128
V7X_VMEM_BYTES = 64 * 1024 * 1024
VMEM_UNSCOPED_BYTES = 4 * 1024 * 1024
VMEM_MIN_LIMIT_BYTES = 32 * 1024 * 1024
MOD_ROWS = 8

ADA_TN = 512
MM_TN = 512
IN_TM = 512
IN_TN = 1024
ROW_TILE = 512
COMBINE_TILE = 256
RET_TILE = 4096
RET_SUB = 256
EXPERT_ROWS = 512
EXPERT_SUBS = (288, 224)
DMA_UNROLL = 8
ROW_DMA_PRIORITY = 0
WEIGHT_DMA_PRIORITY = 1


def _vmem_limit(block_bytes, scratch_bytes):
    want = 2 * block_bytes + scratch_bytes
    return int(min(V7X_VMEM_BYTES - VMEM_UNSCOPED_BYTES, max(2 * want, VMEM_MIN_LIMIT_BYTES)))


def _nbytes(shape, dtype):
    n = 1
    for s in shape:
        n *= s
    return n * jnp.dtype(dtype).itemsize


def _rms_mod(x, g, shift, scale):
    y = x * lax.rsqrt(jnp.mean(x * x, axis=-1, keepdims=True) + EPS) * g
    return y * (1.0 + scale) + shift


def _ada_norm_kernel(cc_ref, w_ref, b_ref, x_ref, g_ref, mod_ref, h_ref, lead_ref, *, n_lead):
    j = pl.program_id(0)
    a = cc_ref[...]
    s = (a * jax.nn.sigmoid(a)).astype(BF16)
    m = jnp.dot(s, w_ref[...].astype(BF16), preferred_element_type=F32) + b_ref[...]
    mod_ref[...] = m

    @pl.when(j < n_lead)
    def _():
        lead_ref[j] = m

    @pl.when(j >= n_lead)
    def _():
        half = n_lead // 2
        shift = jnp.concatenate([lead_ref[t][0:1, :] for t in range(half)], axis=1)
        scale = jnp.concatenate([lead_ref[t][0:1, :] for t in range(half, n_lead)], axis=1)
        h_ref[...] = _rms_mod(x_ref[...], g_ref[...], shift, scale).astype(h_ref.dtype)


def _ada_norm(cc, w, b, x, g):
    d, n = w.shape
    m_rows = x.shape[0]
    n_tiles = n // ADA_TN
    n_lead = 2 * d // ADA_TN
    assert m_rows % (n_tiles - n_lead) == 0
    tm = m_rows // (n_tiles - n_lead)
    assert tm % 16 == 0

    def row_idx(j):
        return (jnp.maximum(j - n_lead, 0), 0)

    blocks = (_nbytes((MOD_ROWS, d), F32) + _nbytes((d, ADA_TN), F32) + 2 * _nbytes((MOD_ROWS, ADA_TN), F32)
              + _nbytes((tm, d), F32) + _nbytes((tm, d), BF16))
    scratch = _nbytes((n_lead, MOD_ROWS, ADA_TN), F32) + _nbytes((d, ADA_TN), BF16)
    return pl.pallas_call(
        functools.partial(_ada_norm_kernel, n_lead=n_lead),
        grid=(n_tiles,),
        in_specs=[pl.BlockSpec((MOD_ROWS, d), lambda j: (0, 0)),
                  pl.BlockSpec((d, ADA_TN), lambda j: (0, j)),
                  pl.BlockSpec((1, ADA_TN), lambda j: (0, j)),
                  pl.BlockSpec((tm, d), row_idx),
                  pl.BlockSpec((1, d), lambda j: (0, 0))],
        out_specs=[pl.BlockSpec((MOD_ROWS, ADA_TN), lambda j: (0, j)),
                   pl.BlockSpec((tm, d), row_idx)],
        out_shape=[jax.ShapeDtypeStruct((MOD_ROWS, n), F32), jax.ShapeDtypeStruct((m_rows, d), BF16)],
        scratch_shapes=[pltpu.VMEM((n_lead, MOD_ROWS, ADA_TN), F32)],
        compiler_params=pltpu.CompilerParams(
            dimension_semantics=("arbitrary",),
            vmem_limit_bytes=_vmem_limit(blocks, scratch)),
        name="ada_norm",
    )(cc, w, b, x, g)


def _norm_mod_kernel(x_ref, g_ref, sh_ref, sc_ref, o_ref, *, row):
    h = _rms_mod(x_ref[...], g_ref[...], sh_ref[row:row + 1, :], sc_ref[row:row + 1, :])
    o_ref[...] = h.astype(o_ref.dtype)


def _norm_mod(x, g, mod, row, shift_chunk, scale_chunk, tm):
    m, d = x.shape
    blocks = _nbytes((tm, d), F32) * 2 + 3 * _nbytes((MOD_ROWS, d), F32)
    return pl.pallas_call(
        functools.partial(_norm_mod_kernel, row=row),
        grid=(m // tm,),
        in_specs=[pl.BlockSpec((tm, d), lambda i: (i, 0)),
                  pl.BlockSpec((1, d), lambda i: (0, 0)),
                  pl.BlockSpec((MOD_ROWS, d), lambda i: (0, shift_chunk)),
                  pl.BlockSpec((MOD_ROWS, d), lambda i: (0, scale_chunk))],
        out_specs=pl.BlockSpec((tm, d), lambda i: (i, 0)),
        out_shape=jax.ShapeDtypeStruct((m, d), BF16),
        compiler_params=pltpu.CompilerParams(
            dimension_semantics=("arbitrary",),
            vmem_limit_bytes=_vmem_limit(blocks, 0)),
        name="norm_mod",
    )(x, g, mod, mod)


def _matmul_acc(a_refs, w_ref, wbf_ref):
    @pl.when(pl.program_id(1) == 0)
    def _():
        wbf_ref[...] = w_ref[...].astype(BF16)

    acc = None
    k0 = 0
    for a_ref in a_refs:
        kk = a_ref.shape[1]
        part = jnp.dot(a_ref[...], wbf_ref[k0:k0 + kk, :], preferred_element_type=F32)
        acc = part if acc is None else acc + part
        k0 += kk
    return acc


def _matmul_kernel(*refs, n_a):
    w_ref, o_ref, wbf_ref = refs[n_a:]
    o_ref[...] = _matmul_acc(refs[:n_a], w_ref, wbf_ref).astype(o_ref.dtype)


def _matmul_res_kernel(*refs, n_a):
    w_ref, x_ref, g_ref, o_ref, wbf_ref = refs[n_a:]
    o_ref[...] = x_ref[...] + g_ref[0:1, :] * _matmul_acc(refs[:n_a], w_ref, wbf_ref)


def _matmul(a_list, w, n_cols, col_off, tm, tn, residual=None, out_dtype=F32):
    m = a_list[0].shape[0]
    k = sum(a.shape[1] for a in a_list)
    n_a = len(a_list)
    joff = col_off // tn
    grid = (n_cols // tn, m // tm)
    in_specs = [pl.BlockSpec((tm, a.shape[1]), lambda j, i: (i, 0)) for a in a_list]
    in_specs.append(pl.BlockSpec((k, tn), lambda j, i: (0, j + joff)))
    args = list(a_list) + [w]
    blocks = _nbytes((tm, k), BF16) + _nbytes((k, tn), F32) + _nbytes((tm, tn), F32)
    if residual is None:
        body = functools.partial(_matmul_kernel, n_a=n_a)
    else:
        x, mod, gate_chunk = residual
        goff = gate_chunk * (n_cols // tn)
        in_specs += [pl.BlockSpec((tm, tn), lambda j, i: (i, j)),
                     pl.BlockSpec((MOD_ROWS, tn), lambda j, i: (0, goff + j))]
        args += [x, mod]
        blocks += _nbytes((tm, tn), F32) + _nbytes((MOD_ROWS, tn), F32)
        body = functools.partial(_matmul_res_kernel, n_a=n_a)
    return pl.pallas_call(
        body,
        grid=grid,
        in_specs=in_specs,
        out_specs=pl.BlockSpec((tm, tn), lambda j, i: (i, j)),
        out_shape=jax.ShapeDtypeStruct((m, n_cols), out_dtype),
        scratch_shapes=[pltpu.VMEM((k, tn), BF16)],
        compiler_params=pltpu.CompilerParams(
            dimension_semantics=("arbitrary", "arbitrary"),
            vmem_limit_bytes=_vmem_limit(blocks, _nbytes((k, tn), BF16))),
        name="matmul_res" if residual is not None else "matmul",
    )(*args)


def _rope(t, cos, sin_signed):
    half = RET_DK // 2
    rot = jnp.concatenate([pltpu.roll(t[:, :half], half // 2, 1),
                           pltpu.roll(t[:, half:], half // 2, 1)], axis=1)
    return t * cos + rot * sin_signed


def _in_proj_kernel(a_ref, w_ref, cos_ref, sin_ref, lng_ref, lnb_ref, o_ref, wbf_ref, *, tile_ends):
    j = pl.program_id(0)
    q_end, k_end, plain_end, u_end = tile_ends

    def acc():
        return _matmul_acc([a_ref], w_ref, wbf_ref)

    @pl.when(j < k_end)
    def _rotated():
        t = acc() * jnp.where(j < q_end, 1.0, RET_DK ** -0.5)
        cos = cos_ref[...]
        sin = sin_ref[...]
        for hh in range(t.shape[1] // RET_DK):
            cols = slice(hh * RET_DK, (hh + 1) * RET_DK)
            o_ref[:, cols] = _rope(t[:, cols], cos, sin).astype(o_ref.dtype)

    @pl.when((j >= k_end) & (j < plain_end))
    def _plain():
        o_ref[...] = acc().astype(o_ref.dtype)

    @pl.when((j >= plain_end) & (j < u_end))
    def _gelu():
        o_ref[...] = jax.nn.gelu(acc()).astype(o_ref.dtype)

    @pl.when(j >= u_end)
    def _gelu_norm():
        t = jax.nn.gelu(acc())
        for gg in range(t.shape[1] // SGU_GROUP_DIM):
            cols = slice(gg * SGU_GROUP_DIM, (gg + 1) * SGU_GROUP_DIM)
            v32 = t[:, cols]
            mu = jnp.mean(v32, axis=-1, keepdims=True)
            var = jnp.mean(jnp.square(v32 - mu), axis=-1, keepdims=True)
            vn = (v32 - mu) * lax.rsqrt(var + EPS) * lng_ref[:, cols] + lnb_ref[:, cols]
            o_ref[:, cols] = vn.astype(o_ref.dtype)


def _in_proj(a, w, cos, sin_signed, ln_g, ln_b, qk_w, v_gate_w, sgu_w, tm, tn):
    m, k = a.shape
    n = w.shape[1]
    assert n == 2 * qk_w + v_gate_w + 2 * sgu_w
    assert qk_w % tn == 0 and v_gate_w % tn == 0 and sgu_w % tn == 0
    assert tn % RET_DK == 0 and tn % SGU_GROUP_DIM == 0
    q_end = qk_w // tn
    k_end = 2 * q_end
    plain_end = k_end + v_gate_w // tn
    u_end = plain_end + sgu_w // tn

    def table_idx(j, i):
        return (jnp.where(j < k_end, i, 0), 0)

    def ln_idx(j, i):
        return (0, jnp.maximum(j - u_end, 0))

    blocks = (_nbytes((tm, k), BF16) + _nbytes((k, tn), F32) + _nbytes((tm, tn), BF16)
              + 2 * _nbytes((tm, RET_DK), F32))
    return pl.pallas_call(
        functools.partial(_in_proj_kernel, tile_ends=(q_end, k_end, plain_end, u_end)),
        grid=(n // tn, m // tm),
        in_specs=[pl.BlockSpec((tm, k), lambda j, i: (i, 0)),
                  pl.BlockSpec((k, tn), lambda j, i: (0, j)),
                  pl.BlockSpec((tm, RET_DK), table_idx),
                  pl.BlockSpec((tm, RET_DK), table_idx),
                  pl.BlockSpec((1, tn), ln_idx),
                  pl.BlockSpec((1, tn), ln_idx)],
        out_specs=pl.BlockSpec((tm, tn), lambda j, i: (i, j)),
        out_shape=jax.ShapeDtypeStruct((m, n), BF16),
        scratch_shapes=[pltpu.VMEM((k, tn), BF16)],
        compiler_params=pltpu.CompilerParams(
            dimension_semantics=("arbitrary", "arbitrary"),
            vmem_limit_bytes=_vmem_limit(blocks, _nbytes((k, tn), BF16))),
        name="in_proj",
    )(a, w, cos, sin_signed, ln_g, ln_b)


def _ctx_state_kernel(lgf_ref, lgb_ref, k_ref, v_ref, sf_ref, sb_ref):
    h = pl.program_id(0)
    n = k_ref.shape[0]
    pos = lax.broadcasted_iota(jnp.int32, (n, 1), 0).astype(F32)
    k = k_ref[...] * (RET_DK ** -0.5)
    v = v_ref[...].astype(BF16)
    wf = jnp.exp((n - 1.0 - pos) * lgf_ref[h])
    wb = jnp.exp(pos * lgb_ref[h])
    tn_dims = (((0,), (0,)), ((), ()))
    sf_ref[0] = lax.dot_general((k * wf).astype(BF16), v, tn_dims, preferred_element_type=F32)
    sb_ref[0] = lax.dot_general((k * wb).astype(BF16), v, tn_dims, preferred_element_type=F32)


def _ctx_state(kv, lg_f, lg_b, n_heads):
    n = kv.shape[0]
    smem = pl.BlockSpec(memory_space=pltpu.SMEM)
    st = jax.ShapeDtypeStruct((n_heads, RET_DK, RET_DV), F32)
    return pl.pallas_call(
        _ctx_state_kernel,
        grid=(n_heads,),
        in_specs=[smem, smem,
                  pl.BlockSpec((n, RET_DK), lambda h: (0, h)),
                  pl.BlockSpec((n, RET_DV), lambda h: (0, n_heads + h))],
        out_specs=[pl.BlockSpec((1, RET_DK, RET_DV), lambda h: (h, 0, 0)),
                   pl.BlockSpec((1, RET_DK, RET_DV), lambda h: (h, 0, 0))],
        out_shape=[st, st],
        compiler_params=pltpu.CompilerParams(dimension_semantics=("arbitrary",)),
        name="ctx_state",
    )(lg_f, lg_b, kv, kv)


def _retention_kernel(lgf_ref, lgb_ref, q_ref, k_ref, v_ref, g_ref, s0f_ref, s0b_ref, gn_ref,
                      u_ref, vn_ref, ws_ref, bs_ref, o_ref, o2_ref, state_ref, yb_ref, *, n_steps):
    h = pl.program_id(0)
    p = pl.program_id(1)
    s = pl.program_id(2)
    c = RET_SUB
    n_sub = q_ref.shape[0] // c
    ii = lax.broadcasted_iota(jnp.int32, (c, c), 0)
    jj = lax.broadcasted_iota(jnp.int32, (c, c), 1)
    idx = lax.broadcasted_iota(jnp.int32, (c, 1), 0).astype(F32)
    nt_dims = (((1,), (1,)), ((), ()))
    tn_dims = (((0,), (0,)), ((), ()))

    def chunk(ci, decay, q_decay, k_decay, chunk_decay):
        rows = pl.ds(ci * c, c)
        qb = q_ref[rows, :]
        kb = k_ref[rows, :]
        v = v_ref[rows, :]
        st = state_ref[...]
        scores = lax.dot_general(qb, kb, nt_dims, preferred_element_type=F32) * decay
        out = (jnp.dot(scores.astype(BF16), v, preferred_element_type=F32)
               + jnp.dot(qb, st.astype(BF16), preferred_element_type=F32) * q_decay)
        state_ref[...] = st * chunk_decay + lax.dot_general(
            (kb.astype(F32) * k_decay).astype(BF16), v, tn_dims, preferred_element_type=F32)
        return out

    @pl.when(p == 0)
    def _backward():
        lg = lgb_ref[h]
        blk = n_steps - 1 - s

        @pl.when(s == 0)
        def _():
            state_ref[...] = s0b_ref[0]

        mask = jj > ii
        decay = jnp.where(mask, jnp.exp(jnp.where(mask, jj - ii, 0).astype(F32) * lg), 0.0)
        q_decay = jnp.exp((c - idx) * lg)
        k_decay = jnp.exp(idx * lg)
        chunk_decay = jnp.exp(jnp.full((1, RET_DV), c, F32) * lg)
        for ci in reversed(range(n_sub)):
            out = chunk(ci, decay, q_decay, k_decay, chunk_decay)
            start = pl.multiple_of(blk * (n_sub * c) + ci * c, c)
            yb_ref[pl.ds(start, c), :] = out

    @pl.when(p == 1)
    def _forward():
        lg = lgf_ref[h]

        @pl.when(s == 0)
        def _():
            state_ref[...] = s0f_ref[0]

        mask = ii >= jj
        decay = jnp.where(mask, jnp.exp(jnp.where(mask, ii - jj, 0).astype(F32) * lg), 0.0)
        q_decay = jnp.exp((idx + 1.0) * lg)
        k_decay = jnp.exp((c - 1.0 - idx) * lg)
        chunk_decay = jnp.exp(jnp.full((1, RET_DV), c, F32) * lg)
        for ci in range(n_sub):
            out = chunk(ci, decay, q_decay, k_decay, chunk_decay)
            start = pl.multiple_of(s * (n_sub * c) + ci * c, c)
            y = out + yb_ref[pl.ds(start, c), :]
            mu = jnp.mean(y, axis=-1, keepdims=True)
            var = jnp.mean(jnp.square(y - mu), axis=-1, keepdims=True)
            yn = (y - mu) * lax.rsqrt(var + EPS) * gn_ref[...]
            gate = g_ref[pl.ds(ci * c, c), :].astype(F32)
            o_ref[pl.ds(ci * c, c), :] = (gate * jax.nn.sigmoid(gate) * yn).astype(o_ref.dtype)

        ws = ws_ref[0].astype(BF16)
        bs = bs_ref[0]
        for ci in range(u_ref.shape[0] // SGU_CHUNK):
            rows = pl.ds(ci * SGU_CHUNK, SGU_CHUNK)
            mixed = jnp.dot(ws, vn_ref[rows, :], preferred_element_type=F32) + bs
            o2_ref[rows, :] = (u_ref[rows, :].astype(F32) * mixed).astype(o2_ref.dtype)


def _mixer(p, lg_f, lg_b, s0_f, s0_b, gn_g, w_s, b_s, n_heads, u_off):
    n_tok = p.shape[0]
    tl = RET_TILE
    n_steps = n_tok // tl
    ub = u_off // SGU_GROUP_DIM
    assert w_s.shape[0] == n_heads and RET_DV == SGU_GROUP_DIM and tl % RET_SUB == 0 and tl % SGU_CHUNK == 0
    smem = pl.BlockSpec(memory_space=pltpu.SMEM)

    def seq_blk(p_, s_):
        return jnp.where(p_ == 0, n_steps - 1 - s_, s_)

    def fwd_blk(col):
        return pl.BlockSpec((tl, RET_DV), lambda h, p_, s_: (s_ * p_, col + h))

    blocks = 8 * _nbytes((tl, RET_DK), BF16) + 2 * _nbytes((RET_DK, RET_DV), F32)
    scratch = _nbytes((RET_DK, RET_DV), F32) + _nbytes((n_tok, RET_DV), F32)
    out = jax.ShapeDtypeStruct((n_tok, n_heads * RET_DV), BF16)
    return pl.pallas_call(
        functools.partial(_retention_kernel, n_steps=n_steps),
        grid=(n_heads, 2, n_steps),
        in_specs=[smem, smem,
                  pl.BlockSpec((tl, RET_DK), lambda h, p_, s_: (seq_blk(p_, s_), h)),
                  pl.BlockSpec((tl, RET_DK), lambda h, p_, s_: (seq_blk(p_, s_), n_heads + h)),
                  pl.BlockSpec((tl, RET_DV), lambda h, p_, s_: (seq_blk(p_, s_), 2 * n_heads + h)),
                  fwd_blk(3 * n_heads),
                  pl.BlockSpec((1, RET_DK, RET_DV), lambda h, p_, s_: (h, 0, 0)),
                  pl.BlockSpec((1, RET_DK, RET_DV), lambda h, p_, s_: (h, 0, 0)),
                  pl.BlockSpec((1, RET_DV), lambda h, p_, s_: (0, h)),
                  fwd_blk(ub),
                  fwd_blk(ub + n_heads),
                  pl.BlockSpec((1, SGU_CHUNK, SGU_CHUNK), lambda h, p_, s_: (h, 0, 0)),
                  pl.BlockSpec((1, SGU_CHUNK, 1), lambda h, p_, s_: (h, 0, 0))],
        out_specs=[fwd_blk(0), fwd_blk(0)],
        out_shape=[out, out],
        scratch_shapes=[pltpu.VMEM((RET_DK, RET_DV), F32), pltpu.VMEM((n_tok, RET_DV), F32)],
        compiler_params=pltpu.CompilerParams(
            dimension_semantics=("arbitrary", "arbitrary", "arbitrary"),
            vmem_limit_bytes=_vmem_limit(blocks, scratch)),
        name="mixer",
    )(lg_f, lg_b, p, p, p, p, s0_f, s0_b, gn_g, p, p, w_s, b_s)


def _split_bf16(t):
    hi = t.astype(BF16)
    lo = (t - hi.astype(F32)).astype(BF16)
    return hi, lo


def _router_kernel(x_ref, g_ref, sh_ref, sc_ref, wr_ref, br_ref, h_ref, eid_ref, ew_ref):
    h = _rms_mod(x_ref[...], g_ref[...], sh_ref[0:1, :], sc_ref[0:1, :])
    half = h.shape[1] // 2
    bits = lax.bitcast_convert_type(h.astype(BF16).astype(F32), jnp.uint32)
    h_ref[...] = (bits[:, half:] & jnp.uint32(0xFFFF0000)) | (bits[:, :half] >> 16)
    h_hi, h_lo = _split_bf16(h)
    w_hi, w_lo = _split_bf16(wr_ref[...])
    logits = (jnp.dot(h_hi, w_hi, preferred_element_type=F32)
              + jnp.dot(h_lo, w_hi, preferred_element_type=F32)
              + jnp.dot(h_hi, w_lo, preferred_element_type=F32)) + br_ref[...]
    lane = lax.broadcasted_iota(jnp.int32, logits.shape, 1)
    lane_f = lane.astype(F32)
    neg = -jnp.inf

    def first_lane(hit):
        return jnp.min(jnp.where(hit, lane_f, float(V7X_LANES)), axis=-1, keepdims=True).astype(jnp.int32)

    gl = jnp.where(lane < MOE_GROUPS, logits, neg)
    g_max = jnp.max(gl, axis=-1, keepdims=True)
    g_sel = first_lane(gl == g_max)
    p_g = 1.0 / jnp.sum(jnp.exp(gl - g_max), axis=-1, keepdims=True)
    e_lo = MOE_GROUPS + g_sel * EXPERTS_PER_GROUP
    el = jnp.where((lane >= e_lo) & (lane < e_lo + EXPERTS_PER_GROUP), logits, neg)
    v1 = jnp.max(el, axis=-1, keepdims=True)
    i1 = first_lane(el == v1)
    el2 = jnp.where(lane == i1, neg, el)
    v2 = jnp.max(el2, axis=-1, keepdims=True)
    i2 = first_lane(el2 == v2)
    e2 = jnp.exp(v2 - v1)
    den = 1.0 + e2
    w1 = p_g * (1.0 / den)
    w2 = p_g * (e2 / den)
    eid_ref[...] = jnp.where(lane == 0, i1 - MOE_GROUPS, jnp.where(lane == 1, i2 - MOE_GROUPS, 0))
    ew_ref[...] = jnp.where(lane == 0, w1, jnp.where(lane == 1, w2, 0.0))


def _router(x, g, mod, shift_chunk, scale_chunk, wr, br):
    m, d = x.shape
    tm = ROW_TILE
    blocks = 2 * _nbytes((tm, d), F32) + 2 * _nbytes((MOD_ROWS, d), F32) + _nbytes((d, V7X_LANES), F32) \
        + 2 * _nbytes((tm, V7X_LANES), F32)
    return pl.pallas_call(
        _router_kernel,
        grid=(m // tm,),
        in_specs=[pl.BlockSpec((tm, d), lambda i: (i, 0)),
                  pl.BlockSpec((1, d), lambda i: (0, 0)),
                  pl.BlockSpec((MOD_ROWS, d), lambda i: (0, shift_chunk)),
                  pl.BlockSpec((MOD_ROWS, d), lambda i: (0, scale_chunk)),
                  pl.BlockSpec((d, V7X_LANES), lambda i: (0, 0)),
                  pl.BlockSpec((1, V7X_LANES), lambda i: (0, 0))],
        out_specs=[pl.BlockSpec((tm, d // 2), lambda i: (i, 0)),
                   pl.BlockSpec((tm, V7X_LANES), lambda i: (i, 0)),
                   pl.BlockSpec((tm, V7X_LANES), lambda i: (i, 0))],
        out_shape=[jax.ShapeDtypeStruct((m, d // 2), jnp.uint32),
                   jax.ShapeDtypeStruct((m, V7X_LANES), jnp.int32),
                   jax.ShapeDtypeStruct((m, V7X_LANES), F32)],
        compiler_params=pltpu.CompilerParams(
            dimension_semantics=("arbitrary",),
            vmem_limit_bytes=_vmem_limit(blocks, 0)),
        name="router",
    )(x, g, mod, mod, wr, br)


def _row_copy(src, src_row, dst, dst_row, sem):
    return pltpu.make_async_copy(src.at[pl.ds(src_row, 1), :], dst.at[pl.ds(dst_row, 1), :], sem)


def _for_rows(n_rows, fn):
    n_groups = lax.shift_right_logical(n_rows, DMA_UNROLL.bit_length() - 1)

    def group(gi, carry):
        for u in range(DMA_UNROLL):
            fn(gi * DMA_UNROLL + u)
        return carry

    def single(r, carry):
        fn(r)
        return carry

    lax.fori_loop(0, n_groups, group, 0)
    lax.fori_loop(n_groups * DMA_UNROLL, n_rows, single, 0)


def _experts_kernel(be_ref, nact_ref, nv_ref, cs_ref, stok_ref, sdst_ref,
                    h_hbm, wg_hbm, wu_hbm, wd_hbm, y_hbm,
                    wg_ref, wu_ref, wd_ref, x_ref, acc_ref, gate_ref, hid_ref, sem_w, sem_in, sem_out):
    n_active = nact_ref[0]
    weights = ((wg_hbm, wg_ref), (wu_hbm, wu_ref), (wd_hbm, wd_ref))

    def weight_copy(k, bb):
        src, dst = weights[k]
        return pltpu.make_async_copy(src.at[be_ref[bb]], dst, sem_w.at[k])

    def wait_rows(src, dst, n, sem):
        n_whole = pl.multiple_of(lax.shift_right_logical(n, DMA_UNROLL.bit_length() - 1) * DMA_UNROLL, DMA_UNROLL)

        @pl.when(n_whole > 0)
        def _():
            rows = pl.ds(0, n_whole)
            pltpu.make_async_copy(src.at[rows, :], dst.at[rows, :], sem).wait()

        def single(r, carry):
            _row_copy(src, r, dst, r, sem).wait()
            return carry

        lax.fori_loop(n_whole, n, single, 0)

    def gather(bb, slot_, wait):
        def one(r):
            _row_copy(h_hbm, stok_ref[cs_ref[bb] + r], x_ref.at[slot_], r,
                      sem_in.at[slot_]).start(priority=ROW_DMA_PRIORITY)

        if wait:
            wait_rows(h_hbm, x_ref.at[slot_], nv_ref[bb], sem_in.at[slot_])
        else:
            _for_rows(nv_ref[bb], one)

    def scatter(bb, wait):
        par = lax.rem(bb, 2)

        def one(r):
            _row_copy(acc_ref.at[par], r, y_hbm, sdst_ref[cs_ref[bb] + r],
                      sem_out.at[par]).start(priority=ROW_DMA_PRIORITY)

        if wait:
            wait_rows(acc_ref.at[par], y_hbm, nv_ref[bb], sem_out.at[par])
        else:
            _for_rows(nv_ref[bb], one)

    half = x_ref.shape[-1]
    sub_starts = [sum(EXPERT_SUBS[:j]) for j in range(len(EXPERT_SUBS))]

    def for_sub_blocks(bb, fn):
        fn(pl.ds(sub_starts[0], EXPERT_SUBS[0]))
        for start, size in zip(sub_starts[1:], EXPERT_SUBS[1:]):
            @pl.when(start < nv_ref[bb])
            def _():
                fn(pl.ds(start, size))

    def x_times(slot, rows, w_ref):
        xw = x_ref[slot, rows, :]
        xa = lax.bitcast_convert_type(xw << 16, F32).astype(BF16)
        xb = lax.bitcast_convert_type(xw & jnp.uint32(0xFFFF0000), F32).astype(BF16)
        return (jnp.dot(xa, w_ref[:half, :].astype(BF16), preferred_element_type=F32)
                + jnp.dot(xb, w_ref[half:, :].astype(BF16), preferred_element_type=F32))

    def block(b, carry):
        slot = lax.rem(b, 2)
        has_next = b + 1 < n_active

        def refill(k):
            @pl.when(has_next)
            def _():
                weight_copy(k, b + 1).start(priority=WEIGHT_DMA_PRIORITY)

        @pl.when(has_next)
        def _():
            gather(b + 1, 1 - slot, False)

        gather(b, slot, True)

        weight_copy(0, b).wait()

        def gate_phase(rows):
            gate_ref[rows, :] = x_times(slot, rows, wg_ref)

        for_sub_blocks(b, gate_phase)
        refill(0)

        weight_copy(1, b).wait()

        def up_phase(rows):
            gate = gate_ref[rows, :]
            hid_ref[rows, :] = (gate * jax.nn.sigmoid(gate) * x_times(slot, rows, wu_ref)).astype(BF16)

        for_sub_blocks(b, up_phase)
        refill(1)

        weight_copy(2, b).wait()

        def down_phase(rows):
            acc_ref[slot, rows, :] = jnp.dot(hid_ref[rows, :], wd_ref[...].astype(BF16),
                                             preferred_element_type=F32)

        for_sub_blocks(b, down_phase)
        refill(2)

        @pl.when(b > 0)
        def _():
            scatter(b - 1, True)

        scatter(b, False)
        return carry

    x_ref[...] = jnp.zeros(x_ref.shape, x_ref.dtype)
    gather(0, 0, False)
    for k in range(len(weights)):
        weight_copy(k, 0).start(priority=WEIGHT_DMA_PRIORITY)
    lax.fori_loop(0, n_active, block, 0)
    scatter(n_active - 1, True)


def _experts(h, w_gate, w_up, w_down, plan, n_out_rows):
    block_e, n_active, n_valid, c_start, s_tok, s_dst = plan
    d = w_gate.shape[1]
    de = w_gate.shape[-1]
    assert h.shape[1] * 2 == d and sum(EXPERT_SUBS) == EXPERT_ROWS

    any_space = pl.BlockSpec(memory_space=pl.ANY)
    scratch = (3 * _nbytes((d, de), F32) + _nbytes((2, EXPERT_ROWS, d // 2), jnp.uint32)
               + _nbytes((2, EXPERT_ROWS, d), F32) + _nbytes((EXPERT_ROWS, de), F32) + _nbytes((EXPERT_ROWS, de), BF16))
    grid_spec = pltpu.PrefetchScalarGridSpec(
        num_scalar_prefetch=6,
        grid=(1,),
        in_specs=[any_space, any_space, any_space, any_space],
        out_specs=any_space,
        scratch_shapes=[pltpu.VMEM((d, de), F32), pltpu.VMEM((d, de), F32), pltpu.VMEM((de, d), F32),
                        pltpu.VMEM((2, EXPERT_ROWS, d // 2), jnp.uint32),
                        pltpu.VMEM((2, EXPERT_ROWS, d), F32),
                        pltpu.VMEM((EXPERT_ROWS, de), F32), pltpu.VMEM((EXPERT_ROWS, de), BF16),
                        pltpu.SemaphoreType.DMA((3,)), pltpu.SemaphoreType.DMA((2,)),
                        pltpu.SemaphoreType.DMA((2,))],
    )
    return pl.pallas_call(
        _experts_kernel,
        grid_spec=grid_spec,
        out_shape=jax.ShapeDtypeStruct((n_out_rows, d), F32),
        compiler_params=pltpu.CompilerParams(
            dimension_semantics=("arbitrary",),
            vmem_limit_bytes=_vmem_limit(0, scratch)),
        name="experts",
    )(block_e, n_active, n_valid, c_start, s_tok, s_dst, h, w_gate, w_up, w_down)


def _combine_kernel(*refs, top_k):
    y_refs = refs[:top_k]
    w_ref, x_ref, g_ref, fg_ref, o_ref = refs[top_k:]
    y = y_refs[0][0] * w_ref[:, 0:1]
    for k in range(1, top_k):
        y = y + y_refs[k][0] * w_ref[:, k:k + 1]
    x = x_ref[...] + g_ref[0:1, :] * y
    o_ref[...] = x * lax.rsqrt(jnp.mean(x * x, axis=-1, keepdims=True) + EPS) * fg_ref[...]


def _combine(y, ew, x, mod, gate_chunk, final_g, top_k):
    m, d = x.shape
    tm = COMBINE_TILE
    blocks = (top_k + 2) * _nbytes((tm, d), F32) + _nbytes((MOD_ROWS, d), F32) + _nbytes((tm, V7X_LANES), F32)
    in_specs = [pl.BlockSpec((1, tm, d), functools.partial(lambda k, i: (k, i, 0), k)) for k in range(top_k)]
    in_specs += [pl.BlockSpec((tm, V7X_LANES), lambda i: (i, 0)),
                 pl.BlockSpec((tm, d), lambda i: (i, 0)),
                 pl.BlockSpec((MOD_ROWS, d), lambda i: (0, gate_chunk)),
                 pl.BlockSpec((1, d), lambda i: (0, 0))]
    return pl.pallas_call(
        functools.partial(_combine_kernel, top_k=top_k),
        grid=(m // tm,),
        in_specs=in_specs,
        out_specs=pl.BlockSpec((tm, d), lambda i: (i, 0)),
        out_shape=jax.ShapeDtypeStruct((m, d), F32),
        compiler_params=pltpu.CompilerParams(
            dimension_semantics=("arbitrary",),
            vmem_limit_bytes=_vmem_limit(blocks, 0)),
        name="combine",
    )(*([y] * top_k), ew, x, mod, final_g)


def _rope_tables(n_tokens):
    n_rows = n_tokens // GRID_W
    n_freq = RET_DK // 4
    freqs = ROPE_BASE ** (-jnp.arange(n_freq, dtype=F32) / n_freq)
    sign = jnp.concatenate([-jnp.ones((n_freq,), F32), jnp.ones((n_freq,), F32)])

    def half_tables(n_pos):
        ang = jnp.arange(n_pos, dtype=F32)[:, None] * freqs
        return jnp.tile(jnp.cos(ang), (1, 2)), jnp.tile(jnp.sin(ang), (1, 2)) * sign

    cos_r, sin_r = half_tables(n_rows)
    cos_c, sin_c = half_tables(GRID_W)

    def expand(by_row, by_col):
        by_row = jnp.broadcast_to(by_row[:, None, :], (n_rows, GRID_W, 2 * n_freq))
        by_col = jnp.broadcast_to(by_col[None, :, :], (n_rows, GRID_W, 2 * n_freq))
        return jnp.concatenate([by_row, by_col], axis=-1).reshape(n_tokens, RET_DK)

    return expand(cos_r, cos_c), expand(sin_r, sin_c)


def _dispatch(eid, n_tok, top_k):
    m = n_tok * top_k
    n_blocks = -(-m // EXPERT_ROWS) + N_EXPERTS
    e_flat = eid[:, :top_k].reshape(-1)
    order = jnp.argsort(e_flat).astype(jnp.int32)
    s_tok = order // top_k
    s_dst = (order % top_k) * n_tok + s_tok
    counts = jnp.sum((e_flat[:, None] == jnp.arange(N_EXPERTS, dtype=jnp.int32)[None, :]).astype(jnp.int32), axis=0)
    starts = jnp.cumsum(counts) - counts
    e_blocks = (counts + EXPERT_ROWS - 1) // EXPERT_ROWS
    b_ends = jnp.cumsum(e_blocks)
    n_active = b_ends[-1].astype(jnp.int32)
    blk = jnp.arange(n_blocks, dtype=jnp.int32)
    block_e = jnp.sum((jnp.minimum(blk, n_active - 1)[:, None] >= b_ends[None, :]).astype(jnp.int32), axis=1)
    block_e = jnp.minimum(block_e, N_EXPERTS - 1)
    own = (block_e[:, None] == jnp.arange(N_EXPERTS, dtype=jnp.int32)[None, :]).astype(jnp.int32)

    def of_block(per_expert):
        return jnp.sum(own * per_expert[None, :], axis=1)

    within = blk - of_block(b_ends - e_blocks)
    c_start = of_block(starts) + within * EXPERT_ROWS
    n_valid = jnp.where(blk < n_active, jnp.clip(of_block(counts) - within * EXPERT_ROWS, 0, EXPERT_ROWS), 0)
    return block_e, n_active.reshape(1), n_valid, c_start, s_tok, s_dst


def kernel(x, c, ctx, c_ctx, w_ada, b_ada, norm1_g, norm2_g, w_in, ret_decay_f, ret_decay_b, ret_gn_g, sgu_ln_g, sgu_ln_b, sgu_w_s, sgu_b_s, w_out, w_router_group, b_router_group, w_router_expert, b_router_expert, w_gate, w_up, w_down, final_g):
    batch, n_tok, d = x.shape
    assert batch == 1 and w_ada.shape[0] == 1
    n_heads = ret_decay_f.shape[-1]
    n_groups = sgu_w_s.shape[1]
    ret_qk_w = n_heads * RET_DK
    ret_w = n_heads * RET_DV
    sgu_w = n_groups * SGU_GROUP_DIM
    k_off = ret_qk_w
    u_off = 2 * ret_qk_w + 2 * ret_w
    in_w = u_off + 2 * sgu_w
    top_k = 2
    assert w_in.shape == (1, d, in_w) and w_out.shape == (1, ret_w + sgu_w, d)

    cc = jnp.concatenate([c, c_ctx[None, :], jnp.zeros((MOD_ROWS - 2, d), F32)], axis=0)
    mod, h1 = _ada_norm(cc, w_ada[0], b_ada[0].reshape(1, N_MOD * d), x[0], norm1_g)

    lg_f = -jnp.exp(ret_decay_f[0])
    lg_b = -jnp.exp(ret_decay_b[0])

    hc = _norm_mod(ctx[0], norm1_g, mod, 1, 0, 1, ctx.shape[1])
    kv_c = _matmul([hc], w_in[0], ret_qk_w + ret_w, k_off, ctx.shape[1], MM_TN)
    s_f, s_b = _ctx_state(kv_c, lg_f, lg_b, n_heads)

    cos, sin_signed = _rope_tables(n_tok)
    p = _in_proj(h1, w_in[0], cos, sin_signed, sgu_ln_g, sgu_ln_b, ret_qk_w, 2 * ret_w, sgu_w, IN_TM, IN_TN)
    ret_out, sgu_out = _mixer(p, lg_f, lg_b, s_f, s_b, ret_gn_g, sgu_w_s[0],
                              sgu_b_s[0].reshape(n_groups, SGU_CHUNK, 1), n_heads, u_off)
    x1 = _matmul([ret_out, sgu_out], w_out[0], d, 0, IN_TM, IN_TN, residual=(x[0], mod, 2))

    n_router = MOE_GROUPS + N_EXPERTS
    wr = jnp.concatenate([w_router_group[0], w_router_expert[0],
                          jnp.zeros((d, V7X_LANES - n_router), F32)], axis=1)
    br = jnp.concatenate([b_router_group, b_router_expert,
                          jnp.zeros((1, V7X_LANES - n_router), F32)], axis=1)
    h2, eid, ew = _router(x1, norm2_g, mod, 3, 4, wr, br)
    plan = _dispatch(eid, n_tok, top_k)
    y = _experts(h2, w_gate[0], w_up[0], w_down[0], plan, top_k * n_tok)
    out = _combine(y.reshape(top_k, n_tok, d), ew, x1, mod, 5, final_g.reshape(1, d), top_k)
    return out.reshape(batch, n_tok, d)
```

```python
import functools

import jax
import jax.numpy as jnp
from jax import lax
from jax.experimental import pallas as pl
from jax.experimental.pallas import tpu as pltpu

F32 = jnp.float32
BF16 = jnp.bfloat16

GRID_W = 64
RET_DK = 256
RET_DV = 256
ROPE_BASE = 10000.0
SGU_GROUP_DIM = 256
SGU_CHUNK = 128
MOE_GROUPS = 8
EXPERTS_PER_GROUP = 8
N_EXPERTS = MOE_GROUPS * EXPERTS_PER_GROUP
N_MOD = 6
EPS = 1e-6

V7X_LANES = 128
BF16_SUBLANES = 16
V7X_VMEM_BYTES = 64 * 1024 * 1024
VMEM_UNSCOPED_BYTES = 4 * 1024 * 1024
VMEM_MIN_LIMIT_BYTES = 32 * 1024 * 1024
MOD_ROWS = 8

ADA_TN = 512
MM_TN = 512
IN_TM = 512
IN_TN = 1024
ROW_TILE = 512
COMBINE_TILE = 256
RET_TILE = 4096
RET_SUB = 512
EXPERT_ROWS = 512
EXPERT_SUBS = (320, 192)
DMA_UNROLL = 8
ROW_DMA_PRIORITY = 0
WEIGHT_DMA_PRIORITY = 1


def _vmem_limit(block_bytes, scratch_bytes):
    want = 2 * block_bytes + scratch_bytes
    return int(min(V7X_VMEM_BYTES - VMEM_UNSCOPED_BYTES, max(2 * want, VMEM_MIN_LIMIT_BYTES)))


def _nbytes(shape, dtype):
    n = 1
    for s in shape:
        n *= s
    return n * jnp.dtype(dtype).itemsize


def _rms_mod(x, g, shift, scale):
    y = x * lax.rsqrt(jnp.mean(x * x, axis=-1, keepdims=True) + EPS) * g
    return y * (1.0 + scale) + shift


def _ada_norm_kernel(cc_ref, w_ref, b_ref, x_ref, g_ref, mod_ref, h_ref, lead_ref, *, n_lead):
    j = pl.program_id(0)
    a = cc_ref[...]
    s = (a * jax.nn.sigmoid(a)).astype(BF16)
    m = jnp.dot(s, w_ref[...].astype(BF16), preferred_element_type=F32) + b_ref[...]
    mod_ref[...] = m

    @pl.when(j < n_lead)
    def _():
        lead_ref[j] = m

    @pl.when(j >= n_lead)
    def _():
        half = n_lead // 2
        shift = jnp.concatenate([lead_ref[t][0:1, :] for t in range(half)], axis=1)
        scale = jnp.concatenate([lead_ref[t][0:1, :] for t in range(half, n_lead)], axis=1)
        h_ref[...] = _rms_mod(x_ref[...], g_ref[...], shift, scale).astype(h_ref.dtype)


def _ada_norm(cc, w, b, x, g):
    d, n = w.shape
    m_rows = x.shape[0]
    n_tiles = n // ADA_TN
    n_lead = 2 * d // ADA_TN
    assert m_rows % (n_tiles - n_lead) == 0
    tm = m_rows // (n_tiles - n_lead)
    assert tm % BF16_SUBLANES == 0

    def row_idx(j):
        return (jnp.maximum(j - n_lead, 0), 0)

    blocks = (_nbytes((MOD_ROWS, d), F32) + _nbytes((d, ADA_TN), F32) + 2 * _nbytes((MOD_ROWS, ADA_TN), F32)
              + _nbytes((tm, d), F32) + _nbytes((tm, d), BF16))
    scratch = _nbytes((n_lead, MOD_ROWS, ADA_TN), F32) + _nbytes((d, ADA_TN), BF16)
    return pl.pallas_call(
        functools.partial(_ada_norm_kernel, n_lead=n_lead),
        grid=(n_tiles,),
        in_specs=[pl.BlockSpec((MOD_ROWS, d), lambda j: (0, 0)),
                  pl.BlockSpec((d, ADA_TN), lambda j: (0, j)),
                  pl.BlockSpec((1, ADA_TN), lambda j: (0, j)),
                  pl.BlockSpec((tm, d), row_idx),
                  pl.BlockSpec((1, d), lambda j: (0, 0))],
        out_specs=[pl.BlockSpec((MOD_ROWS, ADA_TN), lambda j: (0, j)),
                   pl.BlockSpec((tm, d), row_idx)],
        out_shape=[jax.ShapeDtypeStruct((MOD_ROWS, n), F32), jax.ShapeDtypeStruct((m_rows, d), BF16)],
        scratch_shapes=[pltpu.VMEM((n_lead, MOD_ROWS, ADA_TN), F32)],
        compiler_params=pltpu.CompilerParams(
            dimension_semantics=("arbitrary",),
            vmem_limit_bytes=_vmem_limit(blocks, scratch)),
        name="ada_norm",
    )(cc, w, b, x, g)


def _norm_mod_kernel(x_ref, g_ref, sh_ref, sc_ref, o_ref, *, row):
    h = _rms_mod(x_ref[...], g_ref[...], sh_ref[row:row + 1, :], sc_ref[row:row + 1, :])
    o_ref[...] = h.astype(o_ref.dtype)


def _norm_mod(x, g, mod, row, shift_chunk, scale_chunk, tm):
    m, d = x.shape
    blocks = _nbytes((tm, d), F32) * 2 + 3 * _nbytes((MOD_ROWS, d), F32)
    return pl.pallas_call(
        functools.partial(_norm_mod_kernel, row=row),
        grid=(m // tm,),
        in_specs=[pl.BlockSpec((tm, d), lambda i: (i, 0)),
                  pl.BlockSpec((1, d), lambda i: (0, 0)),
                  pl.BlockSpec((MOD_ROWS, d), lambda i: (0, shift_chunk)),
                  pl.BlockSpec((MOD_ROWS, d), lambda i: (0, scale_chunk))],
        out_specs=pl.BlockSpec((tm, d), lambda i: (i, 0)),
        out_shape=jax.ShapeDtypeStruct((m, d), BF16),
        compiler_params=pltpu.CompilerParams(
            dimension_semantics=("arbitrary",),
            vmem_limit_bytes=_vmem_limit(blocks, 0)),
        name="norm_mod",
    )(x, g, mod, mod)


def _matmul_acc(a_refs, w_ref, wbf_ref):
    @pl.when(pl.program_id(1) == 0)
    def _():
        wbf_ref[...] = w_ref[...].astype(BF16)

    acc = None
    k0 = 0
    for a_ref in a_refs:
        kk = a_ref.shape[1]
        part = jnp.dot(a_ref[...], wbf_ref[k0:k0 + kk, :], preferred_element_type=F32)
        acc = part if acc is None else acc + part
        k0 += kk
    return acc


def _matmul_kernel(*refs, n_a):
    w_ref, o_ref, wbf_ref = refs[n_a:]
    o_ref[...] = _matmul_acc(refs[:n_a], w_ref, wbf_ref).astype(o_ref.dtype)


def _matmul_res_kernel(*refs, n_a):
    w_ref, x_ref, g_ref, o_ref, wbf_ref = refs[n_a:]
    o_ref[...] = x_ref[...] + g_ref[0:1, :] * _matmul_acc(refs[:n_a], w_ref, wbf_ref)


def _matmul(a_list, w, n_cols, col_off, tm, tn, residual=None, out_dtype=F32):
    m = a_list[0].shape[0]
    k = sum(a.shape[1] for a in a_list)
    n_a = len(a_list)
    joff = col_off // tn
    grid = (n_cols // tn, m // tm)
    in_specs = [pl.BlockSpec((tm, a.shape[1]), lambda j, i: (i, 0)) for a in a_list]
    in_specs.append(pl.BlockSpec((k, tn), lambda j, i: (0, j + joff)))
    args = list(a_list) + [w]
    blocks = _nbytes((tm, k), BF16) + _nbytes((k, tn), F32) + _nbytes((tm, tn), F32)
    if residual is None:
        body = functools.partial(_matmul_kernel, n_a=n_a)
    else:
        x, mod, gate_chunk = residual
        goff = gate_chunk * (n_cols // tn)
        in_specs += [pl.BlockSpec((tm, tn), lambda j, i: (i, j)),
                     pl.BlockSpec((MOD_ROWS, tn), lambda j, i: (0, goff + j))]
        args += [x, mod]
        blocks += _nbytes((tm, tn), F32) + _nbytes((MOD_ROWS, tn), F32)
        body = functools.partial(_matmul_res_kernel, n_a=n_a)
    return pl.pallas_call(
        body,
        grid=grid,
        in_specs=in_specs,
        out_specs=pl.BlockSpec((tm, tn), lambda j, i: (i, j)),
        out_shape=jax.ShapeDtypeStruct((m, n_cols), out_dtype),
        scratch_shapes=[pltpu.VMEM((k, tn), BF16)],
        compiler_params=pltpu.CompilerParams(
            dimension_semantics=("arbitrary", "arbitrary"),
            vmem_limit_bytes=_vmem_limit(blocks, _nbytes((k, tn), BF16))),
        name="matmul_res" if residual is not None else "matmul",
    )(*args)


def _rope(t, cos, sin_signed):
    half = RET_DK // 2
    rot = jnp.concatenate([pltpu.roll(t[:, :half], half // 2, 1),
                           pltpu.roll(t[:, half:], half // 2, 1)], axis=1)
    return t * cos + rot * sin_signed


def _in_proj_kernel(a_ref, w_ref, cos_ref, sin_ref, lng_ref, lnb_ref, o_ref, wbf_ref, *, tile_ends):
    j = pl.program_id(0)
    q_end, k_end, plain_end, u_end = tile_ends

    def acc():
        return _matmul_acc([a_ref], w_ref, wbf_ref)

    @pl.when(j < k_end)
    def _rotated():
        t = acc() * jnp.where(j < q_end, 1.0, RET_DK ** -0.5)
        cos = cos_ref[...]
        sin = sin_ref[...]
        for hh in range(t.shape[1] // RET_DK):
            cols = slice(hh * RET_DK, (hh + 1) * RET_DK)
            o_ref[:, cols] = _rope(t[:, cols], cos, sin).astype(o_ref.dtype)

    @pl.when((j >= k_end) & (j < plain_end))
    def _plain():
        o_ref[...] = acc().astype(o_ref.dtype)

    @pl.when((j >= plain_end) & (j < u_end))
    def _gelu():
        o_ref[...] = jax.nn.gelu(acc()).astype(o_ref.dtype)

    @pl.when(j >= u_end)
    def _gelu_norm():
        t = jax.nn.gelu(acc())
        for gg in range(t.shape[1] // SGU_GROUP_DIM):
            cols = slice(gg * SGU_GROUP_DIM, (gg + 1) * SGU_GROUP_DIM)
            v32 = t[:, cols]
            mu = jnp.mean(v32, axis=-1, keepdims=True)
            var = jnp.mean(jnp.square(v32 - mu), axis=-1, keepdims=True)
            vn = (v32 - mu) * lax.rsqrt(var + EPS) * lng_ref[:, cols] + lnb_ref[:, cols]
            o_ref[:, cols] = vn.astype(o_ref.dtype)


def _in_proj(a, w, cos, sin_signed, ln_g, ln_b, qk_w, v_gate_w, sgu_w, tm, tn):
    m, k = a.shape
    n = w.shape[1]
    assert n == 2 * qk_w + v_gate_w + 2 * sgu_w
    assert qk_w % tn == 0 and v_gate_w % tn == 0 and sgu_w % tn == 0
    assert tn % RET_DK == 0 and tn % SGU_GROUP_DIM == 0
    q_end = qk_w // tn
    k_end = 2 * q_end
    plain_end = k_end + v_gate_w // tn
    u_end = plain_end + sgu_w // tn

    def table_idx(j, i):
        return (jnp.where(j < k_end, i, 0), 0)

    def ln_idx(j, i):
        return (0, jnp.maximum(j - u_end, 0))

    blocks = (_nbytes((tm, k), BF16) + _nbytes((k, tn), F32) + _nbytes((tm, tn), BF16)
              + 2 * _nbytes((tm, RET_DK), F32))
    return pl.pallas_call(
        functools.partial(_in_proj_kernel, tile_ends=(q_end, k_end, plain_end, u_end)),
        grid=(n // tn, m // tm),
        in_specs=[pl.BlockSpec((tm, k), lambda j, i: (i, 0)),
                  pl.BlockSpec((k, tn), lambda j, i: (0, j)),
                  pl.BlockSpec((tm, RET_DK), table_idx),
                  pl.BlockSpec((tm, RET_DK), table_idx),
                  pl.BlockSpec((1, tn), ln_idx),
                  pl.BlockSpec((1, tn), ln_idx)],
        out_specs=pl.BlockSpec((tm, tn), lambda j, i: (i, j)),
        out_shape=jax.ShapeDtypeStruct((m, n), BF16),
        scratch_shapes=[pltpu.VMEM((k, tn), BF16)],
        compiler_params=pltpu.CompilerParams(
            dimension_semantics=("arbitrary", "arbitrary"),
            vmem_limit_bytes=_vmem_limit(blocks, _nbytes((k, tn), BF16))),
        name="in_proj",
    )(a, w, cos, sin_signed, ln_g, ln_b)


def _ctx_state_kernel(lgf_ref, lgb_ref, k_ref, v_ref, sf_ref, sb_ref):
    h = pl.program_id(0)
    n = k_ref.shape[0]
    pos = lax.broadcasted_iota(jnp.int32, (n, 1), 0).astype(F32)
    k = k_ref[...] * (RET_DK ** -0.5)
    v = v_ref[...].astype(BF16)
    wf = jnp.exp((n - 1.0 - pos) * lgf_ref[h])
    wb = jnp.exp(pos * lgb_ref[h])
    tn_dims = (((0,), (0,)), ((), ()))
    sf_ref[0] = lax.dot_general((k * wf).astype(BF16), v, tn_dims, preferred_element_type=F32)
    sb_ref[0] = lax.dot_general((k * wb).astype(BF16), v, tn_dims, preferred_element_type=F32)


def _ctx_state(kv, lg_f, lg_b, n_heads):
    n = kv.shape[0]
    smem = pl.BlockSpec(memory_space=pltpu.SMEM)
    st = jax.ShapeDtypeStruct((n_heads, RET_DK, RET_DV), F32)
    return pl.pallas_call(
        _ctx_state_kernel,
        grid=(n_heads,),
        in_specs=[smem, smem,
                  pl.BlockSpec((n, RET_DK), lambda h: (0, h)),
                  pl.BlockSpec((n, RET_DV), lambda h: (0, n_heads + h))],
        out_specs=[pl.BlockSpec((1, RET_DK, RET_DV), lambda h: (h, 0, 0)),
                   pl.BlockSpec((1, RET_DK, RET_DV), lambda h: (h, 0, 0))],
        out_shape=[st, st],
        compiler_params=pltpu.CompilerParams(dimension_semantics=("arbitrary",)),
        name="ctx_state",
    )(lg_f, lg_b, kv, kv)


def _retention_kernel(lgf_ref, lgb_ref, q_ref, k_ref, v_ref, g_ref, s0f_ref, s0b_ref, gn_ref,
                      u_ref, vn_ref, ws_ref, bs_ref, o_ref, o2_ref, state_ref, yb_ref, *, n_steps):
    h = pl.program_id(0)
    p = pl.program_id(1)
    s = pl.program_id(2)
    c = RET_SUB
    n_sub = q_ref.shape[0] // c
    ii = lax.broadcasted_iota(jnp.int32, (c, c), 0)
    jj = lax.broadcasted_iota(jnp.int32, (c, c), 1)
    idx = lax.broadcasted_iota(jnp.int32, (c, 1), 0).astype(F32)
    nt_dims = (((1,), (1,)), ((), ()))
    tn_dims = (((0,), (0,)), ((), ()))

    def chunk(ci, decay, q_decay, k_decay, chunk_decay):
        rows = pl.ds(ci * c, c)
        qb = q_ref[rows, :]
        kb = k_ref[rows, :]
        v = v_ref[rows, :]
        st = state_ref[...]
        scores = lax.dot_general(qb, kb, nt_dims, preferred_element_type=F32) * decay
        out = (jnp.dot(scores.astype(BF16), v, preferred_element_type=F32)
               + jnp.dot(qb, st.astype(BF16), preferred_element_type=F32) * q_decay)
        state_ref[...] = st * chunk_decay + lax.dot_general(
            (kb.astype(F32) * k_decay).astype(BF16), v, tn_dims, preferred_element_type=F32)
        return out

    @pl.when(p == 0)
    def _backward():
        lg = lgb_ref[h]
        blk = n_steps - 1 - s

        @pl.when(s == 0)
        def _():
            state_ref[...] = s0b_ref[0]

        mask = jj > ii
        decay = jnp.where(mask, jnp.exp(jnp.where(mask, jj - ii, 0).astype(F32) * lg), 0.0)
        q_decay = jnp.exp((c - idx) * lg)
        k_decay = jnp.exp(idx * lg)
        chunk_decay = jnp.exp(jnp.full((1, RET_DV), c, F32) * lg)
        for ci in reversed(range(n_sub)):
            out = chunk(ci, decay, q_decay, k_decay, chunk_decay)
            start = pl.multiple_of(blk * (n_sub * c) + ci * c, c)
            yb_ref[pl.ds(start, c), :] = out

    @pl.when(p == 1)
    def _forward():
        lg = lgf_ref[h]

        @pl.when(s == 0)
        def _():
            state_ref[...] = s0f_ref[0]

        mask = ii >= jj
        decay = jnp.where(mask, jnp.exp(jnp.where(mask, ii - jj, 0).astype(F32) * lg), 0.0)
        q_decay = jnp.exp((idx + 1.0) * lg)
        k_decay = jnp.exp((c - 1.0 - idx) * lg)
        chunk_decay = jnp.exp(jnp.full((1, RET_DV), c, F32) * lg)
        for ci in range(n_sub):
            out = chunk(ci, decay, q_decay, k_decay, chunk_decay)
            start = pl.multiple_of(s * (n_sub * c) + ci * c, c)
            y = out + yb_ref[pl.ds(start, c), :]
            mu = jnp.mean(y, axis=-1, keepdims=True)
            var = jnp.mean(jnp.square(y - mu), axis=-1, keepdims=True)
            yn = (y - mu) * lax.rsqrt(var + EPS) * gn_ref[...]
            gate = g_ref[pl.ds(ci * c, c), :].astype(F32)
            o_ref[pl.ds(ci * c, c), :] = (gate * jax.nn.sigmoid(gate) * yn).astype(o_ref.dtype)

        ws = ws_ref[0].astype(BF16)
        bs = bs_ref[0]
        for ci in range(u_ref.shape[0] // SGU_CHUNK):
            rows = pl.ds(ci * SGU_CHUNK, SGU_CHUNK)
            mixed = jnp.dot(ws, vn_ref[rows, :], preferred_element_type=F32) + bs
            o2_ref[rows, :] = (u_ref[rows, :].astype(F32) * mixed).astype(o2_ref.dtype)


def _mixer(p, lg_f, lg_b, s0_f, s0_b, gn_g, w_s, b_s, n_heads, u_off):
    n_tok = p.shape[0]
    tl = RET_TILE
    n_steps = n_tok // tl
    ub = u_off // SGU_GROUP_DIM
    assert w_s.shape[0] == n_heads and RET_DV == SGU_GROUP_DIM and tl % RET_SUB == 0 and tl % SGU_CHUNK == 0
    smem = pl.BlockSpec(memory_space=pltpu.SMEM)

    def seq_blk(p_, s_):
        return jnp.where(p_ == 0, n_steps - 1 - s_, s_)

    def fwd_blk(col):
        return pl.BlockSpec((tl, RET_DV), lambda h, p_, s_: (s_ * p_, col + h))

    blocks = 8 * _nbytes((tl, RET_DK), BF16) + 2 * _nbytes((RET_DK, RET_DV), F32)
    scratch = _nbytes((RET_DK, RET_DV), F32) + _nbytes((n_tok, RET_DV), F32)
    out = jax.ShapeDtypeStruct((n_tok, n_heads * RET_DV), BF16)
    return pl.pallas_call(
        functools.partial(_retention_kernel, n_steps=n_steps),
        grid=(n_heads, 2, n_steps),
        in_specs=[smem, smem,
                  pl.BlockSpec((tl, RET_DK), lambda h, p_, s_: (seq_blk(p_, s_), h)),
                  pl.BlockSpec((tl, RET_DK), lambda h, p_, s_: (seq_blk(p_, s_), n_heads + h)),
                  pl.BlockSpec((tl, RET_DV), lambda h, p_, s_: (seq_blk(p_, s_), 2 * n_heads + h)),
                  fwd_blk(3 * n_heads),
                  pl.BlockSpec((1, RET_DK, RET_DV), lambda h, p_, s_: (h, 0, 0)),
                  pl.BlockSpec((1, RET_DK, RET_DV), lambda h, p_, s_: (h, 0, 0)),
                  pl.BlockSpec((1, RET_DV), lambda h, p_, s_: (0, h)),
                  fwd_blk(ub),
                  fwd_blk(ub + n_heads),
                  pl.BlockSpec((1, SGU_CHUNK, SGU_CHUNK), lambda h, p_, s_: (h, 0, 0)),
                  pl.BlockSpec((1, SGU_CHUNK, 1), lambda h, p_, s_: (h, 0, 0))],
        out_specs=[fwd_blk(0), fwd_blk(0)],
        out_shape=[out, out],
        scratch_shapes=[pltpu.VMEM((RET_DK, RET_DV), F32), pltpu.VMEM((n_tok, RET_DV), F32)],
        compiler_params=pltpu.CompilerParams(
            dimension_semantics=("arbitrary", "arbitrary", "arbitrary"),
            vmem_limit_bytes=_vmem_limit(blocks, scratch)),
        name="mixer",
    )(lg_f, lg_b, p, p, p, p, s0_f, s0_b, gn_g, p, p, w_s, b_s)


def _split_bf16(t):
    hi = t.astype(BF16)
    lo = (t - hi.astype(F32)).astype(BF16)
    return hi, lo


def _router_kernel(x_ref, g_ref, sh_ref, sc_ref, wr_ref, br_ref, h_ref, eid_ref, ew_ref):
    h = _rms_mod(x_ref[...], g_ref[...], sh_ref[0:1, :], sc_ref[0:1, :])
    half = h.shape[1] // 2
    bits = lax.bitcast_convert_type(h.astype(BF16).astype(F32), jnp.uint32)
    h_ref[...] = (bits[:, half:] & jnp.uint32(0xFFFF0000)) | (bits[:, :half] >> 16)
    h_hi, h_lo = _split_bf16(h)
    w_hi, w_lo = _split_bf16(wr_ref[...])
    logits = (jnp.dot(h_hi, w_hi, preferred_element_type=F32)
              + jnp.dot(h_lo, w_hi, preferred_element_type=F32)
              + jnp.dot(h_hi, w_lo, preferred_element_type=F32)) + br_ref[...]
    lane = lax.broadcasted_iota(jnp.int32, logits.shape, 1)
    lane_f = lane.astype(F32)
    neg = -jnp.inf

    def first_lane(hit):
        return jnp.min(jnp.where(hit, lane_f, float(V7X_LANES)), axis=-1, keepdims=True).astype(jnp.int32)

    gl = jnp.where(lane < MOE_GROUPS, logits, neg)
    g_max = jnp.max(gl, axis=-1, keepdims=True)
    g_sel = first_lane(gl == g_max)
    p_g = 1.0 / jnp.sum(jnp.exp(gl - g_max), axis=-1, keepdims=True)
    e_lo = MOE_GROUPS + g_sel * EXPERTS_PER_GROUP
    el = jnp.where((lane >= e_lo) & (lane < e_lo + EXPERTS_PER_GROUP), logits, neg)
    v1 = jnp.max(el, axis=-1, keepdims=True)
    i1 = first_lane(el == v1)
    el2 = jnp.where(lane == i1, neg, el)
    v2 = jnp.max(el2, axis=-1, keepdims=True)
    i2 = first_lane(el2 == v2)
    e2 = jnp.exp(v2 - v1)
    den = 1.0 + e2
    w1 = p_g * (1.0 / den)
    w2 = p_g * (e2 / den)
    eid_ref[...] = jnp.where(lane == 0, i1 - MOE_GROUPS, jnp.where(lane == 1, i2 - MOE_GROUPS, 0))
    ew_ref[...] = jnp.where(lane == 0, w1, jnp.where(lane == 1, w2, 0.0))


def _router(x, g, mod, shift_chunk, scale_chunk, wr, br):
    m, d = x.shape
    tm = ROW_TILE
    blocks = 2 * _nbytes((tm, d), F32) + 2 * _nbytes((MOD_ROWS, d), F32) + _nbytes((d, V7X_LANES), F32) \
        + 2 * _nbytes((tm, V7X_LANES), F32)
    return pl.pallas_call(
        _router_kernel,
        grid=(m // tm,),
        in_specs=[pl.BlockSpec((tm, d), lambda i: (i, 0)),
                  pl.BlockSpec((1, d), lambda i: (0, 0)),
                  pl.BlockSpec((MOD_ROWS, d), lambda i: (0, shift_chunk)),
                  pl.BlockSpec((MOD_ROWS, d), lambda i: (0, scale_chunk)),
                  pl.BlockSpec((d, V7X_LANES), lambda i: (0, 0)),
                  pl.BlockSpec((1, V7X_LANES), lambda i: (0, 0))],
        out_specs=[pl.BlockSpec((tm, d // 2), lambda i: (i, 0)),
                   pl.BlockSpec((tm, V7X_LANES), lambda i: (i, 0)),
                   pl.BlockSpec((tm, V7X_LANES), lambda i: (i, 0))],
        out_shape=[jax.ShapeDtypeStruct((m, d // 2), jnp.uint32),
                   jax.ShapeDtypeStruct((m, V7X_LANES), jnp.int32),
                   jax.ShapeDtypeStruct((m, V7X_LANES), F32)],
        compiler_params=pltpu.CompilerParams(
            dimension_semantics=("arbitrary",),
            vmem_limit_bytes=_vmem_limit(blocks, 0)),
        name="router",
    )(x, g, mod, mod, wr, br)


def _row_copy(src, src_row, dst, dst_row, sem):
    return pltpu.make_async_copy(src.at[pl.ds(src_row, 1), :], dst.at[pl.ds(dst_row, 1), :], sem)


def _for_rows(n_rows, fn):
    n_groups = lax.shift_right_logical(n_rows, DMA_UNROLL.bit_length() - 1)

    def group(gi, carry):
        for u in range(DMA_UNROLL):
            fn(gi * DMA_UNROLL + u)
        return carry

    def single(r, carry):
        fn(r)
        return carry

    lax.fori_loop(0, n_groups, group, 0)
    lax.fori_loop(n_groups * DMA_UNROLL, n_rows, single, 0)


def _experts_kernel(be_ref, nact_ref, nv_ref, cs_ref, stok_ref, sdst_ref,
                    h_hbm, wg_hbm, wu_hbm, wd_hbm, y_hbm,
                    wg_ref, wu_ref, wd_ref, x_ref, acc_ref, gate_ref, hid_ref, sem_w, sem_in, sem_out):
    n_active = nact_ref[0]
    weights = ((wg_hbm, wg_ref), (wu_hbm, wu_ref), (wd_hbm, wd_ref))

    def weight_copy(k, bb):
        src, dst = weights[k]
        return pltpu.make_async_copy(src.at[be_ref[bb]], dst, sem_w.at[k])

    def wait_rows(src, dst, n, sem):
        n_whole = pl.multiple_of(lax.shift_right_logical(n, DMA_UNROLL.bit_length() - 1) * DMA_UNROLL, DMA_UNROLL)

        @pl.when(n_whole > 0)
        def _():
            rows = pl.ds(0, n_whole)
            pltpu.make_async_copy(src.at[rows, :], dst.at[rows, :], sem).wait()

        def single(r, carry):
            _row_copy(src, r, dst, r, sem).wait()
            return carry

        lax.fori_loop(n_whole, n, single, 0)

    def gather(bb, slot_, wait):
        def one(r):
            _row_copy(h_hbm, stok_ref[cs_ref[bb] + r], x_ref.at[slot_], r,
                      sem_in.at[slot_]).start(priority=ROW_DMA_PRIORITY)

        if wait:
            wait_rows(h_hbm, x_ref.at[slot_], nv_ref[bb], sem_in.at[slot_])
        else:
            _for_rows(nv_ref[bb], one)

    def scatter(bb, wait):
        par = lax.rem(bb, 2)

        def one(r):
            _row_copy(acc_ref.at[par], r, y_hbm, sdst_ref[cs_ref[bb] + r],
                      sem_out.at[par]).start(priority=ROW_DMA_PRIORITY)

        if wait:
            wait_rows(acc_ref.at[par], y_hbm, nv_ref[bb], sem_out.at[par])
        else:
            _for_rows(nv_ref[bb], one)

    half = x_ref.shape[-1]
    sub_starts = [sum(EXPERT_SUBS[:j]) for j in range(len(EXPERT_SUBS))]

    def for_sub_blocks(bb, fn):
        fn(pl.ds(sub_starts[0], EXPERT_SUBS[0]))
        for start, size in zip(sub_starts[1:], EXPERT_SUBS[1:]):
            @pl.when(start < nv_ref[bb])
            def _():
                fn(pl.ds(start, size))

    def x_times(slot, rows, w_ref):
        xw = x_ref[slot, rows, :]
        xa = lax.bitcast_convert_type(xw << 16, F32).astype(BF16)
        xb = lax.bitcast_convert_type(xw & jnp.uint32(0xFFFF0000), F32).astype(BF16)
        return (jnp.dot(xa, w_ref[:half, :].astype(BF16), preferred_element_type=F32)
                + jnp.dot(xb, w_ref[half:, :].astype(BF16), preferred_element_type=F32))

    def block(b, carry):
        slot = lax.rem(b, 2)
        has_next = b + 1 < n_active

        def refill(k):
            @pl.when(has_next)
            def _():
                weight_copy(k, b + 1).start(priority=WEIGHT_DMA_PRIORITY)

        @pl.when(has_next)
        def _():
            gather(b + 1, 1 - slot, False)

        gather(b, slot, True)

        weight_copy(0, b).wait()

        def gate_phase(rows):
            gate_ref[rows, :] = x_times(slot, rows, wg_ref)

        for_sub_blocks(b, gate_phase)
        refill(0)

        weight_copy(1, b).wait()

        def up_phase(rows):
            gate = gate_ref[rows, :]
            hid_ref[rows, :] = (gate * jax.nn.sigmoid(gate) * x_times(slot, rows, wu_ref)).astype(BF16)

        for_sub_blocks(b, up_phase)
        refill(1)

        weight_copy(2, b).wait()

        def down_phase(rows):
            acc_ref[slot, rows, :] = jnp.dot(hid_ref[rows, :], wd_ref[...].astype(BF16),
                                             preferred_element_type=F32)

        for_sub_blocks(b, down_phase)
        refill(2)

        @pl.when(b > 0)
        def _():
            scatter(b - 1, True)

        scatter(b, False)
        return carry

    x_ref[...] = jnp.zeros(x_ref.shape, x_ref.dtype)
    gather(0, 0, False)
    for k in range(len(weights)):
        weight_copy(k, 0).start(priority=WEIGHT_DMA_PRIORITY)
    lax.fori_loop(0, n_active, block, 0)
    scatter(n_active - 1, True)


def _experts(h, w_gate, w_up, w_down, plan, n_out_rows):
    block_e, n_active, n_valid, c_start, s_tok, s_dst = plan
    d = w_gate.shape[1]
    de = w_gate.shape[-1]
    assert h.shape[1] * 2 == d and sum(EXPERT_SUBS) == EXPERT_ROWS

    any_space = pl.BlockSpec(memory_space=pl.ANY)
    scratch = (3 * _nbytes((d, de), F32) + _nbytes((2, EXPERT_ROWS, d // 2), jnp.uint32)
               + _nbytes((2, EXPERT_ROWS, d), F32) + _nbytes((EXPERT_ROWS, de), F32) + _nbytes((EXPERT_ROWS, de), BF16))
    grid_spec = pltpu.PrefetchScalarGridSpec(
        num_scalar_prefetch=6,
        grid=(1,),
        in_specs=[any_space, any_space, any_space, any_space],
        out_specs=any_space,
        scratch_shapes=[pltpu.VMEM((d, de), F32), pltpu.VMEM((d, de), F32), pltpu.VMEM((de, d), F32),
                        pltpu.VMEM((2, EXPERT_ROWS, d // 2), jnp.uint32),
                        pltpu.VMEM((2, EXPERT_ROWS, d), F32),
                        pltpu.VMEM((EXPERT_ROWS, de), F32), pltpu.VMEM((EXPERT_ROWS, de), BF16),
                        pltpu.SemaphoreType.DMA((3,)), pltpu.SemaphoreType.DMA((2,)),
                        pltpu.SemaphoreType.DMA((2,))],
    )
    return pl.pallas_call(
        _experts_kernel,
        grid_spec=grid_spec,
        out_shape=jax.ShapeDtypeStruct((n_out_rows, d), F32),
        compiler_params=pltpu.CompilerParams(
            dimension_semantics=("arbitrary",),
            vmem_limit_bytes=_vmem_limit(0, scratch)),
        name="experts",
    )(block_e, n_active, n_valid, c_start, s_tok, s_dst, h, w_gate, w_up, w_down)


def _combine_kernel(*refs, top_k):
    y_refs = refs[:top_k]
    w_ref, x_ref, g_ref, fg_ref, o_ref = refs[top_k:]
    y = y_refs[0][0] * w_ref[:, 0:1]
    for k in range(1, top_k):
        y = y + y_refs[k][0] * w_ref[:, k:k + 1]
    x = x_ref[...] + g_ref[0:1, :] * y
    o_ref[...] = x * lax.rsqrt(jnp.mean(x * x, axis=-1, keepdims=True) + EPS) * fg_ref[...]


def _combine(y, ew, x, mod, gate_chunk, final_g, top_k):
    m, d = x.shape
    tm = COMBINE_TILE
    blocks = (top_k + 2) * _nbytes((tm, d), F32) + _nbytes((MOD_ROWS, d), F32) + _nbytes((tm, V7X_LANES), F32)
    in_specs = [pl.BlockSpec((1, tm, d), functools.partial(lambda k, i: (k, i, 0), k)) for k in range(top_k)]
    in_specs += [pl.BlockSpec((tm, V7X_LANES), lambda i: (i, 0)),
                 pl.BlockSpec((tm, d), lambda i: (i, 0)),
                 pl.BlockSpec((MOD_ROWS, d), lambda i: (0, gate_chunk)),
                 pl.BlockSpec((1, d), lambda i: (0, 0))]
    return pl.pallas_call(
        functools.partial(_combine_kernel, top_k=top_k),
        grid=(m // tm,),
        in_specs=in_specs,
        out_specs=pl.BlockSpec((tm, d), lambda i: (i, 0)),
        out_shape=jax.ShapeDtypeStruct((m, d), F32),
        compiler_params=pltpu.CompilerParams(
            dimension_semantics=("arbitrary",),
            vmem_limit_bytes=_vmem_limit(blocks, 0)),
        name="combine",
    )(*([y] * top_k), ew, x, mod, final_g)


def _rope_tables(n_tokens):
    n_rows = n_tokens // GRID_W
    n_freq = RET_DK // 4
    freqs = ROPE_BASE ** (-jnp.arange(n_freq, dtype=F32) / n_freq)
    sign = jnp.concatenate([-jnp.ones((n_freq,), F32), jnp.ones((n_freq,), F32)])

    def half_tables(n_pos):
        ang = jnp.arange(n_pos, dtype=F32)[:, None] * freqs
        return jnp.tile(jnp.cos(ang), (1, 2)), jnp.tile(jnp.sin(ang), (1, 2)) * sign

    cos_r, sin_r = half_tables(n_rows)
    cos_c, sin_c = half_tables(GRID_W)

    def expand(by_row, by_col):
        by_row = jnp.broadcast_to(by_row[:, None, :], (n_rows, GRID_W, 2 * n_freq))
        by_col = jnp.broadcast_to(by_col[None, :, :], (n_rows, GRID_W, 2 * n_freq))
        return jnp.concatenate([by_row, by_col], axis=-1).reshape(n_tokens, RET_DK)

    return expand(cos_r, cos_c), expand(sin_r, sin_c)


def _dispatch(eid, n_tok, top_k):
    m = n_tok * top_k
    n_blocks = -(-m // EXPERT_ROWS) + N_EXPERTS
    e_flat = eid[:, :top_k].reshape(-1)
    order = jnp.argsort(e_flat).astype(jnp.int32)
    s_tok = order // top_k
    s_dst = (order % top_k) * n_tok + s_tok
    counts = jnp.sum((e_flat[:, None] == jnp.arange(N_EXPERTS, dtype=jnp.int32)[None, :]).astype(jnp.int32), axis=0)
    starts = jnp.cumsum(counts) - counts
    e_blocks = (counts + EXPERT_ROWS - 1) // EXPERT_ROWS
    b_ends = jnp.cumsum(e_blocks)
    n_active = b_ends[-1].astype(jnp.int32)
    blk = jnp.arange(n_blocks, dtype=jnp.int32)
    block_e = jnp.sum((jnp.minimum(blk, n_active - 1)[:, None] >= b_ends[None, :]).astype(jnp.int32), axis=1)
    block_e = jnp.minimum(block_e, N_EXPERTS - 1)
    own = (block_e[:, None] == jnp.arange(N_EXPERTS, dtype=jnp.int32)[None, :]).astype(jnp.int32)

    def of_block(per_expert):
        return jnp.sum(own * per_expert[None, :], axis=1)

    within = blk - of_block(b_ends - e_blocks)
    c_start = of_block(starts) + within * EXPERT_ROWS
    n_valid = jnp.where(blk < n_active, jnp.clip(of_block(counts) - within * EXPERT_ROWS, 0, EXPERT_ROWS), 0)
    return block_e, n_active.reshape(1), n_valid, c_start, s_tok, s_dst


def kernel(x, c, ctx, c_ctx, w_ada, b_ada, norm1_g, norm2_g, w_in, ret_decay_f, ret_decay_b, ret_gn_g, sgu_ln_g, sgu_ln_b, sgu_w_s, sgu_b_s, w_out, w_router_group, b_router_group, w_router_expert, b_router_expert, w_gate, w_up, w_down, final_g):
    batch, n_tok, d = x.shape
    assert batch == 1 and w_ada.shape[0] == 1
    n_heads = ret_decay_f.shape[-1]
    n_groups = sgu_w_s.shape[1]
    ret_qk_w = n_heads * RET_DK
    ret_w = n_heads * RET_DV
    sgu_w = n_groups * SGU_GROUP_DIM
    k_off = ret_qk_w
    u_off = 2 * ret_qk_w + 2 * ret_w
    in_w = u_off + 2 * sgu_w
    top_k = 2
    assert w_in.shape == (1, d, in_w) and w_out.shape == (1, ret_w + sgu_w, d)

    cc = jnp.concatenate([c, c_ctx[None, :], jnp.zeros((MOD_ROWS - 2, d), F32)], axis=0)
    mod, h1 = _ada_norm(cc, w_ada[0], b_ada[0].reshape(1, N_MOD * d), x[0], norm1_g)

    lg_f = -jnp.exp(ret_decay_f[0])
    lg_b = -jnp.exp(ret_decay_b[0])

    hc = _norm_mod(ctx[0], norm1_g, mod, 1, 0, 1, ctx.shape[1])
    kv_c = _matmul([hc], w_in[0], ret_qk_w + ret_w, k_off, ctx.shape[1], MM_TN)
    s_f, s_b = _ctx_state(kv_c, lg_f, lg_b, n_heads)

    cos, sin_signed = _rope_tables(n_tok)
    p = _in_proj(h1, w_in[0], cos, sin_signed, sgu_ln_g, sgu_ln_b, ret_qk_w, 2 * ret_w, sgu_w, IN_TM, IN_TN)
    ret_out, sgu_out = _mixer(p, lg_f, lg_b, s_f, s_b, ret_gn_g, sgu_w_s[0],
                              sgu_b_s[0].reshape(n_groups, SGU_CHUNK, 1), n_heads, u_off)
    x1 = _matmul([ret_out, sgu_out], w_out[0], d, 0, IN_TM, IN_TN, residual=(x[0], mod, 2))

    n_router = MOE_GROUPS + N_EXPERTS
    wr = jnp.concatenate([w_router_group[0], w_router_expert[0],
                          jnp.zeros((d, V7X_LANES - n_router), F32)], axis=1)
    br = jnp.concatenate([b_router_group, b_router_expert,
                          jnp.zeros((1, V7X_LANES - n_router), F32)], axis=1)
    h2, eid, ew = _router(x1, norm2_g, mod, 3, 4, wr, br)
    plan = _dispatch(eid, n_tok, top_k)
    y = _experts(h2, w_gate[0], w_up[0], w_down[0], plan, top_k * n_tok)
    out = _combine(y.reshape(top_k, n_tok, d), ew, x1, mod, 5, final_g.reshape(1, d), top_k)
    return out.reshape(batch, n_tok, d)
```
